```python
import math
import jax, jax.numpy as jnp
from jax import lax
import numpy as np

D_MODEL = 2048
BATCH = 2
SEQ = 4096
DEPTH = 1

HEAD_DIM = 128
N_HEADS_A = 8
WIDTH_A = N_HEADS_A * HEAD_DIM
DILATED_PATTERNS = ((128, 1), (512, 4), (2048, 16))
N_HEADS_B = 4
DIFF_HEAD_DIM = 128
WIDTH_B = N_HEADS_B * 2 * DIFF_HEAD_DIM
IN_SPLITS = (WIDTH_A, WIDTH_A, WIDTH_A, WIDTH_B, WIDTH_B, WIDTH_B, D_MODEL, D_MODEL)
IN_WIDTH = sum(IN_SPLITS)
N_EXPERTS = 32
TOP_K = 4
D_FF = D_MODEL
SWIGLU_LIMIT = 7.0
SWIGLU_ALPHA = 1.702
MOE_BLOCK = 128
Q_BLOCK = 128
N_MOD = 6
EPS = 1e-5
NEG = -1e30

kernel_name = "hybrid_dilated_diffattn_moe_encoder"


def _alibi_slopes(n):
    return np.array([2.0 ** (-8.0 * (i + 1) / n) for i in range(n)], dtype=np.float32)


def rmsnorm(x, g):
    xf = x.astype(jnp.float32)
    y = xf * lax.rsqrt(jnp.mean(xf * xf, axis=-1, keepdims=True) + EPS)
    return (y * g.astype(jnp.float32)).astype(x.dtype)


def _dilated_pattern(q, k, v, window, dilation, slopes):
    B, S, H, Dh = q.shape
    d = dilation
    R = window // (2 * d)
    L = S // d
    nb = -(-L // R)
    Lp = nb * R

    def to_sub(t):
        t = t.reshape(B, L, d, H, Dh).transpose(0, 2, 1, 3, 4)
        t = jnp.pad(t, ((0, 0), (0, 0), (0, Lp - L), (0, 0), (0, 0)))
        return t.reshape(B, d, nb, R, H, Dh)

    def band(t):
        tp = jnp.pad(t, ((0, 0), (0, 0), (1, 1), (0, 0), (0, 0), (0, 0)))
        return jnp.concatenate([tp[:, :, :-2], tp[:, :, 1:-1], tp[:, :, 2:]], axis=3)

    qs = to_sub(q)
    kw = band(to_sub(k))
    vw = band(to_sub(v))
    qi = np.arange(nb)[:, None, None] * R + np.arange(R)[None, :, None]
    kj = (np.arange(nb)[:, None, None] - 1) * R + np.arange(3 * R)[None, None, :]
    rel = kj - qi
    valid = (np.abs(rel) <= R) & (kj >= 0) & (kj < L)
    dist = (np.abs(rel) * d).astype(np.float32)
    bias = -slopes[None, :, None, None] * dist[:, None]
    s = jnp.einsum('brnqhe,brnkhe->brnhqk', qs, kw).astype(jnp.float32) * (Dh ** -0.5) + bias
    s = jnp.where(valid[:, None], s, NEG)
    lse = jax.nn.logsumexp(s, axis=-1)
    p = jnp.exp(s - lse[..., None])
    o = jnp.einsum('brnhqk,brnkhe->brnqhe', p.astype(v.dtype), vw)
    o = o.reshape(B, d, Lp, H, Dh)[:, :, :L].transpose(0, 2, 1, 3, 4).reshape(B, S, H, Dh)
    lse = lse.transpose(0, 1, 2, 4, 3).reshape(B, d, Lp, H)[:, :, :L]
    lse = lse.transpose(0, 2, 1, 3).reshape(B, S, H)
    return o, lse


def dilated_attention(q, k, v, slopes):
    outs, lses = [], []
    for window, dilation in DILATED_PATTERNS:
        o, l = _dilated_pattern(q, k, v, window, dilation, slopes)
        outs.append(o)
        lses.append(l)
    o = jnp.stack(outs, axis=0)
    w = jax.nn.softmax(jnp.stack(lses, axis=0), axis=0)
    return jnp.einsum('pbsh,pbshe->bshe', w.astype(o.dtype), o)


def diff_attention(q, k, v, lam_q1, lam_k1, lam_q2, lam_k2, subln_g, slopes, lambda_init):
    B, S, H, _, Dh = q.shape
    f32 = jnp.float32
    lam = (jnp.exp(jnp.sum(lam_q1.astype(f32) * lam_k1.astype(f32)))
           - jnp.exp(jnp.sum(lam_q2.astype(f32) * lam_k2.astype(f32))) + lambda_init)
    nblk = S // Q_BLOCK
    kpos = jnp.arange(S)
    qb = q.reshape(B, nblk, Q_BLOCK, H, 2, Dh).transpose(1, 0, 2, 3, 4, 5)

    def block(args):
        i, qblk = args
        qpos = i * Q_BLOCK + jnp.arange(Q_BLOCK)
        dist = jnp.abs(qpos[:, None] - kpos[None, :]).astype(f32)
        bias = -slopes[:, None, None] * dist[None]
        s = jnp.einsum('bqhcd,bkhcd->bchqk', qblk, k).astype(f32) * (Dh ** -0.5) + bias[None, None]
        p = jax.nn.softmax(s, axis=-1)
        a = p[:, 0] - lam * p[:, 1]
        return jnp.einsum('bhqk,bkhe->bqhe', a.astype(v.dtype), v)

    o = lax.map(block, (jnp.arange(nblk), qb))
    o = o.transpose(1, 0, 2, 3, 4).reshape(B, S, H, 2 * Dh)
    return rmsnorm(o, subln_g) * (1.0 - lambda_init)


def moe(h, w_router, b_router, w_gate, b_gate, w_up, b_up, w_down, b_down):
    B, S, D = h.shape
    T = B * S
    TK = T * TOP_K
    xt = h.reshape(T, D)
    logits = (xt @ w_router + b_router).astype(jnp.float32)
    top_vals, top_idx = lax.top_k(logits, TOP_K)
    probs = jax.nn.softmax(top_vals, axis=-1)
    flat_e = top_idx.reshape(-1)
    flat_tok = jnp.repeat(jnp.arange(T, dtype=jnp.int32), TOP_K)
    order = jnp.argsort(flat_e)
    se = flat_e[order]
    stok = flat_tok[order]
    counts = jnp.bincount(flat_e, length=N_EXPERTS)
    starts = jnp.cumsum(counts) - counts
    padded = (counts + MOE_BLOCK - 1) // MOE_BLOCK * MOE_BLOCK
    pends = jnp.cumsum(padded)
    pstarts = pends - padded
    dest = pstarts[se] + (jnp.arange(TK) - starts[se])
    nblk = -(-TK // MOE_BLOCK) + N_EXPERTS
    buf_tok = jnp.full((nblk * MOE_BLOCK,), T, dtype=jnp.int32).at[dest].set(stok)
    block_expert = jnp.clip(jnp.searchsorted(pends, jnp.arange(nblk) * MOE_BLOCK, side='right'),
                            0, N_EXPERTS - 1)
    x_pad = jnp.concatenate([xt, jnp.zeros((1, D), xt.dtype)], axis=0)

    def block(args):
        e, toks = args
        xb = x_pad[toks]
        g = jnp.minimum(xb @ w_gate[e] + b_gate[e], SWIGLU_LIMIT)
        u = jnp.clip(xb @ w_up[e] + b_up[e], -SWIGLU_LIMIT, SWIGLU_LIMIT)
        y = (u + 1.0) * (g * jax.nn.sigmoid(SWIGLU_ALPHA * g))
        return y @ w_down[e] + b_down[e]

    ybuf = lax.map(block, (block_expert, buf_tok.reshape(nblk, MOE_BLOCK))).reshape(-1, D)
    y_sorted = ybuf[dest]
    y_slots = jnp.zeros_like(y_sorted).at[order].set(y_sorted).reshape(T, TOP_K, D)
    out = jnp.einsum('tk,tkd->td', probs.astype(y_slots.dtype), y_slots)
    return out.reshape(B, S, D).astype(h.dtype)


def setup_inputs(seed: int = 0) -> dict:
    key = jax.random.key(seed)
    ks = jax.random.split(key, 24)

    def nrm(k, shape, scale):
        return jax.random.normal(k, shape, jnp.float32) * scale

    L = DEPTH
    return {
        "x": nrm(ks[0], (BATCH, SEQ, D_MODEL), 1.0),
        "c": nrm(ks[1], (BATCH, D_MODEL), 1.0),
        "w_ada": nrm(ks[2], (L, D_MODEL, N_MOD * D_MODEL), 0.5 * D_MODEL ** -0.5),
        "b_ada": nrm(ks[3], (L, N_MOD * D_MODEL), 0.02),
        "norm1_g": 1.0 + nrm(ks[4], (L, D_MODEL), 0.02),
        "w_in": nrm(ks[5], (L, D_MODEL, IN_WIDTH), D_MODEL ** -0.5),
        "lam_q1": nrm(ks[6], (L, DIFF_HEAD_DIM), 0.1),
        "lam_k1": nrm(ks[7], (L, DIFF_HEAD_DIM), 0.1),
        "lam_q2": nrm(ks[8], (L, DIFF_HEAD_DIM), 0.1),
        "lam_k2": nrm(ks[9], (L, DIFF_HEAD_DIM), 0.1),
        "subln_g": 1.0 + nrm(ks[10], (L, 2 * DIFF_HEAD_DIM), 0.02),
        "w_out_a": nrm(ks[11], (L, WIDTH_A, D_MODEL), WIDTH_A ** -0.5),
        "w_out_b": nrm(ks[12], (L, WIDTH_B, D_MODEL), WIDTH_B ** -0.5),
        "w_o": nrm(ks[13], (L, D_MODEL, D_MODEL), D_MODEL ** -0.5),
        "norm2_g": 1.0 + nrm(ks[14], (L, D_MODEL), 0.02),
        "w_router": nrm(ks[15], (L, D_MODEL, N_EXPERTS), D_MODEL ** -0.5),
        "b_router": nrm(ks[16], (L, N_EXPERTS), 0.01),
        "w_gate": nrm(ks[17], (L, N_EXPERTS, D_MODEL, D_FF), D_MODEL ** -0.5),
        "b_gate": nrm(ks[18], (L, N_EXPERTS, D_FF), 0.01),
        "w_up": nrm(ks[19], (L, N_EXPERTS, D_MODEL, D_FF), D_MODEL ** -0.5),
        "b_up": nrm(ks[20], (L, N_EXPERTS, D_FF), 0.01),
        "w_down": nrm(ks[21], (L, N_EXPERTS, D_FF, D_MODEL), D_FF ** -0.5),
        "b_down": nrm(ks[22], (L, N_EXPERTS, D_MODEL), 0.01),
        "final_g": 1.0 + nrm(ks[23], (D_MODEL,), 0.02),
    }


def reference(x, c, w_ada, b_ada, norm1_g, w_in, lam_q1, lam_k1, lam_q2, lam_k2, subln_g,
              w_out_a, w_out_b, w_o, norm2_g, w_router, b_router, w_gate, b_gate, w_up, b_up,
              w_down, b_down, final_g):
    B, S, D = x.shape
    slopes = _alibi_slopes(N_HEADS_A + N_HEADS_B)
    slopes_a = jnp.asarray(slopes[:N_HEADS_A])
    slopes_b = jnp.asarray(slopes[N_HEADS_A:])
    split_points = np.cumsum(IN_SPLITS)[:-1].tolist()
    h = x
    for l in range(DEPTH):
        lambda_init = 0.8 - 0.6 * math.exp(-0.3 * l)
        mod = (jax.nn.silu(c) @ w_ada[l] + b_ada[l]).reshape(B, N_MOD, 1, D)
        sh1, sc1, g1, sh2, sc2, g2 = (mod[:, i] for i in range(N_MOD))

        u = rmsnorm(h, norm1_g[l]) * (1.0 + sc1) + sh1
        q_a, k_a, v_a, q_b, k_b, v_b, gate_a, gate_b = jnp.split(u @ w_in[l], split_points, axis=-1)
        y_a = dilated_attention(q_a.reshape(B, S, N_HEADS_A, HEAD_DIM),
                                k_a.reshape(B, S, N_HEADS_A, HEAD_DIM),
                                v_a.reshape(B, S, N_HEADS_A, HEAD_DIM), slopes_a)
        y_b = diff_attention(q_b.reshape(B, S, N_HEADS_B, 2, DIFF_HEAD_DIM),
                             k_b.reshape(B, S, N_HEADS_B, 2, DIFF_HEAD_DIM),
                             v_b.reshape(B, S, N_HEADS_B, 2 * DIFF_HEAD_DIM),
                             lam_q1[l], lam_k1[l], lam_q2[l], lam_k2[l], subln_g[l],
                             slopes_b, lambda_init)
        merged = (jax.nn.sigmoid(gate_a) * (y_a.reshape(B, S, WIDTH_A) @ w_out_a[l])
                  + jax.nn.sigmoid(gate_b) * (y_b.reshape(B, S, WIDTH_B) @ w_out_b[l]))
        h = h + g1 * (merged @ w_o[l])

        u2 = rmsnorm(h, norm2_g[l]) * (1.0 + sc2) + sh2
        h = h + g2 * moe(u2, w_router[l], b_router[l], w_gate[l], b_gate[l], w_up[l], b_up[l],
                         w_down[l], b_down[l])
    return rmsnorm(h, final_g)
```

```python
import functools
import math

import numpy as np
import jax
import jax.numpy as jnp
from jax import lax
from jax.experimental import pallas as pl
from jax.experimental.pallas import tpu as pltpu

F32 = jnp.float32
BF16 = jnp.bfloat16
U32 = jnp.uint32
I32 = jnp.int32

HEAD_DIM = 128
N_HEADS_A = 8
N_HEADS_B = 4
DILATED_PATTERNS = ((128, 1), (512, 4), (2048, 16))
N_EXPERTS = 32
TOP_K = 4
SWIGLU_LIMIT = 7.0
SWIGLU_ALPHA = 1.702
N_MOD = 6
EPS = 1e-5
NEG = -1e30

LANES = 128
V7X_VMEM_LIMIT = 56 * 1024 * 1024

ROW_BLOCK = 128
CHUNK_BLOCKS = 9
FF_TILE = 256


def _alibi_slopes(n):
    return np.array([2.0 ** (-8.0 * (i + 1) / n) for i in range(n)], dtype=np.float32)


def _nt_dot(a, b):
    return lax.dot_general(a, b, (((1,), (1,)), ((), ())), preferred_element_type=F32)


def _split_bf16(x):
    hi = x.astype(BF16)
    lo = (x - hi.astype(F32)).astype(BF16)
    return hi, lo


def _params(sem, vmem=V7X_VMEM_LIMIT):
    return pltpu.CompilerParams(dimension_semantics=sem, vmem_limit_bytes=vmem)


def _ada_kernel(c_ref, w_ref, b_ref, o_ref):
    c = c_ref[...]
    a = c * jax.nn.sigmoid(c)
    a_hi, a_lo = _split_bf16(a)
    w_hi, w_lo = _split_bf16(w_ref[...])
    acc = jnp.dot(a_hi, w_hi, preferred_element_type=F32)
    acc += jnp.dot(a_lo, w_hi, preferred_element_type=F32)
    acc += jnp.dot(a_hi, w_lo, preferred_element_type=F32)
    o_ref[...] = acc + b_ref[...]


def _ada(c8, w, b, tn=1024):
    m, d = c8.shape
    n = w.shape[1]
    return pl.pallas_call(
        _ada_kernel,
        grid=(n // tn,),
        in_specs=[pl.BlockSpec((m, d), lambda j: (0, 0)),
                  pl.BlockSpec((d, tn), lambda j: (0, j)),
                  pl.BlockSpec((1, tn), lambda j: (0, j))],
        out_specs=pl.BlockSpec((m, tn), lambda j: (0, j)),
        out_shape=jax.ShapeDtypeStruct((m, n), F32),
        compiler_params=_params(("arbitrary",)),
        name="ada",
    )(c8, w, b)


def _norm_mod_kernel(x_ref, g_ref, sc_ref, sh_ref, o_ref):
    x = x_ref[...]
    y = x * lax.rsqrt(jnp.mean(x * x, axis=-1, keepdims=True) + EPS) * g_ref[...]
    o_ref[...] = (y * (1.0 + sc_ref[...]) + sh_ref[...]).astype(o_ref.dtype)


def _norm_mod(x2, g, mod3, i_scale, i_shift, seq, tm=512):
    t, d = x2.shape
    per_b = seq // tm
    return pl.pallas_call(
        _norm_mod_kernel,
        grid=(t // tm,),
        in_specs=[pl.BlockSpec((tm, d), lambda i: (i, 0)),
                  pl.BlockSpec((1, d), lambda i: (0, 0)),
                  pl.BlockSpec((None, 1, d), lambda i: ((i // per_b) * N_MOD + i_scale, 0, 0)),
                  pl.BlockSpec((None, 1, d), lambda i: ((i // per_b) * N_MOD + i_shift, 0, 0))],
        out_specs=pl.BlockSpec((tm, d), lambda i: (i, 0)),
        out_shape=jax.ShapeDtypeStruct((t, d), BF16),
        compiler_params=_params(("arbitrary",)),
        name="norm1",
    )(x2, g, mod3, mod3)


def _inproj_kernel(u_ref, w_ref, o_ref, wbf_ref, *, q_tiles, scale):
    n = pl.program_id(0)

    @pl.when(pl.program_id(1) == 0)
    def _():
        wbf_ref[...] = w_ref[...].astype(BF16)

    acc = jnp.dot(u_ref[...], wbf_ref[...], preferred_element_type=F32)
    is_q = functools.reduce(jnp.logical_or, [n == q for q in q_tiles])
    o_ref[...] = (acc * jnp.where(is_q, scale, 1.0)).astype(BF16)


def _inproj(u, w, q_tiles, tm=1024, tn=1024):
    t, d = u.shape
    n = w.shape[1]
    return pl.pallas_call(
        functools.partial(_inproj_kernel, q_tiles=q_tiles, scale=HEAD_DIM ** -0.5),
        grid=(n // tn, t // tm),
        in_specs=[pl.BlockSpec((tm, d), lambda j, i: (i, 0)),
                  pl.BlockSpec((d, tn), lambda j, i: (0, j))],
        out_specs=pl.BlockSpec((tm, tn), lambda j, i: (i, j)),
        out_shape=jax.ShapeDtypeStruct((t, n), BF16),
        scratch_shapes=[pltpu.VMEM((d, tn), BF16)],
        compiler_params=_params(("arbitrary", "arbitrary")),
        name="inproj",
    )(u, w)


def _dil_tables(tq, span):
    nside = span // tq
    o = lax.broadcasted_iota(I32, (2 * nside + 1, tq, tq), 0) - nside
    i = lax.broadcasted_iota(I32, (2 * nside + 1, tq, tq), 1)
    j = lax.broadcasted_iota(I32, (2 * nside + 1, tq, tq), 2)
    ad = jnp.abs(o * tq + j - i)
    mult = jnp.zeros_like(ad)
    for window, dil in DILATED_PATTERNS:
        mult += ((ad % dil == 0) & (ad // dil <= window // (2 * dil))).astype(I32)
    lm = jnp.where(mult > 0, jnp.log(jnp.maximum(mult, 1).astype(F32)), NEG)
    return ad.astype(F32), lm


def _dil_kernel(slopes_ref, q_ref, k_ref, v_ref, ad_ref, lm_ref, o_ref, bias_ref, *, tq, nside, nq):
    h = pl.program_id(1)
    i = pl.program_id(2)

    @pl.when(i == 0)
    def _():
        bias_ref[...] = lm_ref[...] - slopes_ref[h] * ad_ref[...]

    q = q_ref[...]
    lo = jnp.maximum(-nside, -i)
    hi = jnp.minimum(nside, nq - 1 - i)

    def body(o, carry):
        m, l, acc = carry
        start = pl.multiple_of((i + o) * tq, tq)
        kb = k_ref[pl.ds(start, tq), :]
        vb = v_ref[pl.ds(start, tq), :]
        s = _nt_dot(q, kb) + bias_ref[o + nside]
        m_new = jnp.maximum(m, jnp.max(s, axis=-1, keepdims=True))
        alpha = jnp.exp(m - m_new)
        p = jnp.exp(s - m_new)
        l = alpha * l + jnp.sum(p, axis=-1, keepdims=True)
        acc = alpha * acc + jnp.dot(p.astype(BF16), vb, preferred_element_type=F32)
        return m_new, l, acc

    m0 = jnp.full((tq, 1), NEG, F32)
    l0 = jnp.zeros((tq, 1), F32)
    a0 = jnp.zeros((tq, HEAD_DIM), F32)
    _, l, acc = lax.fori_loop(lo, hi + 1, body, (m0, l0, a0))
    o_ref[...] = (acc / l).astype(o_ref.dtype)


def _dil_attention(proj, slopes, batch, seq, q_col, k_col, v_col, tq=256):
    span = max(w // 2 for w, _ in DILATED_PATTERNS)
    nside = span // tq
    nq = seq // tq
    ad, lm = _dil_tables(tq, span)
    tab_spec = pl.BlockSpec((2 * nside + 1, tq, tq), lambda b, h, i, s: (0, 0, 0))
    kern = functools.partial(_dil_kernel, tq=tq, nside=nside, nq=nq)
    return pl.pallas_call(
        kern,
        grid_spec=pltpu.PrefetchScalarGridSpec(
            num_scalar_prefetch=1,
            grid=(batch, N_HEADS_A, nq),
            in_specs=[pl.BlockSpec((tq, HEAD_DIM), lambda b, h, i, s: (b * nq + i, q_col + h)),
                      pl.BlockSpec((seq, HEAD_DIM), lambda b, h, i, s: (b, k_col + h)),
                      pl.BlockSpec((seq, HEAD_DIM), lambda b, h, i, s: (b, v_col + h)),
                      tab_spec, tab_spec],
            out_specs=pl.BlockSpec((tq, HEAD_DIM), lambda b, h, i, s: (b * nq + i, h)),
            scratch_shapes=[pltpu.VMEM((2 * nside + 1, tq, tq), F32)]),
        out_shape=jax.ShapeDtypeStruct((batch * seq, N_HEADS_A * HEAD_DIM), BF16),
        compiler_params=_params(("arbitrary", "arbitrary", "arbitrary")),
        name="dil_attn",
    )(slopes, proj, proj, proj, ad, lm)


def _diff_kernel(slopes_ref, lq1_ref, lk1_ref, lq2_ref, lk2_ref, g_ref, q_ref, k_ref, v_ref, o_ref,
                 *, tq, tk, nk, lambda_init):
    h = pl.program_id(1)
    i = pl.program_id(2)
    neg_slope = -slopes_ref[h]
    q = q_ref[...]
    q1 = q[:, :HEAD_DIM]
    q2 = q[:, HEAD_DIM:]
    d0 = (lax.broadcasted_iota(I32, (tq, tk), 0) - lax.broadcasted_iota(I32, (tq, tk), 1)).astype(F32)

    def online(s, v, m, l, acc):
        m_new = jnp.maximum(m, jnp.max(s, axis=-1, keepdims=True))
        alpha = jnp.exp(m - m_new)
        p = jnp.exp(s - m_new)
        l = alpha * l + jnp.sum(p, axis=-1, keepdims=True)
        acc = alpha * acc + jnp.dot(p.astype(BF16), v, preferred_element_type=F32)
        return m_new, l, acc

    def body(kb, carry):
        m1, l1, a1, m2, l2, a2 = carry
        start = pl.multiple_of(kb * tk, tk)
        k = k_ref[pl.ds(start, tk), :]
        v = v_ref[pl.ds(start, tk), :]
        off = (i * tq - kb * tk).astype(F32)
        bias = jnp.abs(d0 + off) * neg_slope
        m1, l1, a1 = online(_nt_dot(q1, k[:, :HEAD_DIM]) + bias, v, m1, l1, a1)
        m2, l2, a2 = online(_nt_dot(q2, k[:, HEAD_DIM:]) + bias, v, m2, l2, a2)
        return m1, l1, a1, m2, l2, a2

    m0 = jnp.full((tq, 1), NEG, F32)
    l0 = jnp.zeros((tq, 1), F32)
    a0 = jnp.zeros((tq, 2 * HEAD_DIM), F32)
    _, l1, a1, _, l2, a2 = lax.fori_loop(0, nk, body, (m0, l0, a0, m0, l0, a0))

    lam = (jnp.exp(jnp.sum(lq1_ref[...] * lk1_ref[...], axis=-1, keepdims=True))
           - jnp.exp(jnp.sum(lq2_ref[...] * lk2_ref[...], axis=-1, keepdims=True)) + lambda_init)
    o = a1 / l1 - lam * (a2 / l2)
    y = o * lax.rsqrt(jnp.mean(o * o, axis=-1, keepdims=True) + EPS) * g_ref[...]
    o_ref[...] = (y * (1.0 - lambda_init)).astype(o_ref.dtype)


def _diff_attention(proj, slopes, lam_vecs, subln_g, lambda_init, batch, seq, q_col, k_col, v_col,
                    tq=256, tk=512):
    nq = seq // tq
    w = 2 * HEAD_DIM
    vec_spec = pl.BlockSpec((1, HEAD_DIM), lambda b, h, i, s: (0, 0))
    kern = functools.partial(_diff_kernel, tq=tq, tk=tk, nk=seq // tk, lambda_init=lambda_init)
    return pl.pallas_call(
        kern,
        grid_spec=pltpu.PrefetchScalarGridSpec(
            num_scalar_prefetch=1,
            grid=(batch, N_HEADS_B, nq),
            in_specs=[vec_spec, vec_spec, vec_spec, vec_spec,
                      pl.BlockSpec((1, w), lambda b, h, i, s: (0, 0)),
                      pl.BlockSpec((tq, w), lambda b, h, i, s: (b * nq + i, q_col + h)),
                      pl.BlockSpec((seq, w), lambda b, h, i, s: (b, k_col + h)),
                      pl.BlockSpec((seq, w), lambda b, h, i, s: (b, v_col + h))],
            out_specs=pl.BlockSpec((tq, w), lambda b, h, i, s: (b * nq + i, h))),
        out_shape=jax.ShapeDtypeStruct((batch * seq, N_HEADS_B * w), BF16),
        compiler_params=_params(("arbitrary", "arbitrary", "arbitrary")),
        name="diff_attn",
    )(slopes, *lam_vecs, subln_g, proj, proj, proj)


def _merge_kernel(ya_ref, yb_ref, ga_ref, gb_ref, x_ref, g1_ref, sc2_ref, sh2_ref, n2g_ref,
                  woa_ref, wob_ref, wo_ref, wrh_ref, wrl_ref, br_ref,
                  h_ref, u2p_ref, lg_ref):
    a = jnp.dot(ya_ref[...], woa_ref[...], preferred_element_type=F32)
    b = jnp.dot(yb_ref[...], wob_ref[...], preferred_element_type=F32)
    merged = (jax.nn.sigmoid(ga_ref[...].astype(F32)) * a
              + jax.nn.sigmoid(gb_ref[...].astype(F32)) * b)
    h = x_ref[...] + g1_ref[...] * jnp.dot(merged.astype(BF16), wo_ref[...],
                                           preferred_element_type=F32)
    h_ref[...] = h
    y = h * lax.rsqrt(jnp.mean(h * h, axis=-1, keepdims=True) + EPS) * n2g_ref[...]
    u2 = y * (1.0 + sc2_ref[...]) + sh2_ref[...]
    hi, lo = _split_bf16(u2)
    lg = _nt_dot(wrh_ref[...], hi) + _nt_dot(wrl_ref[...], hi) + _nt_dot(wrh_ref[...], lo)
    lg_ref[...] = lg + br_ref[...]
    bits = pltpu.bitcast(hi.astype(F32), U32)
    half = bits.shape[1] // 2
    u2p_ref[...] = (bits[:, :half] >> 16) | (bits[:, half:] & jnp.uint32(0xFFFF0000))


def _merge(ya, yb, proj, x2, mod3, n2g, woa, wob, wo, wr_hi, wr_lo, br, seq, ga_col, gb_col, tm=256):
    t, d = x2.shape
    per_b = seq // tm
    wa = ya.shape[1]
    wb = yb.shape[1]
    ne = wr_hi.shape[0]

    def mod_spec(idx):
        return pl.BlockSpec((None, 1, d), lambda i: ((i // per_b) * N_MOD + idx, 0, 0))

    def const_spec(shape):
        return pl.BlockSpec(shape, lambda i: (0,) * len(shape), pipeline_mode=pl.Buffered(1))

    return pl.pallas_call(
        _merge_kernel,
        grid=(t // tm,),
        in_specs=[pl.BlockSpec((tm, wa), lambda i: (i, 0)),
                  pl.BlockSpec((tm, wb), lambda i: (i, 0)),
                  pl.BlockSpec((tm, d), lambda i: (i, ga_col)),
                  pl.BlockSpec((tm, d), lambda i: (i, gb_col)),
                  pl.BlockSpec((tm, d), lambda i: (i, 0)),
                  mod_spec(2), mod_spec(4), mod_spec(3),
                  const_spec((1, d)),
                  const_spec((wa, d)), const_spec((wb, d)), const_spec((d, d)),
                  const_spec((ne, d)), const_spec((ne, d)), const_spec((ne, 1))],
        out_specs=[pl.BlockSpec((tm, d), lambda i: (i, 0)),
                   pl.BlockSpec((tm, d // 2), lambda i: (i, 0)),
                   pl.BlockSpec((ne, tm), lambda i: (0, i))],
        out_shape=[jax.ShapeDtypeStruct((t, d), F32),
                   jax.ShapeDtypeStruct((t, d // 2), U32),
                   jax.ShapeDtypeStruct((ne, t), F32)],
        compiler_params=_params(("arbitrary",)),
        name="merge",
    )(ya, yb, proj, proj, x2, mod3, mod3, mod3, n2g, woa, wob, wo, wr_hi, wr_lo, br)


def _route_kernel(lg_ref, idx_ref, p_ref, rank_ref, cnt_ref, carry_ref):
    @pl.when(pl.program_id(0) == 0)
    def _():
        carry_ref[...] = jnp.zeros_like(carry_ref)

    lg = lg_ref[...]
    ne, tr = lg.shape
    eio = lax.broadcasted_iota(I32, (ne, tr), 0)
    work = lg
    vals, hots = [], []
    for k in range(TOP_K):
        mx = jnp.max(work, axis=0, keepdims=True)
        am = jnp.min(jnp.where(work == mx, eio, ne), axis=0, keepdims=True)
        hot = eio == am
        idx_ref[k:k + 1, :] = am
        vals.append(mx)
        hots.append(hot)
        work = jnp.where(hot, -jnp.inf, work)
    exps = [jnp.exp(v - vals[0]) for v in vals]
    denom = functools.reduce(jnp.add, exps)
    for k in range(TOP_K):
        p_ref[k:k + 1, :] = exps[k] / denom
    chosen = functools.reduce(jnp.logical_or, hots)
    sel = jnp.where(chosen, 1.0, 0.0)
    tri = (lax.broadcasted_iota(I32, (tr, tr), 0) < lax.broadcasted_iota(I32, (tr, tr), 1))
    before = jnp.dot(sel.astype(BF16), jnp.where(tri, 1.0, 0.0).astype(BF16),
                     preferred_element_type=F32)
    carry = carry_ref[...]
    before = before + carry[:, 0:1]
    for k in range(TOP_K):
        rank_ref[k:k + 1, :] = jnp.sum(jnp.where(hots[k], before, 0.0), axis=0,
                                       keepdims=True).astype(I32)
    carry = carry + jnp.sum(sel, axis=1, keepdims=True)
    carry_ref[...] = carry
    cnt_ref[...] = carry.astype(I32)


def _route(logits_t, tr=512):
    ne, t = logits_t.shape
    slot_spec = pl.BlockSpec((TOP_K, tr), lambda i: (0, i))
    return pl.pallas_call(
        _route_kernel,
        grid=(t // tr,),
        in_specs=[pl.BlockSpec((ne, tr), lambda i: (0, i))],
        out_specs=[slot_spec, slot_spec, slot_spec,
                   pl.BlockSpec((ne, LANES), lambda i: (0, 0))],
        out_shape=[jax.ShapeDtypeStruct((TOP_K, t), I32),
                   jax.ShapeDtypeStruct((TOP_K, t), F32),
                   jax.ShapeDtypeStruct((TOP_K, t), I32),
                   jax.ShapeDtypeStruct((ne, LANES), I32)],
        scratch_shapes=[pltpu.VMEM((ne, LANES), F32)],
        compiler_params=_params(("arbitrary",)),
        name="route",
    )(logits_t)


def _dispatch_kernel(tok_ref, nrows_ref, src_ref, dst_ref, sem, *, batch_rows):
    nrows = nrows_ref[0]
    nbatch = nrows // batch_rows

    def issue(b):
        def one(j, c):
            r = b * batch_rows + j
            pltpu.make_async_copy(src_ref.at[pl.ds(tok_ref[r], 1)], dst_ref.at[pl.ds(r, 1)],
                                  sem.at[b % 2]).start()
            return c
        lax.fori_loop(0, batch_rows, one, 0, unroll=8)

    def drain(b):
        rows = pl.ds(pl.multiple_of(b * batch_rows, batch_rows), batch_rows)
        pltpu.make_async_copy(src_ref.at[pl.ds(0, batch_rows)], dst_ref.at[rows], sem.at[b % 2]).wait()

    @pl.when(nbatch > 0)
    def _():
        issue(0)

    def step(b, c):
        @pl.when(b + 1 < nbatch)
        def _():
            issue(b + 1)
        drain(b)
        return c

    lax.fori_loop(0, nbatch, step, 0)

    def fill(b):
        rows = pl.ds(pl.multiple_of(b * batch_rows, batch_rows), batch_rows)
        return pltpu.make_async_copy(src_ref.at[pl.ds(0, batch_rows)], dst_ref.at[rows], sem.at[0])

    def fill_start(b, c):
        fill(b).start()
        return c

    def fill_wait(b, c):
        fill(b).wait()
        return c

    ntotal = dst_ref.shape[0] // batch_rows
    lax.fori_loop(nbatch, ntotal, fill_start, 0)
    lax.fori_loop(nbatch, ntotal, fill_wait, 0)


def _dispatch(buf_tok, nrows, u2p, total_rows):
    t, w = u2p.shape
    kern = functools.partial(_dispatch_kernel, batch_rows=ROW_BLOCK)
    return pl.pallas_call(
        kern,
        grid_spec=pltpu.PrefetchScalarGridSpec(
            num_scalar_prefetch=2,
            grid=(1,),
            in_specs=[pl.BlockSpec(memory_space=pl.ANY)],
            out_specs=pl.BlockSpec(memory_space=pl.ANY),
            scratch_shapes=[pltpu.SemaphoreType.DMA((2,))]),
        out_shape=jax.ShapeDtypeStruct((total_rows, w), U32),
        compiler_params=_params(("arbitrary",)),
        name="dispatch",
    )(buf_tok, nrows, u2p)


def _moe_kernel(che_ref, chblk_ref, chn_ref, nvalid_ref,
                xs_ref, wg_ref, wu_ref, wd_ref, bg_ref, bu_ref, bd_ref,
                ys_ref,
                xin_ref, xbf_ref, yacc_ref, wgb_ref, wub_ref, wdb_ref, in_sem, out_sem,
                *, nff, nsplit):
    c = pl.program_id(0)
    f = pl.program_id(1)
    nvalid = nvalid_ref[0]
    half = xin_ref.shape[1]

    def in_copy(cc, j):
        row = pl.multiple_of((chblk_ref[cc] + j) * ROW_BLOCK, ROW_BLOCK)
        return pltpu.make_async_copy(xs_ref.at[pl.ds(row, ROW_BLOCK)],
                                     xin_ref.at[pl.ds(j * ROW_BLOCK, ROW_BLOCK)], in_sem)

    def out_copy(cc, j):
        row = pl.multiple_of((chblk_ref[cc] + j) * ROW_BLOCK, ROW_BLOCK)
        return pltpu.make_async_copy(yacc_ref.at[pl.ds(j * ROW_BLOCK, ROW_BLOCK)],
                                     ys_ref.at[pl.ds(row, ROW_BLOCK)], out_sem)

    def for_blocks(cc, fn):
        def one(j, carry):
            fn(cc, j)
            return carry
        lax.fori_loop(0, chn_ref[cc], one, 0)

    @pl.when(c < nvalid)
    def _():
        @pl.when(f == 0)
        def _():
            @pl.when(c == 0)
            def _():
                xbf_ref[...] = jnp.zeros_like(xbf_ref)
                for_blocks(c, lambda cc, j: in_copy(cc, j).start())

            for_blocks(c, lambda cc, j: in_copy(cc, j).wait())

            def unpack(cc, j):
                rows = pl.ds(pl.multiple_of(j * ROW_BLOCK, ROW_BLOCK), ROW_BLOCK)
                w = xin_ref[rows, :]
                xbf_ref[rows, :half] = pltpu.bitcast(w << 16, F32).astype(BF16)
                xbf_ref[rows, half:] = pltpu.bitcast(w & jnp.uint32(0xFFFF0000), F32).astype(BF16)

            for_blocks(c, unpack)

            @pl.when(c + 1 < nvalid)
            def _():
                for_blocks(c + 1, lambda cc, j: in_copy(cc, j).start())

        wgb_ref[...] = wg_ref[...].astype(BF16)
        wub_ref[...] = wu_ref[...].astype(BF16)
        wdb_ref[...] = wd_ref[...].astype(BF16)
        x = xbf_ref[...]
        g = jnp.minimum(jnp.dot(x, wgb_ref[...], preferred_element_type=F32) + bg_ref[...],
                        SWIGLU_LIMIT)
        u = jnp.clip(jnp.dot(x, wub_ref[...], preferred_element_type=F32) + bu_ref[...],
                     -SWIGLU_LIMIT, SWIGLU_LIMIT)
        act = ((u + 1.0) * (g * jax.nn.sigmoid(SWIGLU_ALPHA * g))).astype(BF16)

        @pl.when((f == 0) & (c > 0))
        def _():
            for_blocks(c - 1, lambda cc, j: out_copy(cc, j).wait())

        ncol = yacc_ref.shape[1] // nsplit
        for s in range(nsplit):
            cols = slice(s * ncol, (s + 1) * ncol)
            part = jnp.dot(act, wdb_ref[:, cols], preferred_element_type=F32)

            @pl.when(f == 0)
            def _():
                yacc_ref[:, cols] = part + bd_ref[:, cols]

            @pl.when(f > 0)
            def _():
                yacc_ref[:, cols] += part

        @pl.when(f == nff - 1)
        def _():
            for_blocks(c, lambda cc, j: out_copy(cc, j).start())

            @pl.when(c == nvalid - 1)
            def _():
                for_blocks(c, lambda cc, j: out_copy(cc, j).wait())

                def fill(b):
                    rows = pl.ds(pl.multiple_of(b * ROW_BLOCK, ROW_BLOCK), ROW_BLOCK)
                    return pltpu.make_async_copy(yacc_ref.at[pl.ds(0, ROW_BLOCK)], ys_ref.at[rows],
                                                 out_sem)

                def fill_start(b, carry):
                    fill(b).start()
                    return carry

                def fill_wait(b, carry):
                    fill(b).wait()
                    return carry

                nused = chblk_ref[c] + chn_ref[c]
                ntotal = ys_ref.shape[0] // ROW_BLOCK
                lax.fori_loop(nused, ntotal, fill_start, 0)
                lax.fori_loop(nused, ntotal, fill_wait, 0)


def _moe(che, chblk, chn, nvalid, xs, wg, wu, wd, bg, bu, bd, nchunks):
    ne, d, dff = wg.shape
    rows = CHUNK_BLOCKS * ROW_BLOCK
    nff = dff // FF_TILE

    def ff(c, f, nv):
        return jnp.where(c < nv[0], f, nff - 1)

    kern = functools.partial(_moe_kernel, nff=nff, nsplit=4)
    return pl.pallas_call(
        kern,
        grid_spec=pltpu.PrefetchScalarGridSpec(
            num_scalar_prefetch=4,
            grid=(nchunks, nff),
            in_specs=[pl.BlockSpec(memory_space=pl.ANY),
                      pl.BlockSpec((None, d, FF_TILE), lambda c, f, e, b, n, nv: (e[c], 0, ff(c, f, nv))),
                      pl.BlockSpec((None, d, FF_TILE), lambda c, f, e, b, n, nv: (e[c], 0, ff(c, f, nv))),
                      pl.BlockSpec((None, FF_TILE, d), lambda c, f, e, b, n, nv: (e[c], ff(c, f, nv), 0)),
                      pl.BlockSpec((None, 1, FF_TILE), lambda c, f, e, b, n, nv: (e[c], 0, ff(c, f, nv))),
                      pl.BlockSpec((None, 1, FF_TILE), lambda c, f, e, b, n, nv: (e[c], 0, ff(c, f, nv))),
                      pl.BlockSpec((None, 1, d), lambda c, f, e, b, n, nv: (e[c], 0, 0))],
            out_specs=pl.BlockSpec(memory_space=pl.ANY),
            scratch_shapes=[pltpu.VMEM((rows, d // 2), U32),
                            pltpu.VMEM((rows, d), BF16),
                            pltpu.VMEM((rows, d), F32),
                            pltpu.VMEM((d, FF_TILE), BF16),
                            pltpu.VMEM((d, FF_TILE), BF16),
                            pltpu.VMEM((FF_TILE, d), BF16),
                            pltpu.SemaphoreType.DMA(()),
                            pltpu.SemaphoreType.DMA(())]),
        out_shape=jax.ShapeDtypeStruct((xs.shape[0], d), F32),
        compiler_params=_params(("arbitrary", "arbitrary")),
        name="moe",
    )(che, chblk, chn, nvalid, xs, wg, wu, wd, bg, bu, bd)


def _combine_kernel(dest_ref, ys_ref, h_ref, p_ref, g2_ref, fg_ref, o_ref, buf_ref, sem,
                    *, tc, nsteps, final_norm):
    i = pl.program_id(0)
    t_total = nsteps * tc

    def issue(step, slot):
        for k in range(TOP_K):
            def one(j, c):
                src = dest_ref[k * t_total + step * tc + j]
                pltpu.make_async_copy(ys_ref.at[pl.ds(src, 1)], buf_ref.at[slot, k, pl.ds(j, 1)],
                                      sem.at[slot]).start()
                return c
            lax.fori_loop(0, tc, one, 0, unroll=8)

    def drain(slot):
        for k in range(TOP_K):
            pltpu.make_async_copy(ys_ref.at[pl.ds(0, tc)], buf_ref.at[slot, k], sem.at[slot]).wait()

    slot = i % 2

    @pl.when(i == 0)
    def _():
        issue(0, 0)

    @pl.when(i + 1 < nsteps)
    def _():
        issue(i + 1, 1 - slot)

    drain(slot)
    p = p_ref[...]
    moe = p[:, 0:1] * buf_ref[slot, 0]
    for k in range(1, TOP_K):
        moe += p[:, k:k + 1] * buf_ref[slot, k]
    h = h_ref[...] + g2_ref[...] * moe
    if final_norm:
        h = h * lax.rsqrt(jnp.mean(h * h, axis=-1, keepdims=True) + EPS) * fg_ref[...]
    o_ref[...] = h


def _combine(dest_flat, ys, h, probs_t, mod3, final_g, seq, final_norm, tc=256):
    t, d = h.shape
    per_b = seq // tc
    nsteps = t // tc
    kern = functools.partial(_combine_kernel, tc=tc, nsteps=nsteps, final_norm=final_norm)
    return pl.pallas_call(
        kern,
        grid_spec=pltpu.PrefetchScalarGridSpec(
            num_scalar_prefetch=1,
            grid=(nsteps,),
            in_specs=[pl.BlockSpec(memory_space=pl.ANY),
                      pl.BlockSpec((tc, d), lambda i, s: (i, 0)),
                      pl.BlockSpec((tc, TOP_K), lambda i, s: (i, 0)),
                      pl.BlockSpec((None, 1, d), lambda i, s: ((i // per_b) * N_MOD + 5, 0, 0)),
                      pl.BlockSpec((1, d), lambda i, s: (0, 0))],
            out_specs=pl.BlockSpec((tc, d), lambda i, s: (i, 0)),
            scratch_shapes=[pltpu.VMEM((2, TOP_K, tc, d), F32),
                            pltpu.SemaphoreType.DMA((2,))]),
        out_shape=jax.ShapeDtypeStruct((t, d), F32),
        compiler_params=_params(("arbitrary",)),
        name="combine",
    )(dest_flat, ys, h, probs_t, mod3, final_g)


def _routing_plan(idx, rank, counts, t):
    nblk_total = -(-t * TOP_K // ROW_BLOCK) + N_EXPERTS
    nblk = (counts + ROW_BLOCK - 1) // ROW_BLOCK
    blk_end = jnp.cumsum(nblk)
    blk_start = blk_end - nblk
    dest = blk_start[idx] * ROW_BLOCK + rank
    tok = jnp.broadcast_to(jnp.arange(t, dtype=I32)[None, :], dest.shape)
    buf_tok = jnp.zeros((nblk_total * ROW_BLOCK,), I32).at[dest.reshape(-1)].set(tok.reshape(-1))
    nrows = (blk_end[-1] * ROW_BLOCK).astype(I32).reshape(1)
    nchunks = -(-nblk_total // CHUNK_BLOCKS) + N_EXPERTS
    nch = (nblk + CHUNK_BLOCKS - 1) // CHUNK_BLOCKS
    ch_end = jnp.cumsum(nch)
    ch_start = ch_end - nch
    nvalid = ch_end[-1]
    cid = jnp.minimum(jnp.arange(nchunks, dtype=I32), nvalid - 1)
    che = jnp.clip(jnp.searchsorted(ch_end, cid, side='right'), 0, N_EXPERTS - 1).astype(I32)
    local = cid - ch_start[che]
    chblk = (blk_start[che] + local * CHUNK_BLOCKS).astype(I32)
    chn = jnp.minimum(CHUNK_BLOCKS, nblk[che] - local * CHUNK_BLOCKS).astype(I32)
    return (dest.astype(I32), buf_tok, nrows, che, chblk, chn,
            nvalid.astype(I32).reshape(1), nblk_total * ROW_BLOCK, nchunks)


def kernel(x, c, w_ada, b_ada, norm1_g, w_in, lam_q1, lam_k1, lam_q2, lam_k2, subln_g, w_out_a, w_out_b, w_o, norm2_g, w_router, b_router, w_gate, b_gate, w_up, b_up, w_down, b_down, final_g):
    batch, seq, d = x.shape
    t = batch * seq
    depth = w_ada.shape[0]
    wa = N_HEADS_A * HEAD_DIM
    wb = N_HEADS_B * 2 * HEAD_DIM
    slopes = _alibi_slopes(N_HEADS_A + N_HEADS_B)
    slopes_a = jnp.asarray(slopes[:N_HEADS_A])
    slopes_b = jnp.asarray(slopes[N_HEADS_A:])
    c8 = jnp.pad(c, ((0, 8 - batch), (0, 0)))

    h = x.reshape(t, d)
    for l in range(depth):
        lambda_init = 0.8 - 0.6 * math.exp(-0.3 * l)
        mod = _ada(c8, w_ada[l], b_ada[l].reshape(1, -1))[:batch]
        mod3 = mod.reshape(batch * N_MOD, 1, d)

        u = _norm_mod(h, norm1_g[l].reshape(1, d), mod3, 1, 0, seq)
        proj = _inproj(u, w_in[l], q_tiles=(0, 3 * wa // 1024))
        y_a = _dil_attention(proj, slopes_a, batch, seq, 0, wa // HEAD_DIM, 2 * wa // HEAD_DIM)
        off_b = 3 * wa // (2 * HEAD_DIM)
        nb = wb // (2 * HEAD_DIM)
        y_b = _diff_attention(proj, slopes_b,
                              [v[l].reshape(1, HEAD_DIM) for v in (lam_q1, lam_k1, lam_q2, lam_k2)],
                              subln_g[l].reshape(1, -1), lambda_init, batch, seq,
                              off_b, off_b + nb, off_b + 2 * nb)
        wr_hi, wr_lo = _split_bf16(w_router[l].T)
        gate_col = (3 * wa + 3 * wb) // d
        h, u2p, logits_t = _merge(y_a, y_b, proj, h, mod3, norm2_g[l].reshape(1, d),
                                  w_out_a[l].astype(BF16), w_out_b[l].astype(BF16),
                                  w_o[l].astype(BF16), wr_hi, wr_lo, b_router[l].reshape(-1, 1),
                                  seq, gate_col, gate_col + 1)
        idx, probs, rank, cnt = _route(logits_t)
        (dest, buf_tok, nrows, che, chblk, chn, nvalid, total_rows, nchunks) = _routing_plan(
            idx, rank, cnt[:, 0], t)
        xs = _dispatch(buf_tok, nrows, u2p, total_rows)
        ys = _moe(che, chblk, chn, nvalid, xs, w_gate[l], w_up[l], w_down[l],
                  b_gate[l].reshape(N_EXPERTS, 1, -1), b_up[l].reshape(N_EXPERTS, 1, -1),
                  b_down[l].reshape(N_EXPERTS, 1, -1), nchunks)
        h = _combine(dest.reshape(-1), ys, h, probs.T, mod3, final_g.reshape(1, d), seq,
                     final_norm=(l == depth - 1))
    return h.reshape(batch, seq, d)
```

```python
import functools
import math

import numpy as np
import jax
import jax.numpy as jnp
from jax import lax
from jax.experimental import pallas as pl
from jax.experimental.pallas import tpu as pltpu

F32 = jnp.float32
BF16 = jnp.bfloat16
U32 = jnp.uint32
I32 = jnp.int32

HEAD_DIM = 128
N_HEADS_A = 8
N_HEADS_B = 4
DILATED_PATTERNS = ((128, 1), (512, 4), (2048, 16))
N_EXPERTS = 32
TOP_K = 4
SWIGLU_LIMIT = 7.0
SWIGLU_ALPHA = 1.702
N_MOD = 6
EPS = 1e-5
NEG = -1e30

LANES = 128
V7X_VMEM_LIMIT = 56 * 1024 * 1024

ROW_BLOCK = 128
CHUNK_BLOCKS = 9
MM_ROW_BLOCKS = 3
FF_TILE = 512


def _alibi_slopes(n):
    return np.array([2.0 ** (-8.0 * (i + 1) / n) for i in range(n)], dtype=np.float32)


def _nt_dot(a, b):
    return lax.dot_general(a, b, (((1,), (1,)), ((), ())), preferred_element_type=F32)


def _split_bf16(x):
    hi = x.astype(BF16)
    lo = (x - hi.astype(F32)).astype(BF16)
    return hi, lo


def _params(sem, vmem=V7X_VMEM_LIMIT):
    return pltpu.CompilerParams(dimension_semantics=sem, vmem_limit_bytes=vmem)


def _ada_kernel(c_ref, w_ref, b_ref, o_ref):
    c = c_ref[...]
    a = c * jax.nn.sigmoid(c)
    a_hi, a_lo = _split_bf16(a)
    w_hi, w_lo = _split_bf16(w_ref[...])
    acc = jnp.dot(a_hi, w_hi, preferred_element_type=F32)
    acc += jnp.dot(a_lo, w_hi, preferred_element_type=F32)
    acc += jnp.dot(a_hi, w_lo, preferred_element_type=F32)
    o_ref[...] = acc + b_ref[...]


def _ada(c8, w, b, tn=1024):
    m, d = c8.shape
    n = w.shape[1]
    return pl.pallas_call(
        _ada_kernel,
        grid=(n // tn,),
        in_specs=[pl.BlockSpec((m, d), lambda j: (0, 0)),
                  pl.BlockSpec((d, tn), lambda j: (0, j)),
                  pl.BlockSpec((1, tn), lambda j: (0, j))],
        out_specs=pl.BlockSpec((m, tn), lambda j: (0, j)),
        out_shape=jax.ShapeDtypeStruct((m, n), F32),
        compiler_params=_params(("arbitrary",)),
        name="ada",
    )(c8, w, b)


def _norm_mod_kernel(x_ref, g_ref, sc_ref, sh_ref, o_ref):
    x = x_ref[...]
    y = x * lax.rsqrt(jnp.mean(x * x, axis=-1, keepdims=True) + EPS) * g_ref[...]
    o_ref[...] = (y * (1.0 + sc_ref[...]) + sh_ref[...]).astype(o_ref.dtype)


def _norm_mod(x2, g, mod3, i_scale, i_shift, seq, tm=512):
    t, d = x2.shape
    per_b = seq // tm
    return pl.pallas_call(
        _norm_mod_kernel,
        grid=(t // tm,),
        in_specs=[pl.BlockSpec((tm, d), lambda i: (i, 0)),
                  pl.BlockSpec((1, d), lambda i: (0, 0)),
                  pl.BlockSpec((None, 1, d), lambda i: ((i // per_b) * N_MOD + i_scale, 0, 0)),
                  pl.BlockSpec((None, 1, d), lambda i: ((i // per_b) * N_MOD + i_shift, 0, 0))],
        out_specs=pl.BlockSpec((tm, d), lambda i: (i, 0)),
        out_shape=jax.ShapeDtypeStruct((t, d), BF16),
        compiler_params=_params(("arbitrary",)),
        name="norm1",
    )(x2, g, mod3, mod3)


def _inproj_kernel(u_ref, w_ref, o_ref, wbf_ref, *, q_tiles, scale):
    n = pl.program_id(0)

    @pl.when(pl.program_id(1) == 0)
    def _():
        wbf_ref[...] = w_ref[...].astype(BF16)

    acc = jnp.dot(u_ref[...], wbf_ref[...], preferred_element_type=F32)
    is_q = functools.reduce(jnp.logical_or, [n == q for q in q_tiles])
    o_ref[...] = (acc * jnp.where(is_q, scale, 1.0)).astype(BF16)


def _inproj(u, w, q_tiles, tm=1024, tn=1024):
    t, d = u.shape
    n = w.shape[1]
    return pl.pallas_call(
        functools.partial(_inproj_kernel, q_tiles=q_tiles, scale=HEAD_DIM ** -0.5),
        grid=(n // tn, t // tm),
        in_specs=[pl.BlockSpec((tm, d), lambda j, i: (i, 0)),
                  pl.BlockSpec((d, tn), lambda j, i: (0, j))],
        out_specs=pl.BlockSpec((tm, tn), lambda j, i: (i, j)),
        out_shape=jax.ShapeDtypeStruct((t, n), BF16),
        scratch_shapes=[pltpu.VMEM((d, tn), BF16)],
        compiler_params=_params(("arbitrary", "arbitrary")),
        name="inproj",
    )(u, w)


def _dil_tables(tq, span):
    nside = span // tq
    o = lax.broadcasted_iota(I32, (2 * nside + 1, tq, tq), 0) - nside
    i = lax.broadcasted_iota(I32, (2 * nside + 1, tq, tq), 1)
    j = lax.broadcasted_iota(I32, (2 * nside + 1, tq, tq), 2)
    ad = jnp.abs(o * tq + j - i)
    mult = jnp.zeros_like(ad)
    for window, dil in DILATED_PATTERNS:
        mult += ((ad % dil == 0) & (ad // dil <= window // (2 * dil))).astype(I32)
    lm = jnp.where(mult > 0, jnp.log(jnp.maximum(mult, 1).astype(F32)), NEG)
    return ad.astype(F32), lm


def _dil_kernel(slopes_ref, q_ref, k_ref, v_ref, ad_ref, lm_ref, o_ref, bias_ref, *, tq, nside, nq):
    h = pl.program_id(1)
    i = pl.program_id(2)

    @pl.when(i == 0)
    def _():
        bias_ref[...] = lm_ref[...] - slopes_ref[h] * ad_ref[...]

    q = q_ref[...]
    lo = jnp.maximum(-nside, -i)
    hi = jnp.minimum(nside, nq - 1 - i)

    def body(o, carry):
        m, l, acc = carry
        start = pl.multiple_of((i + o) * tq, tq)
        kb = k_ref[pl.ds(start, tq), :]
        vb = v_ref[pl.ds(start, tq), :]
        s = _nt_dot(q, kb) + bias_ref[o + nside]
        m_new = jnp.maximum(m, jnp.max(s, axis=-1, keepdims=True))
        alpha = jnp.exp(m - m_new)
        p = jnp.exp(s - m_new)
        l = alpha * l + jnp.sum(p, axis=-1, keepdims=True)
        acc = alpha * acc + jnp.dot(p.astype(BF16), vb, preferred_element_type=F32)
        return m_new, l, acc

    m0 = jnp.full((tq, 1), NEG, F32)
    l0 = jnp.zeros((tq, 1), F32)
    a0 = jnp.zeros((tq, HEAD_DIM), F32)
    _, l, acc = lax.fori_loop(lo, hi + 1, body, (m0, l0, a0))
    o_ref[...] = (acc / l).astype(o_ref.dtype)


def _dil_attention(proj, slopes, batch, seq, q_col, k_col, v_col, tq=256):
    span = max(w // 2 for w, _ in DILATED_PATTERNS)
    nside = span // tq
    nq = seq // tq
    ad, lm = _dil_tables(tq, span)
    tab_spec = pl.BlockSpec((2 * nside + 1, tq, tq), lambda b, h, i, s: (0, 0, 0))
    kern = functools.partial(_dil_kernel, tq=tq, nside=nside, nq=nq)
    return pl.pallas_call(
        kern,
        grid_spec=pltpu.PrefetchScalarGridSpec(
            num_scalar_prefetch=1,
            grid=(batch, N_HEADS_A, nq),
            in_specs=[pl.BlockSpec((tq, HEAD_DIM), lambda b, h, i, s: (b * nq + i, q_col + h)),
                      pl.BlockSpec((seq, HEAD_DIM), lambda b, h, i, s: (b, k_col + h)),
                      pl.BlockSpec((seq, HEAD_DIM), lambda b, h, i, s: (b, v_col + h)),
                      tab_spec, tab_spec],
            out_specs=pl.BlockSpec((tq, HEAD_DIM), lambda b, h, i, s: (b * nq + i, h)),
            scratch_shapes=[pltpu.VMEM((2 * nside + 1, tq, tq), F32)]),
        out_shape=jax.ShapeDtypeStruct((batch * seq, N_HEADS_A * HEAD_DIM), BF16),
        compiler_params=_params(("arbitrary", "arbitrary", "arbitrary")),
        name="dil_attn",
    )(slopes, proj, proj, proj, ad, lm)


def _diff_kernel(slopes_ref, lq1_ref, lk1_ref, lq2_ref, lk2_ref, g_ref, q_ref, k_ref, v_ref, o_ref,
                 *, tq, tk, nk, lambda_init):
    h = pl.program_id(1)
    i = pl.program_id(2)
    neg_slope = -slopes_ref[h]
    q = q_ref[...]
    q1 = q[:, :HEAD_DIM]
    q2 = q[:, HEAD_DIM:]
    d0 = (lax.broadcasted_iota(I32, (tq, tk), 0) - lax.broadcasted_iota(I32, (tq, tk), 1)).astype(F32)

    def online(s, v, m, l, acc):
        m_new = jnp.maximum(m, jnp.max(s, axis=-1, keepdims=True))
        alpha = jnp.exp(m - m_new)
        p = jnp.exp(s - m_new)
        l = alpha * l + jnp.sum(p, axis=-1, keepdims=True)
        acc = alpha * acc + jnp.dot(p.astype(BF16), v, preferred_element_type=F32)
        return m_new, l, acc

    def body(kb, carry):
        m1, l1, a1, m2, l2, a2 = carry
        start = pl.multiple_of(kb * tk, tk)
        k = k_ref[pl.ds(start, tk), :]
        v = v_ref[pl.ds(start, tk), :]
        off = (i * tq - kb * tk).astype(F32)
        bias = jnp.abs(d0 + off) * neg_slope
        m1, l1, a1 = online(_nt_dot(q1, k[:, :HEAD_DIM]) + bias, v, m1, l1, a1)
        m2, l2, a2 = online(_nt_dot(q2, k[:, HEAD_DIM:]) + bias, v, m2, l2, a2)
        return m1, l1, a1, m2, l2, a2

    m0 = jnp.full((tq, 1), NEG, F32)
    l0 = jnp.zeros((tq, 1), F32)
    a0 = jnp.zeros((tq, 2 * HEAD_DIM), F32)
    _, l1, a1, _, l2, a2 = lax.fori_loop(0, nk, body, (m0, l0, a0, m0, l0, a0))

    lam = (jnp.exp(jnp.sum(lq1_ref[...] * lk1_ref[...], axis=-1, keepdims=True))
           - jnp.exp(jnp.sum(lq2_ref[...] * lk2_ref[...], axis=-1, keepdims=True)) + lambda_init)
    o = a1 / l1 - lam * (a2 / l2)
    y = o * lax.rsqrt(jnp.mean(o * o, axis=-1, keepdims=True) + EPS) * g_ref[...]
    o_ref[...] = (y * (1.0 - lambda_init)).astype(o_ref.dtype)


def _diff_attention(proj, slopes, lam_vecs, subln_g, lambda_init, batch, seq, q_col, k_col, v_col,
                    tq=256, tk=512):
    nq = seq // tq
    w = 2 * HEAD_DIM
    vec_spec = pl.BlockSpec((1, HEAD_DIM), lambda b, h, i, s: (0, 0))
    kern = functools.partial(_diff_kernel, tq=tq, tk=tk, nk=seq // tk, lambda_init=lambda_init)
    return pl.pallas_call(
        kern,
        grid_spec=pltpu.PrefetchScalarGridSpec(
            num_scalar_prefetch=1,
            grid=(batch, N_HEADS_B, nq),
            in_specs=[vec_spec, vec_spec, vec_spec, vec_spec,
                      pl.BlockSpec((1, w), lambda b, h, i, s: (0, 0)),
                      pl.BlockSpec((tq, w), lambda b, h, i, s: (b * nq + i, q_col + h)),
                      pl.BlockSpec((seq, w), lambda b, h, i, s: (b, k_col + h)),
                      pl.BlockSpec((seq, w), lambda b, h, i, s: (b, v_col + h))],
            out_specs=pl.BlockSpec((tq, w), lambda b, h, i, s: (b * nq + i, h))),
        out_shape=jax.ShapeDtypeStruct((batch * seq, N_HEADS_B * w), BF16),
        compiler_params=_params(("arbitrary", "arbitrary", "arbitrary")),
        name="diff_attn",
    )(slopes, *lam_vecs, subln_g, proj, proj, proj)


def _merge_kernel(ya_ref, yb_ref, ga_ref, gb_ref, x_ref, g1_ref, sc2_ref, sh2_ref, n2g_ref,
                  woa_ref, wob_ref, wo_ref, wrh_ref, wrl_ref, br_ref,
                  h_ref, u2p_ref, lg_ref):
    a = jnp.dot(ya_ref[...], woa_ref[...], preferred_element_type=F32)
    b = jnp.dot(yb_ref[...], wob_ref[...], preferred_element_type=F32)
    merged = (jax.nn.sigmoid(ga_ref[...].astype(F32)) * a
              + jax.nn.sigmoid(gb_ref[...].astype(F32)) * b)
    h = x_ref[...] + g1_ref[...] * jnp.dot(merged.astype(BF16), wo_ref[...],
                                           preferred_element_type=F32)
    h_ref[...] = h
    y = h * lax.rsqrt(jnp.mean(h * h, axis=-1, keepdims=True) + EPS) * n2g_ref[...]
    u2 = y * (1.0 + sc2_ref[...]) + sh2_ref[...]
    hi, lo = _split_bf16(u2)
    lg = _nt_dot(wrh_ref[...], hi) + _nt_dot(wrl_ref[...], hi) + _nt_dot(wrh_ref[...], lo)
    lg_ref[...] = lg + br_ref[...]
    bits = pltpu.bitcast(hi.astype(F32), U32)
    half = bits.shape[1] // 2
    u2p_ref[...] = (bits[:, :half] >> 16) | (bits[:, half:] & jnp.uint32(0xFFFF0000))


def _merge(ya, yb, proj, x2, mod3, n2g, woa, wob, wo, wr_hi, wr_lo, br, seq, ga_col, gb_col, tm=256):
    t, d = x2.shape
    per_b = seq // tm
    wa = ya.shape[1]
    wb = yb.shape[1]
    ne = wr_hi.shape[0]

    def mod_spec(idx):
        return pl.BlockSpec((None, 1, d), lambda i: ((i // per_b) * N_MOD + idx, 0, 0))

    def const_spec(shape):
        return pl.BlockSpec(shape, lambda i: (0,) * len(shape), pipeline_mode=pl.Buffered(1))

    return pl.pallas_call(
        _merge_kernel,
        grid=(t // tm,),
        in_specs=[pl.BlockSpec((tm, wa), lambda i: (i, 0)),
                  pl.BlockSpec((tm, wb), lambda i: (i, 0)),
                  pl.BlockSpec((tm, d), lambda i: (i, ga_col)),
                  pl.BlockSpec((tm, d), lambda i: (i, gb_col)),
                  pl.BlockSpec((tm, d), lambda i: (i, 0)),
                  mod_spec(2), mod_spec(4), mod_spec(3),
                  const_spec((1, d)),
                  const_spec((wa, d)), const_spec((wb, d)), const_spec((d, d)),
                  const_spec((ne, d)), const_spec((ne, d)), const_spec((ne, 1))],
        out_specs=[pl.BlockSpec((tm, d), lambda i: (i, 0)),
                   pl.BlockSpec((tm, d // 2), lambda i: (i, 0)),
                   pl.BlockSpec((ne, tm), lambda i: (0, i))],
        out_shape=[jax.ShapeDtypeStruct((t, d), F32),
                   jax.ShapeDtypeStruct((t, d // 2), U32),
                   jax.ShapeDtypeStruct((ne, t), F32)],
        compiler_params=_params(("arbitrary",)),
        name="merge",
    )(ya, yb, proj, proj, x2, mod3, mod3, mod3, n2g, woa, wob, wo, wr_hi, wr_lo, br)


def _route_kernel(lg_ref, p_ref, dest_ref, cnt_ref, carry_ref, total_ref):
    sweep = pl.program_id(0)

    @pl.when(pl.program_id(1) == 0)
    def _():
        @pl.when(sweep == 0)
        def _():
            total_ref[...] = jnp.zeros_like(total_ref)

        @pl.when(sweep == 1)
        def _():
            total_ref[...] = carry_ref[...]

        carry_ref[...] = jnp.zeros_like(carry_ref)

    lg = lg_ref[...]
    ne, tr = lg.shape
    eio = lax.broadcasted_iota(I32, (ne, tr), 0)
    work = lg
    vals, hots = [], []
    for k in range(TOP_K):
        mx = jnp.max(work, axis=0, keepdims=True)
        am = jnp.min(jnp.where(work == mx, eio, ne), axis=0, keepdims=True)
        hot = eio == am
        vals.append(mx)
        hots.append(hot)
        work = jnp.where(hot, -jnp.inf, work)
    exps = [jnp.exp(v - vals[0]) for v in vals]
    denom = functools.reduce(jnp.add, exps)
    for k in range(TOP_K):
        p_ref[k:k + 1, :] = exps[k] / denom
    chosen = functools.reduce(jnp.logical_or, hots)
    sel = jnp.where(chosen, 1.0, 0.0)
    tri = (lax.broadcasted_iota(I32, (tr, tr), 0) < lax.broadcasted_iota(I32, (tr, tr), 1))
    before = jnp.dot(sel.astype(BF16), jnp.where(tri, 1.0, 0.0).astype(BF16),
                     preferred_element_type=F32)
    carry = carry_ref[...]
    nblk = jnp.floor((total_ref[...][:, 0:1] + (ROW_BLOCK - 1)) * (1.0 / ROW_BLOCK))
    e_row = lax.broadcasted_iota(I32, (ne, ne), 0)
    e_col = lax.broadcasted_iota(I32, (ne, ne), 1)
    nblk_lanes = jnp.sum(jnp.where(e_row == e_col, nblk, 0.0), axis=0, keepdims=True)
    first_row = jnp.sum(jnp.where(e_col < e_row, nblk_lanes, 0.0), axis=1,
                        keepdims=True) * ROW_BLOCK
    place = before + (carry[:, 0:1] + first_row)
    for k in range(TOP_K):
        dest_ref[k:k + 1, :] = jnp.sum(jnp.where(hots[k], place, 0.0), axis=0,
                                       keepdims=True).astype(I32)
    carry = carry + jnp.sum(sel, axis=1, keepdims=True)
    carry_ref[...] = carry
    cnt_ref[...] = carry.astype(I32)


def _route(logits_t, tr=512):
    ne, t = logits_t.shape
    slot_spec = pl.BlockSpec((None, TOP_K, tr), lambda s, i: (s, 0, i))
    probs, dest, cnt = pl.pallas_call(
        _route_kernel,
        grid=(2, t // tr),
        in_specs=[pl.BlockSpec((ne, tr), lambda s, i: (0, i))],
        out_specs=[slot_spec, slot_spec,
                   pl.BlockSpec((None, ne, LANES), lambda s, i: (s, 0, 0))],
        out_shape=[jax.ShapeDtypeStruct((2, TOP_K, t), F32),
                   jax.ShapeDtypeStruct((2, TOP_K, t), I32),
                   jax.ShapeDtypeStruct((2, ne, LANES), I32)],
        scratch_shapes=[pltpu.VMEM((ne, LANES), F32), pltpu.VMEM((ne, LANES), F32)],
        compiler_params=_params(("arbitrary", "arbitrary")),
        name="route",
    )(logits_t)
    return probs[1], dest[1], cnt[1]


def _dispatch_kernel(dest_ref, fill_ref, nfill_ref, src_ref, dst_ref, sem, fill_sem, *, tm, t_total):
    i = pl.program_id(0)

    @pl.when(i == 0)
    def _():
        def fill(b):
            rows = pl.ds(pl.multiple_of(fill_ref[b] * ROW_BLOCK, ROW_BLOCK), ROW_BLOCK)
            return pltpu.make_async_copy(src_ref.at[pl.ds(0, ROW_BLOCK)], dst_ref.at[rows], fill_sem)

        def fill_start(b, c):
            fill(b).start()
            return c

        def fill_wait(b, c):
            fill(b).wait()
            return c

        lax.fori_loop(0, nfill_ref[0], fill_start, 0)
        lax.fori_loop(0, nfill_ref[0], fill_wait, 0)

    for k in range(TOP_K):
        def one(j, c):
            row = dest_ref[k * t_total + i * tm + j]
            pltpu.make_async_copy(src_ref.at[pl.ds(j, 1)], dst_ref.at[pl.ds(row, 1)], sem).start()
            return c
        lax.fori_loop(0, tm, one, 0, unroll=8)
    for k in range(TOP_K):
        pltpu.make_async_copy(src_ref, dst_ref.at[pl.ds(0, tm)], sem).wait()


def _dispatch(dest_flat, fill_blocks, nfill, u2p, total_rows, tm=512):
    t, w = u2p.shape
    kern = functools.partial(_dispatch_kernel, tm=tm, t_total=t)
    return pl.pallas_call(
        kern,
        grid_spec=pltpu.PrefetchScalarGridSpec(
            num_scalar_prefetch=3,
            grid=(t // tm,),
            in_specs=[pl.BlockSpec((tm, w), lambda i, d, fb, nf: (i, 0))],
            out_specs=pl.BlockSpec(memory_space=pl.ANY),
            scratch_shapes=[pltpu.SemaphoreType.DMA(()), pltpu.SemaphoreType.DMA(())]),
        out_shape=jax.ShapeDtypeStruct((total_rows, w), U32),
        compiler_params=_params(("arbitrary",)),
        name="dispatch",
    )(dest_flat, fill_blocks, nfill, u2p)


def _moe_kernel(che_ref, chblk_ref, chn_ref, nvalid_ref,
                xs_ref, wg_ref, wu_ref, wd_ref, bg_ref, bu_ref, bd_ref,
                ys_ref,
                xin_ref, xbf_ref, yacc_ref, act_ref, wgb_ref, wub_ref, wdb_ref, in_sem, out_sem,
                *, nff, nsplit, mm_rows):
    c = pl.program_id(0)
    f = pl.program_id(1)
    nvalid = nvalid_ref[0]
    half = xin_ref.shape[1]

    def in_copy(cc, j):
        row = pl.multiple_of((chblk_ref[cc] + j) * ROW_BLOCK, ROW_BLOCK)
        return pltpu.make_async_copy(xs_ref.at[pl.ds(row, ROW_BLOCK)],
                                     xin_ref.at[pl.ds(j * ROW_BLOCK, ROW_BLOCK)], in_sem)

    def out_copy(cc, j):
        row = pl.multiple_of((chblk_ref[cc] + j) * ROW_BLOCK, ROW_BLOCK)
        return pltpu.make_async_copy(yacc_ref.at[pl.ds(j * ROW_BLOCK, ROW_BLOCK)],
                                     ys_ref.at[pl.ds(row, ROW_BLOCK)], out_sem)

    def for_blocks(cc, fn):
        def one(j, carry):
            fn(cc, j)
            return carry
        lax.fori_loop(0, chn_ref[cc], one, 0)

    @pl.when(c < nvalid)
    def _():
        @pl.when(f == 0)
        def _():
            @pl.when(c == 0)
            def _():
                xbf_ref[...] = jnp.zeros_like(xbf_ref)
                for_blocks(c, lambda cc, j: in_copy(cc, j).start())

            for_blocks(c, lambda cc, j: in_copy(cc, j).wait())

            def unpack(cc, j):
                rows = pl.ds(pl.multiple_of(j * ROW_BLOCK, ROW_BLOCK), ROW_BLOCK)
                w = xin_ref[rows, :]
                xbf_ref[rows, :half] = pltpu.bitcast(w << 16, F32).astype(BF16)
                xbf_ref[rows, half:] = pltpu.bitcast(w & jnp.uint32(0xFFFF0000), F32).astype(BF16)

            for_blocks(c, unpack)

            @pl.when(c + 1 < nvalid)
            def _():
                for_blocks(c + 1, lambda cc, j: in_copy(cc, j).start())

        wgb_ref[...] = wg_ref[...].astype(BF16)
        wub_ref[...] = wu_ref[...].astype(BF16)
        nrows = xbf_ref.shape[0]
        for r0 in range(0, nrows, mm_rows):
            rows = slice(r0, r0 + mm_rows)
            x = xbf_ref[rows, :]
            g = jnp.minimum(jnp.dot(x, wgb_ref[...], preferred_element_type=F32) + bg_ref[...],
                            SWIGLU_LIMIT)
            u = jnp.clip(jnp.dot(x, wub_ref[...], preferred_element_type=F32) + bu_ref[...],
                         -SWIGLU_LIMIT, SWIGLU_LIMIT)
            act_ref[rows, :] = ((u + 1.0) * (g * jax.nn.sigmoid(SWIGLU_ALPHA * g))).astype(BF16)
        wdb_ref[...] = wd_ref[...].astype(BF16)

        @pl.when((f == 0) & (c > 0))
        def _():
            for_blocks(c - 1, lambda cc, j: out_copy(cc, j).wait())

        first = f == 0
        ncol = yacc_ref.shape[1] // nsplit
        for r0 in range(0, nrows, mm_rows):
            rows = slice(r0, r0 + mm_rows)
            a = act_ref[rows, :]
            for s in range(nsplit):
                cols = slice(s * ncol, (s + 1) * ncol)
                part = jnp.dot(a, wdb_ref[:, cols], preferred_element_type=F32)
                base = jnp.where(first, jnp.broadcast_to(bd_ref[:, cols], part.shape),
                                 yacc_ref[rows, cols])
                yacc_ref[rows, cols] = base + part

        @pl.when(f == nff - 1)
        def _():
            for_blocks(c, lambda cc, j: out_copy(cc, j).start())

            @pl.when(c == nvalid - 1)
            def _():
                for_blocks(c, lambda cc, j: out_copy(cc, j).wait())

                def fill(b):
                    rows = pl.ds(pl.multiple_of(b * ROW_BLOCK, ROW_BLOCK), ROW_BLOCK)
                    return pltpu.make_async_copy(yacc_ref.at[pl.ds(0, ROW_BLOCK)], ys_ref.at[rows],
                                                 out_sem)

                def fill_start(b, carry):
                    fill(b).start()
                    return carry

                def fill_wait(b, carry):
                    fill(b).wait()
                    return carry

                nused = chblk_ref[c] + chn_ref[c]
                ntotal = ys_ref.shape[0] // ROW_BLOCK
                lax.fori_loop(nused, ntotal, fill_start, 0)
                lax.fori_loop(nused, ntotal, fill_wait, 0)


def _moe(che, chblk, chn, nvalid, xs, wg, wu, wd, bg, bu, bd, nchunks):
    ne, d, dff = wg.shape
    rows = CHUNK_BLOCKS * ROW_BLOCK
    nff = dff // FF_TILE

    def ff(c, f, nv):
        return jnp.where(c < nv[0], f, nff - 1)

    kern = functools.partial(_moe_kernel, nff=nff, nsplit=4, mm_rows=MM_ROW_BLOCKS * ROW_BLOCK)
    return pl.pallas_call(
        kern,
        grid_spec=pltpu.PrefetchScalarGridSpec(
            num_scalar_prefetch=4,
            grid=(nchunks, nff),
            in_specs=[pl.BlockSpec(memory_space=pl.ANY),
                      pl.BlockSpec((None, d, FF_TILE), lambda c, f, e, b, n, nv: (e[c], 0, ff(c, f, nv))),
                      pl.BlockSpec((None, d, FF_TILE), lambda c, f, e, b, n, nv: (e[c], 0, ff(c, f, nv))),
                      pl.BlockSpec((None, FF_TILE, d), lambda c, f, e, b, n, nv: (e[c], ff(c, f, nv), 0)),
                      pl.BlockSpec((None, 1, FF_TILE), lambda c, f, e, b, n, nv: (e[c], 0, ff(c, f, nv))),
                      pl.BlockSpec((None, 1, FF_TILE), lambda c, f, e, b, n, nv: (e[c], 0, ff(c, f, nv))),
                      pl.BlockSpec((None, 1, d), lambda c, f, e, b, n, nv: (e[c], 0, 0))],
            out_specs=pl.BlockSpec(memory_space=pl.ANY),
            scratch_shapes=[pltpu.VMEM((rows, d // 2), U32),
                            pltpu.VMEM((rows, d), BF16),
                            pltpu.VMEM((rows, d), F32),
                            pltpu.VMEM((rows, FF_TILE), BF16),
                            pltpu.VMEM((d, FF_TILE), BF16),
                            pltpu.VMEM((d, FF_TILE), BF16),
                            pltpu.VMEM((FF_TILE, d), BF16),
                            pltpu.SemaphoreType.DMA(()),
                            pltpu.SemaphoreType.DMA(())]),
        out_shape=jax.ShapeDtypeStruct((xs.shape[0], d), F32),
        compiler_params=_params(("arbitrary", "arbitrary")),
        name="moe",
    )(che, chblk, chn, nvalid, xs, wg, wu, wd, bg, bu, bd)


def _combine_kernel(dest_ref, ys_ref, h_ref, p_ref, g2_ref, fg_ref, o_ref, buf_ref, sem,
                    *, tc, nsteps, final_norm):
    i = pl.program_id(0)
    t_total = nsteps * tc

    def issue(step, slot):
        for k in range(TOP_K):
            def one(j, c):
                src = dest_ref[k * t_total + step * tc + j]
                pltpu.make_async_copy(ys_ref.at[pl.ds(src, 1)], buf_ref.at[slot, k, pl.ds(j, 1)],
                                      sem.at[slot]).start()
                return c
            lax.fori_loop(0, tc, one, 0, unroll=8)

    def drain(slot):
        for k in range(TOP_K):
            pltpu.make_async_copy(ys_ref.at[pl.ds(0, tc)], buf_ref.at[slot, k], sem.at[slot]).wait()

    slot = i % 2

    @pl.when(i == 0)
    def _():
        issue(0, 0)

    @pl.when(i + 1 < nsteps)
    def _():
        issue(i + 1, 1 - slot)

    drain(slot)
    p = p_ref[...]
    moe = p[:, 0:1] * buf_ref[slot, 0]
    for k in range(1, TOP_K):
        moe += p[:, k:k + 1] * buf_ref[slot, k]
    h = h_ref[...] + g2_ref[...] * moe
    if final_norm:
        h = h * lax.rsqrt(jnp.mean(h * h, axis=-1, keepdims=True) + EPS) * fg_ref[...]
    o_ref[...] = h


def _combine(dest_flat, ys, h, probs_t, mod3, final_g, seq, final_norm, tc=256):
    t, d = h.shape
    per_b = seq // tc
    nsteps = t // tc
    kern = functools.partial(_combine_kernel, tc=tc, nsteps=nsteps, final_norm=final_norm)
    return pl.pallas_call(
        kern,
        grid_spec=pltpu.PrefetchScalarGridSpec(
            num_scalar_prefetch=1,
            grid=(nsteps,),
            in_specs=[pl.BlockSpec(memory_space=pl.ANY),
                      pl.BlockSpec((tc, d), lambda i, s: (i, 0)),
                      pl.BlockSpec((tc, TOP_K), lambda i, s: (i, 0)),
                      pl.BlockSpec((None, 1, d), lambda i, s: ((i // per_b) * N_MOD + 5, 0, 0)),
                      pl.BlockSpec((1, d), lambda i, s: (0, 0))],
            out_specs=pl.BlockSpec((tc, d), lambda i, s: (i, 0)),
            scratch_shapes=[pltpu.VMEM((2, TOP_K, tc, d), F32),
                            pltpu.SemaphoreType.DMA((2,))]),
        out_shape=jax.ShapeDtypeStruct((t, d), F32),
        compiler_params=_params(("arbitrary",)),
        name="combine",
    )(dest_flat, ys, h, probs_t, mod3, final_g)


def _routing_plan(counts, t):
    nblk_total = -(-t * TOP_K // ROW_BLOCK) + N_EXPERTS
    nblk = (counts + ROW_BLOCK - 1) // ROW_BLOCK
    blk_end = jnp.cumsum(nblk)
    blk_start = blk_end - nblk
    ntail_max = nblk_total - t * TOP_K // ROW_BLOCK
    fill_blocks = jnp.concatenate([jnp.maximum(blk_end - 1, 0),
                                   jnp.minimum(blk_end[-1] + jnp.arange(ntail_max), nblk_total - 1)])
    nfill = N_EXPERTS + nblk_total - blk_end[-1]
    nchunks = -(-nblk_total // CHUNK_BLOCKS) + N_EXPERTS
    nch = (nblk + CHUNK_BLOCKS - 1) // CHUNK_BLOCKS
    ch_end = jnp.cumsum(nch)
    ch_start = ch_end - nch
    nvalid = ch_end[-1]
    cid = jnp.minimum(jnp.arange(nchunks, dtype=I32), nvalid - 1)
    che = jnp.clip(jnp.searchsorted(ch_end, cid, side='right'), 0, N_EXPERTS - 1).astype(I32)
    local = cid - ch_start[che]
    chblk = (blk_start[che] + local * CHUNK_BLOCKS).astype(I32)
    chn = jnp.minimum(CHUNK_BLOCKS, nblk[che] - local * CHUNK_BLOCKS).astype(I32)
    return (fill_blocks.astype(I32), nfill.astype(I32).reshape(1), che, chblk, chn,
            nvalid.astype(I32).reshape(1), nblk_total * ROW_BLOCK, nchunks)


def kernel(x, c, w_ada, b_ada, norm1_g, w_in, lam_q1, lam_k1, lam_q2, lam_k2, subln_g, w_out_a, w_out_b, w_o, norm2_g, w_router, b_router, w_gate, b_gate, w_up, b_up, w_down, b_down, final_g):
    batch, seq, d = x.shape
    t = batch * seq
    depth = w_ada.shape[0]
    wa = N_HEADS_A * HEAD_DIM
    wb = N_HEADS_B * 2 * HEAD_DIM
    slopes = _alibi_slopes(N_HEADS_A + N_HEADS_B)
    slopes_a = jnp.asarray(slopes[:N_HEADS_A])
    slopes_b = jnp.asarray(slopes[N_HEADS_A:])
    c8 = jnp.pad(c, ((0, 8 - batch), (0, 0)))

    h = x.reshape(t, d)
    for l in range(depth):
        lambda_init = 0.8 - 0.6 * math.exp(-0.3 * l)
        mod = _ada(c8, w_ada[l], b_ada[l].reshape(1, -1))[:batch]
        mod3 = mod.reshape(batch * N_MOD, 1, d)

        u = _norm_mod(h, norm1_g[l].reshape(1, d), mod3, 1, 0, seq)
        proj = _inproj(u, w_in[l], q_tiles=(0, 3 * wa // 1024))
        y_a = _dil_attention(proj, slopes_a, batch, seq, 0, wa // HEAD_DIM, 2 * wa // HEAD_DIM)
        off_b = 3 * wa // (2 * HEAD_DIM)
        nb = wb // (2 * HEAD_DIM)
        y_b = _diff_attention(proj, slopes_b,
                              [v[l].reshape(1, HEAD_DIM) for v in (lam_q1, lam_k1, lam_q2, lam_k2)],
                              subln_g[l].reshape(1, -1), lambda_init, batch, seq,
                              off_b, off_b + nb, off_b + 2 * nb)
        wr_hi, wr_lo = _split_bf16(w_router[l].T)
        gate_col = (3 * wa + 3 * wb) // d
        h, u2p, logits_t = _merge(y_a, y_b, proj, h, mod3, norm2_g[l].reshape(1, d),
                                  w_out_a[l].astype(BF16), w_out_b[l].astype(BF16),
                                  w_o[l].astype(BF16), wr_hi, wr_lo, b_router[l].reshape(-1, 1),
                                  seq, gate_col, gate_col + 1)
        probs, dest, cnt = _route(logits_t)
        (fill_blocks, nfill, che, chblk, chn, nvalid, total_rows, nchunks) = _routing_plan(
            cnt[:, 0], t)
        xs = _dispatch(dest.reshape(-1), fill_blocks, nfill, u2p, total_rows)
        ys = _moe(che, chblk, chn, nvalid, xs, w_gate[l], w_up[l], w_down[l],
                  b_gate[l].reshape(N_EXPERTS, 1, -1), b_up[l].reshape(N_EXPERTS, 1, -1),
                  b_down[l].reshape(N_EXPERTS, 1, -1), nchunks)
        h = _combine(dest.reshape(-1), ys, h, probs.T, mod3, final_g.reshape(1, d), seq,
                     final_norm=(l == depth - 1))
    return h.reshape(batch, seq, d)
```

```python
import functools
import math

import numpy as np
import jax
import jax.numpy as jnp
from jax import lax
from jax.experimental import pallas as pl
from jax.experimental.pallas import tpu as pltpu

F32 = jnp.float32
BF16 = jnp.bfloat16
U32 = jnp.uint32
I32 = jnp.int32

HEAD_DIM = 128
N_HEADS_A = 8
N_HEADS_B = 4
DILATED_PATTERNS = ((128, 1), (512, 4), (2048, 16))
N_EXPERTS = 32
TOP_K = 4
SWIGLU_LIMIT = 7.0
SWIGLU_ALPHA = 1.702
N_MOD = 6
EPS = 1e-5
NEG = -1e30
LOG2E = math.log2(math.e)

LANES = 128
V7X_VMEM_LIMIT = 56 * 1024 * 1024

ROW_BLOCK = 128
CHUNK_BLOCKS = 9
MM_ROW_BLOCKS = 3
FF_TILE = 512


def _alibi_slopes(n):
    return np.array([2.0 ** (-8.0 * (i + 1) / n) for i in range(n)], dtype=np.float32)


def _nt_dot(a, b):
    return lax.dot_general(a, b, (((1,), (1,)), ((), ())), preferred_element_type=F32)


def _split_bf16(x):
    hi = x.astype(BF16)
    lo = (x - hi.astype(F32)).astype(BF16)
    return hi, lo


def _params(sem, vmem=V7X_VMEM_LIMIT):
    return pltpu.CompilerParams(dimension_semantics=sem, vmem_limit_bytes=vmem)


def _ada_kernel(c_ref, w_ref, b_ref, o_ref):
    c = c_ref[...]
    a = c * jax.nn.sigmoid(c)
    a_hi, a_lo = _split_bf16(a)
    w_hi, w_lo = _split_bf16(w_ref[...])
    acc = jnp.dot(a_hi, w_hi, preferred_element_type=F32)
    acc += jnp.dot(a_lo, w_hi, preferred_element_type=F32)
    acc += jnp.dot(a_hi, w_lo, preferred_element_type=F32)
    o_ref[...] = acc + b_ref[...]


def _ada(c8, w, b, tn=1024):
    m, d = c8.shape
    n = w.shape[1]
    return pl.pallas_call(
        _ada_kernel,
        grid=(n // tn,),
        in_specs=[pl.BlockSpec((m, d), lambda j: (0, 0)),
                  pl.BlockSpec((d, tn), lambda j: (0, j)),
                  pl.BlockSpec((1, tn), lambda j: (0, j))],
        out_specs=pl.BlockSpec((m, tn), lambda j: (0, j)),
        out_shape=jax.ShapeDtypeStruct((m, n), F32),
        compiler_params=_params(("arbitrary",)),
        name="ada",
    )(c8, w, b)


def _norm_mod_kernel(x_ref, g_ref, sc_ref, sh_ref, o_ref):
    x = x_ref[...]
    y = x * lax.rsqrt(jnp.mean(x * x, axis=-1, keepdims=True) + EPS) * g_ref[...]
    o_ref[...] = (y * (1.0 + sc_ref[...]) + sh_ref[...]).astype(o_ref.dtype)


def _norm_mod(x2, g, mod3, i_scale, i_shift, seq, tm=512):
    t, d = x2.shape
    per_b = seq // tm
    return pl.pallas_call(
        _norm_mod_kernel,
        grid=(t // tm,),
        in_specs=[pl.BlockSpec((tm, d), lambda i: (i, 0)),
                  pl.BlockSpec((1, d), lambda i: (0, 0)),
                  pl.BlockSpec((None, 1, d), lambda i: ((i // per_b) * N_MOD + i_scale, 0, 0)),
                  pl.BlockSpec((None, 1, d), lambda i: ((i // per_b) * N_MOD + i_shift, 0, 0))],
        out_specs=pl.BlockSpec((tm, d), lambda i: (i, 0)),
        out_shape=jax.ShapeDtypeStruct((t, d), BF16),
        compiler_params=_params(("arbitrary",)),
        name="norm1",
    )(x2, g, mod3, mod3)


def _inproj_kernel(u_ref, w_ref, o_ref, wbf_ref, *, q_tiles, scale):
    n = pl.program_id(0)

    @pl.when(pl.program_id(1) == 0)
    def _():
        wbf_ref[...] = w_ref[...].astype(BF16)

    acc = jnp.dot(u_ref[...], wbf_ref[...], preferred_element_type=F32)
    is_q = functools.reduce(jnp.logical_or, [n == q for q in q_tiles])
    o_ref[...] = (acc * jnp.where(is_q, scale, 1.0)).astype(BF16)


def _inproj(u, w, q_tiles, tm=1024, tn=1024):
    t, d = u.shape
    n = w.shape[1]
    return pl.pallas_call(
        functools.partial(_inproj_kernel, q_tiles=q_tiles, scale=HEAD_DIM ** -0.5 * LOG2E),
        grid=(n // tn, t // tm),
        in_specs=[pl.BlockSpec((tm, d), lambda j, i: (i, 0)),
                  pl.BlockSpec((d, tn), lambda j, i: (0, j))],
        out_specs=pl.BlockSpec((tm, tn), lambda j, i: (i, j)),
        out_shape=jax.ShapeDtypeStruct((t, n), BF16),
        scratch_shapes=[pltpu.VMEM((d, tn), BF16)],
        compiler_params=_params(("arbitrary", "arbitrary")),
        name="inproj",
    )(u, w)


def _dil_tables(tq, span):
    nside = span // tq
    o = lax.broadcasted_iota(I32, (2 * nside + 1, tq, tq), 0) - nside
    i = lax.broadcasted_iota(I32, (2 * nside + 1, tq, tq), 1)
    j = lax.broadcasted_iota(I32, (2 * nside + 1, tq, tq), 2)
    ad = jnp.abs(o * tq + j - i)
    mult = jnp.zeros_like(ad)
    for window, dil in DILATED_PATTERNS:
        mult += ((ad % dil == 0) & (ad // dil <= window // (2 * dil))).astype(I32)
    lm = jnp.where(mult > 0, jnp.log2(jnp.maximum(mult, 1).astype(F32)), NEG)
    return ad.astype(F32) * LOG2E, lm


def _dil_kernel(slopes_ref, q_ref, k_ref, v_ref, ad_ref, lm_ref, o_ref, bias_ref, vt_ref, st_ref,
                *, tq, nside, nq):
    h = pl.program_id(1)
    i = pl.program_id(2)

    @pl.when(i == 0)
    def _():
        bias_ref[:2 * nside + 1] = lm_ref[...] - slopes_ref[h] * ad_ref[...]
        bias_ref[2 * nside + 1] = jnp.full((tq, tq), NEG, F32)
        for kb in range(nq):
            vt_ref[kb] = v_ref[kb * tq:(kb + 1) * tq, :].astype(F32).T.astype(BF16)

    nwin = 2 * nside + 1
    b0 = jnp.clip(i - nside, 0, nq - nwin)
    q = q_ref[...]
    m = None
    for j in range(nwin):
        o = b0 + j - i
        plane = jnp.where(jnp.abs(o) <= nside, nside - o, nwin)
        kb = k_ref[pl.ds(pl.multiple_of((b0 + j) * tq, tq), tq), :]
        s = _nt_dot(kb, q) + bias_ref[plane]
        st_ref[j] = s
        cm = jnp.max(s, axis=0, keepdims=True)
        m = cm if j == 0 else jnp.maximum(m, cm)
    l = acc = None
    for j in range(nwin):
        p = jnp.exp2(st_ref[j] - m)
        cl = jnp.sum(p, axis=0, keepdims=True)
        ca = jnp.dot(vt_ref[b0 + j], p.astype(BF16), preferred_element_type=F32)
        l = cl if j == 0 else l + cl
        acc = ca if j == 0 else acc + ca
    o_ref[...] = (acc / l).T.astype(o_ref.dtype)


def _dil_attention(proj, slopes, batch, seq, q_col, k_col, v_col, tq=256):
    span = max(w // 2 for w, _ in DILATED_PATTERNS)
    nside = span // tq
    nq = seq // tq
    ad, lm = _dil_tables(tq, span)
    tab_spec = pl.BlockSpec((2 * nside + 1, tq, tq), lambda b, h, i, s: (0, 0, 0))
    kern = functools.partial(_dil_kernel, tq=tq, nside=nside, nq=nq)
    return pl.pallas_call(
        kern,
        grid_spec=pltpu.PrefetchScalarGridSpec(
            num_scalar_prefetch=1,
            grid=(batch, N_HEADS_A, nq),
            in_specs=[pl.BlockSpec((tq, HEAD_DIM), lambda b, h, i, s: (b * nq + i, q_col + h)),
                      pl.BlockSpec((seq, HEAD_DIM), lambda b, h, i, s: (b, k_col + h)),
                      pl.BlockSpec((seq, HEAD_DIM), lambda b, h, i, s: (b, v_col + h)),
                      tab_spec, tab_spec],
            out_specs=pl.BlockSpec((tq, HEAD_DIM), lambda b, h, i, s: (b * nq + i, h)),
            scratch_shapes=[pltpu.VMEM((2 * nside + 2, tq, tq), F32),
                            pltpu.VMEM((nq, HEAD_DIM, tq), BF16),
                            pltpu.VMEM((2 * nside + 1, tq, tq), F32)]),
        out_shape=jax.ShapeDtypeStruct((batch * seq, N_HEADS_A * HEAD_DIM), BF16),
        compiler_params=_params(("arbitrary", "arbitrary", "arbitrary")),
        name="dil_attn",
    )(slopes, proj, proj, proj, ad, lm)


def _diff_kernel(slopes_ref, lq1_ref, lk1_ref, lq2_ref, lk2_ref, g_ref, q_ref, k_ref, v_ref, o_ref,
                 vt_ref, bias_ref, st1_ref, st2_ref, *, tq, tk, nk, lambda_init):
    h = pl.program_id(1)
    i = pl.program_id(2)

    seq = k_ref.shape[0]
    nq = seq // tq

    @pl.when(i == 0)
    def _():
        for kb in range(nk):
            cols = slice(kb * tk, (kb + 1) * tk)
            vt_ref[:, cols] = v_ref[cols, :].astype(F32).T.astype(BF16)

        neg_c = -slopes_ref[h] * LOG2E
        base = (lax.broadcasted_iota(I32, (tq, tq), 0) - lax.broadcasted_iota(I32, (tq, tq), 1)
                - (seq - tq))

        def fill(r, carry):
            rows = pl.ds(pl.multiple_of(r * tq, tq), tq)
            bias_ref[rows, :] = jnp.abs(base + r * tq).astype(F32) * neg_c
            return carry

        lax.fori_loop(0, 2 * nq - 1, fill, 0)

    q = q_ref[...]
    qs = (q[:, :HEAD_DIM], q[:, HEAD_DIM:])
    st_refs = (st1_ref, st2_ref)
    w0 = (nq - 1 - i) * tq
    mx = [None, None]
    for c in range(nk):
        rows = slice(c * tk, (c + 1) * tk)
        kc = k_ref[rows, :]
        b = bias_ref[pl.ds(pl.multiple_of(w0 + c * tk, tq), tk), :]
        for j in range(2):
            s = _nt_dot(kc[:, j * HEAD_DIM:(j + 1) * HEAD_DIM], qs[j]) + b
            st_refs[j][rows, :] = s
            cm = jnp.max(s, axis=0, keepdims=True)
            mx[j] = cm if c == 0 else jnp.maximum(mx[j], cm)
    ls = [None, None]
    accs = [None, None]
    for c in range(nk):
        rows = slice(c * tk, (c + 1) * tk)
        vt = vt_ref[:, rows]
        for j in range(2):
            p = jnp.exp2(st_refs[j][rows, :] - mx[j])
            cl = jnp.sum(p, axis=0, keepdims=True)
            ca = jnp.dot(vt, p.astype(BF16), preferred_element_type=F32)
            ls[j] = cl if c == 0 else ls[j] + cl
            accs[j] = ca if c == 0 else accs[j] + ca
    (a1, a2), (l1, l2) = accs, ls

    lam = (jnp.exp(jnp.sum(lq1_ref[...] * lk1_ref[...], axis=-1, keepdims=True))
           - jnp.exp(jnp.sum(lq2_ref[...] * lk2_ref[...], axis=-1, keepdims=True)) + lambda_init)
    ot = a1 / l1 - lam * (a2 / l2)
    yt = ot * lax.rsqrt(jnp.mean(ot * ot, axis=0, keepdims=True) + EPS)
    o_ref[...] = (yt.T * (g_ref[...] * (1.0 - lambda_init))).astype(o_ref.dtype)


def _diff_attention(proj, slopes, lam_vecs, subln_g, lambda_init, batch, seq, q_col, k_col, v_col,
                    tq=256, tk=512):
    nq = seq // tq
    w = 2 * HEAD_DIM
    vec_spec = pl.BlockSpec((1, HEAD_DIM), lambda b, h, i, s: (0, 0))
    kern = functools.partial(_diff_kernel, tq=tq, tk=tk, nk=seq // tk, lambda_init=lambda_init)
    return pl.pallas_call(
        kern,
        grid_spec=pltpu.PrefetchScalarGridSpec(
            num_scalar_prefetch=1,
            grid=(batch, N_HEADS_B, nq),
            in_specs=[vec_spec, vec_spec, vec_spec, vec_spec,
                      pl.BlockSpec((1, w), lambda b, h, i, s: (0, 0)),
                      pl.BlockSpec((tq, w), lambda b, h, i, s: (b * nq + i, q_col + h)),
                      pl.BlockSpec((seq, w), lambda b, h, i, s: (b, k_col + h)),
                      pl.BlockSpec((seq, w), lambda b, h, i, s: (b, v_col + h))],
            out_specs=pl.BlockSpec((tq, w), lambda b, h, i, s: (b * nq + i, h)),
            scratch_shapes=[pltpu.VMEM((w, seq), BF16),
                            pltpu.VMEM((2 * seq - tq, tq), F32),
                            pltpu.VMEM((seq, tq), F32),
                            pltpu.VMEM((seq, tq), F32)]),
        out_shape=jax.ShapeDtypeStruct((batch * seq, N_HEADS_B * w), BF16),
        compiler_params=_params(("arbitrary", "arbitrary", "arbitrary")),
        name="diff_attn",
    )(slopes, *lam_vecs, subln_g, proj, proj, proj)


def _merge_kernel(ya_ref, yb_ref, ga_ref, gb_ref, x_ref, g1_ref, sc2_ref, sh2_ref, n2g_ref,
                  woa_ref, wob_ref, wo_ref, wrh_ref, wrl_ref, br_ref,
                  h_ref, u2p_ref, lg_ref):
    a = jnp.dot(ya_ref[...], woa_ref[...], preferred_element_type=F32)
    b = jnp.dot(yb_ref[...], wob_ref[...], preferred_element_type=F32)
    merged = (jax.nn.sigmoid(ga_ref[...].astype(F32)) * a
              + jax.nn.sigmoid(gb_ref[...].astype(F32)) * b)
    h = x_ref[...] + g1_ref[...] * jnp.dot(merged.astype(BF16), wo_ref[...],
                                           preferred_element_type=F32)
    h_ref[...] = h
    y = h * lax.rsqrt(jnp.mean(h * h, axis=-1, keepdims=True) + EPS) * n2g_ref[...]
    u2 = y * (1.0 + sc2_ref[...]) + sh2_ref[...]
    hi, lo = _split_bf16(u2)
    lg = _nt_dot(wrh_ref[...], hi) + _nt_dot(wrl_ref[...], hi) + _nt_dot(wrh_ref[...], lo)
    lg_ref[...] = lg + br_ref[...]
    bits = pltpu.bitcast(hi.astype(F32), U32)
    half = bits.shape[1] // 2
    u2p_ref[...] = (bits[:, :half] >> 16) | (bits[:, half:] & jnp.uint32(0xFFFF0000))


def _merge(ya, yb, proj, x2, mod3, n2g, woa, wob, wo, wr_hi, wr_lo, br, seq, ga_col, gb_col, tm=256):
    t, d = x2.shape
    per_b = seq // tm
    wa = ya.shape[1]
    wb = yb.shape[1]
    ne = wr_hi.shape[0]

    def mod_spec(idx):
        return pl.BlockSpec((None, 1, d), lambda i: ((i // per_b) * N_MOD + idx, 0, 0))

    def const_spec(shape):
        return pl.BlockSpec(shape, lambda i: (0,) * len(shape), pipeline_mode=pl.Buffered(1))

    return pl.pallas_call(
        _merge_kernel,
        grid=(t // tm,),
        in_specs=[pl.BlockSpec((tm, wa), lambda i: (i, 0)),
                  pl.BlockSpec((tm, wb), lambda i: (i, 0)),
                  pl.BlockSpec((tm, d), lambda i: (i, ga_col)),
                  pl.BlockSpec((tm, d), lambda i: (i, gb_col)),
                  pl.BlockSpec((tm, d), lambda i: (i, 0)),
                  mod_spec(2), mod_spec(4), mod_spec(3),
                  const_spec((1, d)),
                  const_spec((wa, d)), const_spec((wb, d)), const_spec((d, d)),
                  const_spec((ne, d)), const_spec((ne, d)), const_spec((ne, 1))],
        out_specs=[pl.BlockSpec((tm, d), lambda i: (i, 0)),
                   pl.BlockSpec((tm, d // 2), lambda i: (i, 0)),
                   pl.BlockSpec((ne, tm), lambda i: (0, i))],
        out_shape=[jax.ShapeDtypeStruct((t, d), F32),
                   jax.ShapeDtypeStruct((t, d // 2), U32),
                   jax.ShapeDtypeStruct((ne, t), F32)],
        compiler_params=_params(("arbitrary",)),
        name="merge",
    )(ya, yb, proj, proj, x2, mod3, mod3, mod3, n2g, woa, wob, wo, wr_hi, wr_lo, br)


def _route_kernel(lg_ref, p_ref, dest_ref, cnt_ref, carry_ref, total_ref):
    sweep = pl.program_id(0)

    @pl.when(pl.program_id(1) == 0)
    def _():
        @pl.when(sweep == 0)
        def _():
            total_ref[...] = jnp.zeros_like(total_ref)

        @pl.when(sweep == 1)
        def _():
            total_ref[...] = carry_ref[...]

        carry_ref[...] = jnp.zeros_like(carry_ref)

    lg = lg_ref[...]
    ne, tr = lg.shape
    eio = lax.broadcasted_iota(I32, (ne, tr), 0)
    work = lg
    vals, hots = [], []
    for k in range(TOP_K):
        mx = jnp.max(work, axis=0, keepdims=True)
        am = jnp.min(jnp.where(work == mx, eio, ne), axis=0, keepdims=True)
        hot = eio == am
        vals.append(mx)
        hots.append(hot)
        work = jnp.where(hot, -jnp.inf, work)
    exps = [jnp.exp(v - vals[0]) for v in vals]
    denom = functools.reduce(jnp.add, exps)
    for k in range(TOP_K):
        p_ref[k:k + 1, :] = exps[k] / denom
    chosen = functools.reduce(jnp.logical_or, hots)
    sel = jnp.where(chosen, 1.0, 0.0)
    tri = (lax.broadcasted_iota(I32, (tr, tr), 0) < lax.broadcasted_iota(I32, (tr, tr), 1))
    before = jnp.dot(sel.astype(BF16), jnp.where(tri, 1.0, 0.0).astype(BF16),
                     preferred_element_type=F32)
    carry = carry_ref[...]
    nblk = jnp.floor((total_ref[...][:, 0:1] + (ROW_BLOCK - 1)) * (1.0 / ROW_BLOCK))
    e_row = lax.broadcasted_iota(I32, (ne, ne), 0)
    e_col = lax.broadcasted_iota(I32, (ne, ne), 1)
    nblk_lanes = jnp.sum(jnp.where(e_row == e_col, nblk, 0.0), axis=0, keepdims=True)
    first_row = jnp.sum(jnp.where(e_col < e_row, nblk_lanes, 0.0), axis=1,
                        keepdims=True) * ROW_BLOCK
    place = before + (carry[:, 0:1] + first_row)
    for k in range(TOP_K):
        dest_ref[k:k + 1, :] = jnp.sum(jnp.where(hots[k], place, 0.0), axis=0,
                                       keepdims=True).astype(I32)
    carry = carry + jnp.sum(sel, axis=1, keepdims=True)
    carry_ref[...] = carry
    cnt_ref[...] = carry.astype(I32)


def _route(logits_t, tr=512):
    ne, t = logits_t.shape
    slot_spec = pl.BlockSpec((None, TOP_K, tr), lambda s, i: (s, 0, i))
    probs, dest, cnt = pl.pallas_call(
        _route_kernel,
        grid=(2, t // tr),
        in_specs=[pl.BlockSpec((ne, tr), lambda s, i: (0, i))],
        out_specs=[slot_spec, slot_spec,
                   pl.BlockSpec((None, ne, LANES), lambda s, i: (s, 0, 0))],
        out_shape=[jax.ShapeDtypeStruct((2, TOP_K, t), F32),
                   jax.ShapeDtypeStruct((2, TOP_K, t), I32),
                   jax.ShapeDtypeStruct((2, ne, LANES), I32)],
        scratch_shapes=[pltpu.VMEM((ne, LANES), F32), pltpu.VMEM((ne, LANES), F32)],
        compiler_params=_params(("arbitrary", "arbitrary")),
        name="route",
    )(logits_t)
    return probs[1], dest[1], cnt[1]


def _dispatch_kernel(dest_ref, fill_ref, nfill_ref, src_ref, dst_ref, sem, fill_sem, *, tm, t_total):
    i = pl.program_id(0)

    @pl.when(i == 0)
    def _():
        def fill(b):
            rows = pl.ds(pl.multiple_of(fill_ref[b] * ROW_BLOCK, ROW_BLOCK), ROW_BLOCK)
            return pltpu.make_async_copy(src_ref.at[pl.ds(0, ROW_BLOCK)], dst_ref.at[rows], fill_sem)

        def fill_start(b, c):
            fill(b).start()
            return c

        def fill_wait(b, c):
            fill(b).wait()
            return c

        lax.fori_loop(0, nfill_ref[0], fill_start, 0)
        lax.fori_loop(0, nfill_ref[0], fill_wait, 0)

    for k in range(TOP_K):
        def one(j, c):
            row = dest_ref[k * t_total + i * tm + j]
            pltpu.make_async_copy(src_ref.at[pl.ds(j, 1)], dst_ref.at[pl.ds(row, 1)], sem).start()
            return c
        lax.fori_loop(0, tm, one, 0, unroll=8)
    for k in range(TOP_K):
        pltpu.make_async_copy(src_ref, dst_ref.at[pl.ds(0, tm)], sem).wait()


def _dispatch(dest_flat, fill_blocks, nfill, u2p, total_rows, tm=512):
    t, w = u2p.shape
    kern = functools.partial(_dispatch_kernel, tm=tm, t_total=t)
    return pl.pallas_call(
        kern,
        grid_spec=pltpu.PrefetchScalarGridSpec(
            num_scalar_prefetch=3,
            grid=(t // tm,),
            in_specs=[pl.BlockSpec((tm, w), lambda i, d, fb, nf: (i, 0))],
            out_specs=pl.BlockSpec(memory_space=pl.ANY),
            scratch_shapes=[pltpu.SemaphoreType.DMA(()), pltpu.SemaphoreType.DMA(())]),
        out_shape=jax.ShapeDtypeStruct((total_rows, w), U32),
        compiler_params=_params(("arbitrary",)),
        name="dispatch",
    )(dest_flat, fill_blocks, nfill, u2p)


def _moe_kernel(che_ref, chblk_ref, chn_ref, nvalid_ref,
                xs_ref, wg_ref, wu_ref, wd_ref, bg_ref, bu_ref, bd_ref,
                ys_ref,
                xin_ref, xbf_ref, yacc_ref, act_ref, wgb_ref, wub_ref, wdb_ref, in_sem, out_sem,
                *, nff, nsplit, mm_rows):
    c = pl.program_id(0)
    f = pl.program_id(1)
    nvalid = nvalid_ref[0]
    half = xin_ref.shape[1]

    def in_copy(cc, j):
        row = pl.multiple_of((chblk_ref[cc] + j) * ROW_BLOCK, ROW_BLOCK)
        return pltpu.make_async_copy(xs_ref.at[pl.ds(row, ROW_BLOCK)],
                                     xin_ref.at[pl.ds(j * ROW_BLOCK, ROW_BLOCK)], in_sem)

    def out_copy(cc, j):
        row = pl.multiple_of((chblk_ref[cc] + j) * ROW_BLOCK, ROW_BLOCK)
        return pltpu.make_async_copy(yacc_ref.at[pl.ds(j * ROW_BLOCK, ROW_BLOCK)],
                                     ys_ref.at[pl.ds(row, ROW_BLOCK)], out_sem)

    def for_blocks(cc, fn):
        def one(j, carry):
            fn(cc, j)
            return carry
        lax.fori_loop(0, chn_ref[cc], one, 0)

    @pl.when(c < nvalid)
    def _():
        @pl.when(f == 0)
        def _():
            @pl.when(c == 0)
            def _():
                xbf_ref[...] = jnp.zeros_like(xbf_ref)
                for_blocks(c, lambda cc, j: in_copy(cc, j).start())

            for_blocks(c, lambda cc, j: in_copy(cc, j).wait())

            def unpack(cc, j):
                rows = pl.ds(pl.multiple_of(j * ROW_BLOCK, ROW_BLOCK), ROW_BLOCK)
                w = xin_ref[rows, :]
                xbf_ref[rows, :half] = pltpu.bitcast(w << 16, F32).astype(BF16)
                xbf_ref[rows, half:] = pltpu.bitcast(w & jnp.uint32(0xFFFF0000), F32).astype(BF16)

            for_blocks(c, unpack)

            @pl.when(c + 1 < nvalid)
            def _():
                for_blocks(c + 1, lambda cc, j: in_copy(cc, j).start())

        wgb_ref[...] = wg_ref[...].astype(BF16)
        wub_ref[...] = wu_ref[...].astype(BF16)
        nrows = xbf_ref.shape[0]
        for r0 in range(0, nrows, mm_rows):
            rows = slice(r0, r0 + mm_rows)
            x = xbf_ref[rows, :]
            g = jnp.minimum(jnp.dot(x, wgb_ref[...], preferred_element_type=F32) + bg_ref[...],
                            SWIGLU_LIMIT)
            u = jnp.clip(jnp.dot(x, wub_ref[...], preferred_element_type=F32) + bu_ref[...],
                         -SWIGLU_LIMIT, SWIGLU_LIMIT)
            act_ref[rows, :] = ((u + 1.0) * (g * jax.nn.sigmoid(SWIGLU_ALPHA * g))).astype(BF16)
        wdb_ref[...] = wd_ref[...].astype(BF16)

        @pl.when((f == 0) & (c > 0))
        def _():
            for_blocks(c - 1, lambda cc, j: out_copy(cc, j).wait())

        first = f == 0
        ncol = yacc_ref.shape[1] // nsplit
        for r0 in range(0, nrows, mm_rows):
            rows = slice(r0, r0 + mm_rows)
            a = act_ref[rows, :]
            for s in range(nsplit):
                cols = slice(s * ncol, (s + 1) * ncol)
                part = jnp.dot(a, wdb_ref[:, cols], preferred_element_type=F32)
                base = jnp.where(first, jnp.broadcast_to(bd_ref[:, cols], part.shape),
                                 yacc_ref[rows, cols])
                yacc_ref[rows, cols] = base + part

        @pl.when(f == nff - 1)
        def _():
            for_blocks(c, lambda cc, j: out_copy(cc, j).start())

            @pl.when(c == nvalid - 1)
            def _():
                for_blocks(c, lambda cc, j: out_copy(cc, j).wait())

                def fill(b):
                    rows = pl.ds(pl.multiple_of(b * ROW_BLOCK, ROW_BLOCK), ROW_BLOCK)
                    return pltpu.make_async_copy(yacc_ref.at[pl.ds(0, ROW_BLOCK)], ys_ref.at[rows],
                                                 out_sem)

                def fill_start(b, carry):
                    fill(b).start()
                    return carry

                def fill_wait(b, carry):
                    fill(b).wait()
                    return carry

                nused = chblk_ref[c] + chn_ref[c]
                ntotal = ys_ref.shape[0] // ROW_BLOCK
                lax.fori_loop(nused, ntotal, fill_start, 0)
                lax.fori_loop(nused, ntotal, fill_wait, 0)


def _moe(che, chblk, chn, nvalid, xs, wg, wu, wd, bg, bu, bd, nchunks):
    ne, d, dff = wg.shape
    rows = CHUNK_BLOCKS * ROW_BLOCK
    nff = dff // FF_TILE

    def ff(c, f, nv):
        return jnp.where(c < nv[0], f, nff - 1)

    kern = functools.partial(_moe_kernel, nff=nff, nsplit=4, mm_rows=MM_ROW_BLOCKS * ROW_BLOCK)
    return pl.pallas_call(
        kern,
        grid_spec=pltpu.PrefetchScalarGridSpec(
            num_scalar_prefetch=4,
            grid=(nchunks, nff),
            in_specs=[pl.BlockSpec(memory_space=pl.ANY),
                      pl.BlockSpec((None, d, FF_TILE), lambda c, f, e, b, n, nv: (e[c], 0, ff(c, f, nv))),
                      pl.BlockSpec((None, d, FF_TILE), lambda c, f, e, b, n, nv: (e[c], 0, ff(c, f, nv))),
                      pl.BlockSpec((None, FF_TILE, d), lambda c, f, e, b, n, nv: (e[c], ff(c, f, nv), 0)),
                      pl.BlockSpec((None, 1, FF_TILE), lambda c, f, e, b, n, nv: (e[c], 0, ff(c, f, nv))),
                      pl.BlockSpec((None, 1, FF_TILE), lambda c, f, e, b, n, nv: (e[c], 0, ff(c, f, nv))),
                      pl.BlockSpec((None, 1, d), lambda c, f, e, b, n, nv: (e[c], 0, 0))],
            out_specs=pl.BlockSpec(memory_space=pl.ANY),
            scratch_shapes=[pltpu.VMEM((rows, d // 2), U32),
                            pltpu.VMEM((rows, d), BF16),
                            pltpu.VMEM((rows, d), F32),
                            pltpu.VMEM((rows, FF_TILE), BF16),
                            pltpu.VMEM((d, FF_TILE), BF16),
                            pltpu.VMEM((d, FF_TILE), BF16),
                            pltpu.VMEM((FF_TILE, d), BF16),
                            pltpu.SemaphoreType.DMA(()),
                            pltpu.SemaphoreType.DMA(())]),
        out_shape=jax.ShapeDtypeStruct((xs.shape[0], d), F32),
        compiler_params=_params(("arbitrary", "arbitrary")),
        name="moe",
    )(che, chblk, chn, nvalid, xs, wg, wu, wd, bg, bu, bd)


def _combine_kernel(dest_ref, ys_ref, h_ref, p_ref, g2_ref, fg_ref, o_ref, buf_ref, sem,
                    *, tc, nsteps, final_norm):
    i = pl.program_id(0)
    t_total = nsteps * tc

    def issue(step, slot):
        for k in range(TOP_K):
            def one(j, c):
                src = dest_ref[k * t_total + step * tc + j]
                pltpu.make_async_copy(ys_ref.at[pl.ds(src, 1)], buf_ref.at[slot, k, pl.ds(j, 1)],
                                      sem.at[slot]).start()
                return c
            lax.fori_loop(0, tc, one, 0, unroll=8)

    def drain(slot):
        for k in range(TOP_K):
            pltpu.make_async_copy(ys_ref.at[pl.ds(0, tc)], buf_ref.at[slot, k], sem.at[slot]).wait()

    slot = i % 2

    @pl.when(i == 0)
    def _():
        issue(0, 0)

    @pl.when(i + 1 < nsteps)
    def _():
        issue(i + 1, 1 - slot)

    drain(slot)
    p = p_ref[...]
    moe = p[:, 0:1] * buf_ref[slot, 0]
    for k in range(1, TOP_K):
        moe += p[:, k:k + 1] * buf_ref[slot, k]
    h = h_ref[...] + g2_ref[...] * moe
    if final_norm:
        h = h * lax.rsqrt(jnp.mean(h * h, axis=-1, keepdims=True) + EPS) * fg_ref[...]
    o_ref[...] = h


def _combine(dest_flat, ys, h, probs_t, mod3, final_g, seq, final_norm, tc=256):
    t, d = h.shape
    per_b = seq // tc
    nsteps = t // tc
    kern = functools.partial(_combine_kernel, tc=tc, nsteps=nsteps, final_norm=final_norm)
    return pl.pallas_call(
        kern,
        grid_spec=pltpu.PrefetchScalarGridSpec(
            num_scalar_prefetch=1,
            grid=(nsteps,),
            in_specs=[pl.BlockSpec(memory_space=pl.ANY),
                      pl.BlockSpec((tc, d), lambda i, s: (i, 0)),
                      pl.BlockSpec((tc, TOP_K), lambda i, s: (i, 0)),
                      pl.BlockSpec((None, 1, d), lambda i, s: ((i // per_b) * N_MOD + 5, 0, 0)),
                      pl.BlockSpec((1, d), lambda i, s: (0, 0))],
            out_specs=pl.BlockSpec((tc, d), lambda i, s: (i, 0)),
            scratch_shapes=[pltpu.VMEM((2, TOP_K, tc, d), F32),
                            pltpu.SemaphoreType.DMA((2,))]),
        out_shape=jax.ShapeDtypeStruct((t, d), F32),
        compiler_params=_params(("arbitrary",)),
        name="combine",
    )(dest_flat, ys, h, probs_t, mod3, final_g)


def _routing_plan(counts, t):
    nblk_total = -(-t * TOP_K // ROW_BLOCK) + N_EXPERTS
    nblk = (counts + ROW_BLOCK - 1) // ROW_BLOCK
    blk_end = jnp.cumsum(nblk)
    blk_start = blk_end - nblk
    ntail_max = nblk_total - t * TOP_K // ROW_BLOCK
    fill_blocks = jnp.concatenate([jnp.maximum(blk_end - 1, 0),
                                   jnp.minimum(blk_end[-1] + jnp.arange(ntail_max), nblk_total - 1)])
    nfill = N_EXPERTS + nblk_total - blk_end[-1]
    nchunks = -(-nblk_total // CHUNK_BLOCKS) + N_EXPERTS
    nch = (nblk + CHUNK_BLOCKS - 1) // CHUNK_BLOCKS
    ch_end = jnp.cumsum(nch)
    ch_start = ch_end - nch
    nvalid = ch_end[-1]
    cid = jnp.minimum(jnp.arange(nchunks, dtype=I32), nvalid - 1)
    che = jnp.clip(jnp.searchsorted(ch_end, cid, side='right'), 0, N_EXPERTS - 1).astype(I32)
    local = cid - ch_start[che]
    chblk = (blk_start[che] + local * CHUNK_BLOCKS).astype(I32)
    chn = jnp.minimum(CHUNK_BLOCKS, nblk[che] - local * CHUNK_BLOCKS).astype(I32)
    return (fill_blocks.astype(I32), nfill.astype(I32).reshape(1), che, chblk, chn,
            nvalid.astype(I32).reshape(1), nblk_total * ROW_BLOCK, nchunks)


def kernel(x, c, w_ada, b_ada, norm1_g, w_in, lam_q1, lam_k1, lam_q2, lam_k2, subln_g, w_out_a, w_out_b, w_o, norm2_g, w_router, b_router, w_gate, b_gate, w_up, b_up, w_down, b_down, final_g):
    batch, seq, d = x.shape
    t = batch * seq
    depth = w_ada.shape[0]
    wa = N_HEADS_A * HEAD_DIM
    wb = N_HEADS_B * 2 * HEAD_DIM
    slopes = _alibi_slopes(N_HEADS_A + N_HEADS_B)
    slopes_a = jnp.asarray(slopes[:N_HEADS_A])
    slopes_b = jnp.asarray(slopes[N_HEADS_A:])
    c8 = jnp.pad(c, ((0, 8 - batch), (0, 0)))

    h = x.reshape(t, d)
    for l in range(depth):
        lambda_init = 0.8 - 0.6 * math.exp(-0.3 * l)
        mod = _ada(c8, w_ada[l], b_ada[l].reshape(1, -1))[:batch]
        mod3 = mod.reshape(batch * N_MOD, 1, d)

        u = _norm_mod(h, norm1_g[l].reshape(1, d), mod3, 1, 0, seq)
        proj = _inproj(u, w_in[l], q_tiles=(0, 3 * wa // 1024))
        y_a = _dil_attention(proj, slopes_a, batch, seq, 0, wa // HEAD_DIM, 2 * wa // HEAD_DIM)
        off_b = 3 * wa // (2 * HEAD_DIM)
        nb = wb // (2 * HEAD_DIM)
        y_b = _diff_attention(proj, slopes_b,
                              [v[l].reshape(1, HEAD_DIM) for v in (lam_q1, lam_k1, lam_q2, lam_k2)],
                              subln_g[l].reshape(1, -1), lambda_init, batch, seq,
                              off_b, off_b + nb, off_b + 2 * nb)
        wr_hi, wr_lo = _split_bf16(w_router[l].T)
        gate_col = (3 * wa + 3 * wb) // d
        h, u2p, logits_t = _merge(y_a, y_b, proj, h, mod3, norm2_g[l].reshape(1, d),
                                  w_out_a[l].astype(BF16), w_out_b[l].astype(BF16),
                                  w_o[l].astype(BF16), wr_hi, wr_lo, b_router[l].reshape(-1, 1),
                                  seq, gate_col, gate_col + 1)
        probs, dest, cnt = _route(logits_t)
        (fill_blocks, nfill, che, chblk, chn, nvalid, total_rows, nchunks) = _routing_plan(
            cnt[:, 0], t)
        xs = _dispatch(dest.reshape(-1), fill_blocks, nfill, u2p, total_rows)
        ys = _moe(che, chblk, chn, nvalid, xs, w_gate[l], w_up[l], w_down[l],
                  b_gate[l].reshape(N_EXPERTS, 1, -1), b_up[l].reshape(N_EXPERTS, 1, -1),
                  b_down[l].reshape(N_EXPERTS, 1, -1), nchunks)
        h = _combine(dest.reshape(-1), ys, h, probs.T, mod3, final_g.reshape(1, d), seq,
                     final_norm=(l == depth - 1))
    return h.reshape(batch, seq, d)
```

```python
import functools
import math

import numpy as np
import jax
import jax.numpy as jnp
from jax import lax
from jax.experimental import pallas as pl
from jax.experimental.pallas import tpu as pltpu

F32 = jnp.float32
BF16 = jnp.bfloat16
U32 = jnp.uint32
I32 = jnp.int32

HEAD_DIM = 128
N_HEADS_A = 8
N_HEADS_B = 4
DILATED_PATTERNS = ((128, 1), (512, 4), (2048, 16))
N_EXPERTS = 32
TOP_K = 4
SWIGLU_LIMIT = 7.0
SWIGLU_ALPHA = 1.702
N_MOD = 6
EPS = 1e-5
NEG = -1e30
LOG2E = math.log2(math.e)

LANES = 128
SUBLANES = 8
V7X_VMEM_LIMIT = 56 * 1024 * 1024

ROW_BLOCK = 128
CHUNK_BLOCKS = 9
MM_ROW_BLOCKS = 3
FF_TILE = 512


def _alibi_slopes(n):
    return np.array([2.0 ** (-8.0 * (i + 1) / n) for i in range(n)], dtype=np.float32)


def _nt_dot(a, b):
    return lax.dot_general(a, b, (((1,), (1,)), ((), ())), preferred_element_type=F32)


def _split_bf16(x):
    hi = x.astype(BF16)
    lo = (x - hi.astype(F32)).astype(BF16)
    return hi, lo


def _params(sem, vmem=V7X_VMEM_LIMIT):
    return pltpu.CompilerParams(dimension_semantics=sem, vmem_limit_bytes=vmem)


def _ada_kernel(c_ref, w_ref, b_ref, o_ref):
    c = c_ref[...]
    a = c * jax.nn.sigmoid(c)
    a_hi, a_lo = _split_bf16(a)
    w_hi, w_lo = _split_bf16(w_ref[...])
    acc = jnp.dot(a_hi, w_hi, preferred_element_type=F32)
    acc += jnp.dot(a_lo, w_hi, preferred_element_type=F32)
    acc += jnp.dot(a_hi, w_lo, preferred_element_type=F32)
    o_ref[...] = acc + b_ref[...]


def _ada(c8, w, b, tn=1024):
    m, d = c8.shape
    n = w.shape[1]
    return pl.pallas_call(
        _ada_kernel,
        grid=(n // tn,),
        in_specs=[pl.BlockSpec((m, d), lambda j: (0, 0)),
                  pl.BlockSpec((d, tn), lambda j: (0, j)),
                  pl.BlockSpec((1, tn), lambda j: (0, j))],
        out_specs=pl.BlockSpec((m, tn), lambda j: (0, j)),
        out_shape=jax.ShapeDtypeStruct((m, n), F32),
        compiler_params=_params(("arbitrary",)),
        name="ada",
    )(c8, w, b)


def _norm_mod_kernel(x_ref, g_ref, sc_ref, sh_ref, o_ref):
    x = x_ref[...]
    y = x * lax.rsqrt(jnp.mean(x * x, axis=-1, keepdims=True) + EPS) * g_ref[...]
    o_ref[...] = (y * (1.0 + sc_ref[...]) + sh_ref[...]).astype(o_ref.dtype)


def _norm_mod(x2, g, mod3, i_scale, i_shift, seq, tm=512):
    t, d = x2.shape
    per_b = seq // tm
    return pl.pallas_call(
        _norm_mod_kernel,
        grid=(t // tm,),
        in_specs=[pl.BlockSpec((tm, d), lambda i: (i, 0)),
                  pl.BlockSpec((1, d), lambda i: (0, 0)),
                  pl.BlockSpec((None, 1, d), lambda i: ((i // per_b) * N_MOD + i_scale, 0, 0)),
                  pl.BlockSpec((None, 1, d), lambda i: ((i // per_b) * N_MOD + i_shift, 0, 0))],
        out_specs=pl.BlockSpec((tm, d), lambda i: (i, 0)),
        out_shape=jax.ShapeDtypeStruct((t, d), BF16),
        compiler_params=_params(("arbitrary",)),
        name="norm1",
    )(x2, g, mod3, mod3)


def _inproj_kernel(u_ref, w_ref, o_ref, wbf_ref, *, q_tiles, scale):
    n = pl.program_id(0)

    @pl.when(pl.program_id(1) == 0)
    def _():
        wbf_ref[...] = w_ref[...].astype(BF16)

    acc = jnp.dot(u_ref[...], wbf_ref[...], preferred_element_type=F32)
    is_q = functools.reduce(jnp.logical_or, [n == q for q in q_tiles])
    o_ref[...] = (acc * jnp.where(is_q, scale, 1.0)).astype(BF16)


def _inproj(u, w, q_tiles, tm=1024, tn=1024):
    t, d = u.shape
    n = w.shape[1]
    return pl.pallas_call(
        functools.partial(_inproj_kernel, q_tiles=q_tiles, scale=HEAD_DIM ** -0.5 * LOG2E),
        grid=(n // tn, t // tm),
        in_specs=[pl.BlockSpec((tm, d), lambda j, i: (i, 0)),
                  pl.BlockSpec((d, tn), lambda j, i: (0, j))],
        out_specs=pl.BlockSpec((tm, tn), lambda j, i: (i, j)),
        out_shape=jax.ShapeDtypeStruct((t, n), BF16),
        scratch_shapes=[pltpu.VMEM((d, tn), BF16)],
        compiler_params=_params(("arbitrary", "arbitrary")),
        name="inproj",
    )(u, w)


def _dil_tables(tq, span):
    nside = span // tq
    o = lax.broadcasted_iota(I32, (2 * nside + 1, tq, tq), 0) - nside
    i = lax.broadcasted_iota(I32, (2 * nside + 1, tq, tq), 1)
    j = lax.broadcasted_iota(I32, (2 * nside + 1, tq, tq), 2)
    ad = jnp.abs(o * tq + j - i)
    mult = jnp.zeros_like(ad)
    for window, dil in DILATED_PATTERNS:
        mult += ((ad % dil == 0) & (ad // dil <= window // (2 * dil))).astype(I32)
    lm = jnp.where(mult > 0, jnp.log2(jnp.maximum(mult, 1).astype(F32)), NEG)
    return ad.astype(F32) * LOG2E, lm


def _dil_kernel(slopes_ref, q_ref, k_ref, v_ref, ad_ref, lm_ref, o_ref, bias_ref, vt_ref, st_ref,
                *, tq, nside, nq):
    h = pl.program_id(1)
    i = pl.program_id(2)

    @pl.when(i == 0)
    def _():
        bias_ref[:2 * nside + 1] = lm_ref[...] - slopes_ref[h] * ad_ref[...]
        bias_ref[2 * nside + 1] = jnp.full((tq, tq), NEG, F32)
        for kb in range(nq):
            vt_ref[kb] = v_ref[kb * tq:(kb + 1) * tq, :].astype(F32).T.astype(BF16)

    nwin = 2 * nside + 1
    b0 = jnp.clip(i - nside, 0, nq - nwin)
    q = q_ref[...]
    m = None
    for j in range(nwin):
        o = b0 + j - i
        plane = jnp.where(jnp.abs(o) <= nside, nside - o, nwin)
        kb = k_ref[pl.ds(pl.multiple_of((b0 + j) * tq, tq), tq), :]
        s = _nt_dot(kb, q) + bias_ref[plane]
        st_ref[j] = s
        cm = jnp.max(s, axis=0, keepdims=True)
        m = cm if j == 0 else jnp.maximum(m, cm)
    l = acc = None
    for j in range(nwin):
        p = jnp.exp2(st_ref[j] - m)
        cl = jnp.sum(p, axis=0, keepdims=True)
        ca = jnp.dot(vt_ref[b0 + j], p.astype(BF16), preferred_element_type=F32)
        l = cl if j == 0 else l + cl
        acc = ca if j == 0 else acc + ca
    o_ref[...] = (acc / l).T.astype(o_ref.dtype)


def _dil_attention(proj, slopes, batch, seq, q_col, k_col, v_col, tq=256):
    span = max(w // 2 for w, _ in DILATED_PATTERNS)
    nside = span // tq
    nq = seq // tq
    ad, lm = _dil_tables(tq, span)
    tab_spec = pl.BlockSpec((2 * nside + 1, tq, tq), lambda b, h, i, s: (0, 0, 0))
    kern = functools.partial(_dil_kernel, tq=tq, nside=nside, nq=nq)
    return pl.pallas_call(
        kern,
        grid_spec=pltpu.PrefetchScalarGridSpec(
            num_scalar_prefetch=1,
            grid=(batch, N_HEADS_A, nq),
            in_specs=[pl.BlockSpec((tq, HEAD_DIM), lambda b, h, i, s: (b * nq + i, q_col + h)),
                      pl.BlockSpec((seq, HEAD_DIM), lambda b, h, i, s: (b, k_col + h)),
                      pl.BlockSpec((seq, HEAD_DIM), lambda b, h, i, s: (b, v_col + h)),
                      tab_spec, tab_spec],
            out_specs=pl.BlockSpec((tq, HEAD_DIM), lambda b, h, i, s: (b * nq + i, h)),
            scratch_shapes=[pltpu.VMEM((2 * nside + 2, tq, tq), F32),
                            pltpu.VMEM((nq, HEAD_DIM, tq), BF16),
                            pltpu.VMEM((2 * nside + 1, tq, tq), F32)]),
        out_shape=jax.ShapeDtypeStruct((batch * seq, N_HEADS_A * HEAD_DIM), BF16),
        compiler_params=_params(("arbitrary", "arbitrary", "arbitrary")),
        name="dil_attn",
    )(slopes, proj, proj, proj, ad, lm)


def _diff_kernel(slopes_ref, lq1_ref, lk1_ref, lq2_ref, lk2_ref, g_ref, q_ref, k_ref, v_ref, o_ref,
                 vt_ref, bias_ref, st1_ref, st2_ref, *, tq, tk, nk, lambda_init):
    h = pl.program_id(1)
    i = pl.program_id(2)

    seq = k_ref.shape[0]
    nq = seq // tq

    @pl.when(i == 0)
    def _():
        for kb in range(nk):
            cols = slice(kb * tk, (kb + 1) * tk)
            vt_ref[:, cols] = v_ref[cols, :].astype(F32).T.astype(BF16)

        neg_c = -slopes_ref[h] * LOG2E
        base = (lax.broadcasted_iota(I32, (tq, tq), 0) - lax.broadcasted_iota(I32, (tq, tq), 1)
                - (seq - tq))

        def fill(r, carry):
            rows = pl.ds(pl.multiple_of(r * tq, tq), tq)
            bias_ref[rows, :] = jnp.abs(base + r * tq).astype(F32) * neg_c
            return carry

        lax.fori_loop(0, 2 * nq - 1, fill, 0)

    q = q_ref[...]
    qs = (q[:, :HEAD_DIM], q[:, HEAD_DIM:])
    st_refs = (st1_ref, st2_ref)
    w0 = (nq - 1 - i) * tq
    mx = [None, None]
    for c in range(nk):
        rows = slice(c * tk, (c + 1) * tk)
        kc = k_ref[rows, :]
        b = bias_ref[pl.ds(pl.multiple_of(w0 + c * tk, tq), tk), :]
        for j in range(2):
            s = _nt_dot(kc[:, j * HEAD_DIM:(j + 1) * HEAD_DIM], qs[j]) + b
            st_refs[j][rows, :] = s
            cm = jnp.max(s, axis=0, keepdims=True)
            mx[j] = cm if c == 0 else jnp.maximum(mx[j], cm)
    ls = [None, None]
    accs = [None, None]
    for c in range(nk):
        rows = slice(c * tk, (c + 1) * tk)
        vt = vt_ref[:, rows]
        for j in range(2):
            p = jnp.exp2(st_refs[j][rows, :] - mx[j])
            cl = jnp.sum(p, axis=0, keepdims=True)
            ca = jnp.dot(vt, p.astype(BF16), preferred_element_type=F32)
            ls[j] = cl if c == 0 else ls[j] + cl
            accs[j] = ca if c == 0 else accs[j] + ca
    (a1, a2), (l1, l2) = accs, ls

    lam = (jnp.exp(jnp.sum(lq1_ref[...] * lk1_ref[...], axis=-1, keepdims=True))
           - jnp.exp(jnp.sum(lq2_ref[...] * lk2_ref[...], axis=-1, keepdims=True)) + lambda_init)
    ot = a1 / l1 - lam * (a2 / l2)
    yt = ot * lax.rsqrt(jnp.mean(ot * ot, axis=0, keepdims=True) + EPS)
    o_ref[...] = (yt.T * (g_ref[...] * (1.0 - lambda_init))).astype(o_ref.dtype)


def _diff_attention(proj, slopes, lam_vecs, subln_g, lambda_init, batch, seq, q_col, k_col, v_col,
                    tq=256, tk=512):
    nq = seq // tq
    w = 2 * HEAD_DIM
    vec_spec = pl.BlockSpec((1, HEAD_DIM), lambda b, h, i, s: (0, 0))
    kern = functools.partial(_diff_kernel, tq=tq, tk=tk, nk=seq // tk, lambda_init=lambda_init)
    return pl.pallas_call(
        kern,
        grid_spec=pltpu.PrefetchScalarGridSpec(
            num_scalar_prefetch=1,
            grid=(batch, N_HEADS_B, nq),
            in_specs=[vec_spec, vec_spec, vec_spec, vec_spec,
                      pl.BlockSpec((1, w), lambda b, h, i, s: (0, 0)),
                      pl.BlockSpec((tq, w), lambda b, h, i, s: (b * nq + i, q_col + h)),
                      pl.BlockSpec((seq, w), lambda b, h, i, s: (b, k_col + h)),
                      pl.BlockSpec((seq, w), lambda b, h, i, s: (b, v_col + h))],
            out_specs=pl.BlockSpec((tq, w), lambda b, h, i, s: (b * nq + i, h)),
            scratch_shapes=[pltpu.VMEM((w, seq), BF16),
                            pltpu.VMEM((2 * seq - tq, tq), F32),
                            pltpu.VMEM((seq, tq), F32),
                            pltpu.VMEM((seq, tq), F32)]),
        out_shape=jax.ShapeDtypeStruct((batch * seq, N_HEADS_B * w), BF16),
        compiler_params=_params(("arbitrary", "arbitrary", "arbitrary")),
        name="diff_attn",
    )(slopes, *lam_vecs, subln_g, proj, proj, proj)


def _merge_kernel(ya_ref, yb_ref, ga_ref, gb_ref, x_ref, g1_ref, sc2_ref, sh2_ref, n2g_ref,
                  woa_ref, wob_ref, wo_ref, wrh_ref, wrl_ref, br_ref,
                  h_ref, u2p_ref, lg_ref):
    a = jnp.dot(ya_ref[...], woa_ref[...], preferred_element_type=F32)
    b = jnp.dot(yb_ref[...], wob_ref[...], preferred_element_type=F32)
    merged = (jax.nn.sigmoid(ga_ref[...].astype(F32)) * a
              + jax.nn.sigmoid(gb_ref[...].astype(F32)) * b)
    h = x_ref[...] + g1_ref[...] * jnp.dot(merged.astype(BF16), wo_ref[...],
                                           preferred_element_type=F32)
    h_ref[...] = h
    y = h * lax.rsqrt(jnp.mean(h * h, axis=-1, keepdims=True) + EPS) * n2g_ref[...]
    u2 = y * (1.0 + sc2_ref[...]) + sh2_ref[...]
    hi, lo = _split_bf16(u2)
    lg = _nt_dot(wrh_ref[...], hi) + _nt_dot(wrl_ref[...], hi) + _nt_dot(wrh_ref[...], lo)
    lg_ref[...] = lg + br_ref[...]
    bits = pltpu.bitcast(hi.astype(F32), U32)
    half = bits.shape[1] // 2
    u2p_ref[...] = (bits[:, :half] >> 16) | (bits[:, half:] & jnp.uint32(0xFFFF0000))


def _merge(ya, yb, proj, x2, mod3, n2g, woa, wob, wo, wr_hi, wr_lo, br, seq, ga_col, gb_col, tm=256):
    t, d = x2.shape
    per_b = seq // tm
    wa = ya.shape[1]
    wb = yb.shape[1]
    ne = wr_hi.shape[0]

    def mod_spec(idx):
        return pl.BlockSpec((None, 1, d), lambda i: ((i // per_b) * N_MOD + idx, 0, 0))

    def const_spec(shape):
        return pl.BlockSpec(shape, lambda i: (0,) * len(shape), pipeline_mode=pl.Buffered(1))

    return pl.pallas_call(
        _merge_kernel,
        grid=(t // tm,),
        in_specs=[pl.BlockSpec((tm, wa), lambda i: (i, 0)),
                  pl.BlockSpec((tm, wb), lambda i: (i, 0)),
                  pl.BlockSpec((tm, d), lambda i: (i, ga_col)),
                  pl.BlockSpec((tm, d), lambda i: (i, gb_col)),
                  pl.BlockSpec((tm, d), lambda i: (i, 0)),
                  mod_spec(2), mod_spec(4), mod_spec(3),
                  const_spec((1, d)),
                  const_spec((wa, d)), const_spec((wb, d)), const_spec((d, d)),
                  const_spec((ne, d)), const_spec((ne, d)), const_spec((ne, 1))],
        out_specs=[pl.BlockSpec((tm, d), lambda i: (i, 0)),
                   pl.BlockSpec((tm, d // 2), lambda i: (i, 0)),
                   pl.BlockSpec((ne, tm), lambda i: (0, i))],
        out_shape=[jax.ShapeDtypeStruct((t, d), F32),
                   jax.ShapeDtypeStruct((t, d // 2), U32),
                   jax.ShapeDtypeStruct((ne, t), F32)],
        compiler_params=_params(("arbitrary",)),
        name="merge",
    )(ya, yb, proj, proj, x2, mod3, mod3, mod3, n2g, woa, wob, wo, wr_hi, wr_lo, br)


def _route_kernel(lg_ref, p_ref, dest_ref, cnt_ref, carry_ref, total_ref):
    sweep = pl.program_id(0)

    @pl.when(pl.program_id(1) == 0)
    def _():
        @pl.when(sweep == 0)
        def _():
            total_ref[...] = jnp.zeros_like(total_ref)

        @pl.when(sweep == 1)
        def _():
            total_ref[...] = carry_ref[...]

        carry_ref[...] = jnp.zeros_like(carry_ref)

    lg = lg_ref[...]
    ne, tr = lg.shape
    eio = lax.broadcasted_iota(I32, (ne, tr), 0)
    work = lg
    vals, hots = [], []
    for k in range(TOP_K):
        mx = jnp.max(work, axis=0, keepdims=True)
        am = jnp.min(jnp.where(work == mx, eio, ne), axis=0, keepdims=True)
        hot = eio == am
        vals.append(mx)
        hots.append(hot)
        work = jnp.where(hot, -jnp.inf, work)
    exps = [jnp.exp(v - vals[0]) for v in vals]
    denom = functools.reduce(jnp.add, exps)
    for k in range(TOP_K):
        p_ref[k:k + 1, :] = exps[k] / denom
    chosen = functools.reduce(jnp.logical_or, hots)
    sel = jnp.where(chosen, 1.0, 0.0)
    tri = (lax.broadcasted_iota(I32, (tr, tr), 0) < lax.broadcasted_iota(I32, (tr, tr), 1))
    before = jnp.dot(sel.astype(BF16), jnp.where(tri, 1.0, 0.0).astype(BF16),
                     preferred_element_type=F32)
    carry = carry_ref[...]
    nblk = jnp.floor((total_ref[...][:, 0:1] + (ROW_BLOCK - 1)) * (1.0 / ROW_BLOCK))
    e_row = lax.broadcasted_iota(I32, (ne, ne), 0)
    e_col = lax.broadcasted_iota(I32, (ne, ne), 1)
    nblk_lanes = jnp.sum(jnp.where(e_row == e_col, nblk, 0.0), axis=0, keepdims=True)
    first_row = jnp.sum(jnp.where(e_col < e_row, nblk_lanes, 0.0), axis=1,
                        keepdims=True) * ROW_BLOCK
    place = before + (carry[:, 0:1] + first_row)
    for k in range(TOP_K):
        dest_ref[k:k + 1, :] = jnp.sum(jnp.where(hots[k], place, 0.0), axis=0,
                                       keepdims=True).astype(I32)
    carry = carry + jnp.sum(sel, axis=1, keepdims=True)
    carry_ref[...] = carry
    cnt_ref[...] = carry.astype(I32)


def _route(logits_t, tr=512):
    ne, t = logits_t.shape
    slot_spec = pl.BlockSpec((None, TOP_K, tr), lambda s, i: (s, 0, i))
    probs, dest, cnt = pl.pallas_call(
        _route_kernel,
        grid=(2, t // tr),
        in_specs=[pl.BlockSpec((ne, tr), lambda s, i: (0, i))],
        out_specs=[slot_spec, slot_spec,
                   pl.BlockSpec((None, ne, LANES), lambda s, i: (s, 0, 0))],
        out_shape=[jax.ShapeDtypeStruct((2, TOP_K, t), F32),
                   jax.ShapeDtypeStruct((2, TOP_K, t), I32),
                   jax.ShapeDtypeStruct((2, ne, LANES), I32)],
        scratch_shapes=[pltpu.VMEM((ne, LANES), F32), pltpu.VMEM((ne, LANES), F32)],
        compiler_params=_params(("arbitrary", "arbitrary")),
        name="route",
    )(logits_t)
    return probs[1], dest[1], cnt[1]


def _dispatch_kernel(dest_ref, fill_ref, nfill_ref, src_ref, dst_ref, sem, fill_sem, *, tm, t_total):
    i = pl.program_id(0)

    @pl.when(i == 0)
    def _():
        def fill(b):
            rows = pl.ds(pl.multiple_of(fill_ref[b] * ROW_BLOCK, ROW_BLOCK), ROW_BLOCK)
            return pltpu.make_async_copy(src_ref.at[pl.ds(0, ROW_BLOCK)], dst_ref.at[rows], fill_sem)

        def fill_start(b, c):
            fill(b).start()
            return c

        def fill_wait(b, c):
            fill(b).wait()
            return c

        lax.fori_loop(0, nfill_ref[0], fill_start, 0)
        lax.fori_loop(0, nfill_ref[0], fill_wait, 0)

    for k in range(TOP_K):
        def group(g, c):
            j0 = pl.multiple_of(g * SUBLANES, SUBLANES)
            tile = src_ref.at[pl.ds(j0, SUBLANES)]
            for u in range(SUBLANES):
                row = dest_ref[k * t_total + i * tm + j0 + u]
                pltpu.make_async_copy(tile.at[pl.ds(u, 1)], dst_ref.at[pl.ds(row, 1)], sem).start()
            return c
        lax.fori_loop(0, tm // SUBLANES, group, 0)
    for k in range(TOP_K):
        pltpu.make_async_copy(src_ref, dst_ref.at[pl.ds(0, tm)], sem).wait()


def _dispatch(dest_flat, fill_blocks, nfill, u2p, total_rows, tm=512):
    t, w = u2p.shape
    kern = functools.partial(_dispatch_kernel, tm=tm, t_total=t)
    return pl.pallas_call(
        kern,
        grid_spec=pltpu.PrefetchScalarGridSpec(
            num_scalar_prefetch=3,
            grid=(t // tm,),
            in_specs=[pl.BlockSpec((tm, w), lambda i, d, fb, nf: (i, 0))],
            out_specs=pl.BlockSpec(memory_space=pl.ANY),
            scratch_shapes=[pltpu.SemaphoreType.DMA(()), pltpu.SemaphoreType.DMA(())]),
        out_shape=jax.ShapeDtypeStruct((total_rows, w), U32),
        compiler_params=_params(("arbitrary",)),
        name="dispatch",
    )(dest_flat, fill_blocks, nfill, u2p)


def _moe_kernel(che_ref, chblk_ref, chn_ref, nvalid_ref,
                xs_ref, wg_ref, wu_ref, wd_ref, bg_ref, bu_ref, bd_ref,
                ys_ref,
                xin_ref, xbf_ref, yacc_ref, act_ref, wgb_ref, wub_ref, wdb_ref, in_sem, out_sem,
                *, nff, nsplit, mm_rows):
    c = pl.program_id(0)
    f = pl.program_id(1)
    nvalid = nvalid_ref[0]
    half = xin_ref.shape[1]

    def in_copy(cc, j):
        row = pl.multiple_of((chblk_ref[cc] + j) * ROW_BLOCK, ROW_BLOCK)
        return pltpu.make_async_copy(xs_ref.at[pl.ds(row, ROW_BLOCK)],
                                     xin_ref.at[pl.ds(j * ROW_BLOCK, ROW_BLOCK)], in_sem)

    def out_copy(cc, j):
        row = pl.multiple_of((chblk_ref[cc] + j) * ROW_BLOCK, ROW_BLOCK)
        return pltpu.make_async_copy(yacc_ref.at[pl.ds(j * ROW_BLOCK, ROW_BLOCK)],
                                     ys_ref.at[pl.ds(row, ROW_BLOCK)], out_sem)

    def for_blocks(cc, fn):
        def one(j, carry):
            fn(cc, j)
            return carry
        lax.fori_loop(0, chn_ref[cc], one, 0)

    def chunk_step():
        @pl.when(f == 0)
        def _():
            @pl.when(c == 0)
            def _():
                xbf_ref[...] = jnp.zeros_like(xbf_ref)
                for_blocks(c, lambda cc, j: in_copy(cc, j).start())

            for_blocks(c, lambda cc, j: in_copy(cc, j).wait())

            def unpack(cc, j):
                rows = pl.ds(pl.multiple_of(j * ROW_BLOCK, ROW_BLOCK), ROW_BLOCK)
                w = xin_ref[rows, :]
                xbf_ref[rows, :half] = pltpu.bitcast(w << 16, F32).astype(BF16)
                xbf_ref[rows, half:] = pltpu.bitcast(w & jnp.uint32(0xFFFF0000), F32).astype(BF16)

            for_blocks(c, unpack)

            @pl.when(c + 1 < nvalid)
            def _():
                for_blocks(c + 1, lambda cc, j: in_copy(cc, j).start())

        wgb_ref[...] = wg_ref[...].astype(BF16)
        wub_ref[...] = wu_ref[...].astype(BF16)
        nrows = xbf_ref.shape[0]
        for r0 in range(0, nrows, mm_rows):
            rows = slice(r0, r0 + mm_rows)
            x = xbf_ref[rows, :]
            g = jnp.minimum(jnp.dot(x, wgb_ref[...], preferred_element_type=F32) + bg_ref[...],
                            SWIGLU_LIMIT)
            u = jnp.clip(jnp.dot(x, wub_ref[...], preferred_element_type=F32) + bu_ref[...],
                         -SWIGLU_LIMIT, SWIGLU_LIMIT)
            act_ref[rows, :] = ((u + 1.0) * (g * jax.nn.sigmoid(SWIGLU_ALPHA * g))).astype(BF16)
        wdb_ref[...] = wd_ref[...].astype(BF16)

        @pl.when((f == 0) & (c > 0))
        def _():
            for_blocks(c - 1, lambda cc, j: out_copy(cc, j).wait())

        first = f == 0
        ncol = yacc_ref.shape[1] // nsplit
        for r0 in range(0, nrows, mm_rows):
            rows = slice(r0, r0 + mm_rows)
            a = act_ref[rows, :]
            for s in range(nsplit):
                cols = slice(s * ncol, (s + 1) * ncol)
                part = jnp.dot(a, wdb_ref[:, cols], preferred_element_type=F32)
                base = jnp.where(first, jnp.broadcast_to(bd_ref[:, cols], part.shape),
                                 yacc_ref[rows, cols])
                yacc_ref[rows, cols] = base + part

        @pl.when(f == nff - 1)
        def _():
            for_blocks(c, lambda cc, j: out_copy(cc, j).start())

            @pl.when(c == nvalid - 1)
            def _():
                for_blocks(c, lambda cc, j: out_copy(cc, j).wait())

                def fill(b):
                    rows = pl.ds(pl.multiple_of(b * ROW_BLOCK, ROW_BLOCK), ROW_BLOCK)
                    return pltpu.make_async_copy(yacc_ref.at[pl.ds(0, ROW_BLOCK)], ys_ref.at[rows],
                                                 out_sem)

                def fill_start(b, carry):
                    fill(b).start()
                    return carry

                def fill_wait(b, carry):
                    fill(b).wait()
                    return carry

                nused = chblk_ref[c] + chn_ref[c]
                ntotal = ys_ref.shape[0] // ROW_BLOCK
                lax.fori_loop(nused, ntotal, fill_start, 0)
                lax.fori_loop(nused, ntotal, fill_wait, 0)

    chunk_step()


def _moe(che, chblk, chn, nvalid, xs, wg, wu, wd, bg, bu, bd):
    ne, d, dff = wg.shape
    rows = CHUNK_BLOCKS * ROW_BLOCK
    nff = dff // FF_TILE

    kern = functools.partial(_moe_kernel, nff=nff, nsplit=4, mm_rows=MM_ROW_BLOCKS * ROW_BLOCK)
    return pl.pallas_call(
        kern,
        grid_spec=pltpu.PrefetchScalarGridSpec(
            num_scalar_prefetch=4,
            grid=(nvalid[0], nff),
            in_specs=[pl.BlockSpec(memory_space=pl.ANY),
                      pl.BlockSpec((None, d, FF_TILE), lambda c, f, e, b, n, nv: (e[c], 0, f)),
                      pl.BlockSpec((None, d, FF_TILE), lambda c, f, e, b, n, nv: (e[c], 0, f)),
                      pl.BlockSpec((None, FF_TILE, d), lambda c, f, e, b, n, nv: (e[c], f, 0)),
                      pl.BlockSpec((None, 1, FF_TILE), lambda c, f, e, b, n, nv: (e[c], 0, f)),
                      pl.BlockSpec((None, 1, FF_TILE), lambda c, f, e, b, n, nv: (e[c], 0, f)),
                      pl.BlockSpec((None, 1, d), lambda c, f, e, b, n, nv: (e[c], 0, 0))],
            out_specs=pl.BlockSpec(memory_space=pl.ANY),
            scratch_shapes=[pltpu.VMEM((rows, d // 2), U32),
                            pltpu.VMEM((rows, d), BF16),
                            pltpu.VMEM((rows, d), F32),
                            pltpu.VMEM((rows, FF_TILE), BF16),
                            pltpu.VMEM((d, FF_TILE), BF16),
                            pltpu.VMEM((d, FF_TILE), BF16),
                            pltpu.VMEM((FF_TILE, d), BF16),
                            pltpu.SemaphoreType.DMA(()),
                            pltpu.SemaphoreType.DMA(())]),
        out_shape=jax.ShapeDtypeStruct((xs.shape[0], d), F32),
        compiler_params=_params(("arbitrary", "arbitrary")),
        name="moe",
    )(che, chblk, chn, nvalid, xs, wg, wu, wd, bg, bu, bd)


def _combine_kernel(dest_ref, ys_ref, h_ref, p_ref, g2_ref, fg_ref, o_ref, buf_ref, sem,
                    *, tc, nsteps, final_norm):
    i = pl.program_id(0)
    t_total = nsteps * tc

    def issue(step, slot):
        for k in range(TOP_K):
            def group(g, c):
                j0 = pl.multiple_of(g * SUBLANES, SUBLANES)
                tile = buf_ref.at[slot, k, pl.ds(j0, SUBLANES)]
                for u in range(SUBLANES):
                    src = dest_ref[k * t_total + step * tc + j0 + u]
                    pltpu.make_async_copy(ys_ref.at[pl.ds(src, 1)], tile.at[pl.ds(u, 1)],
                                          sem.at[slot]).start()
                return c
            lax.fori_loop(0, tc // SUBLANES, group, 0)

    def drain(slot):
        for k in range(TOP_K):
            pltpu.make_async_copy(ys_ref.at[pl.ds(0, tc)], buf_ref.at[slot, k], sem.at[slot]).wait()

    slot = i % 2

    @pl.when(i == 0)
    def _():
        issue(0, 0)

    @pl.when(i + 1 < nsteps)
    def _():
        issue(i + 1, 1 - slot)

    drain(slot)
    p = p_ref[...]
    moe = p[:, 0:1] * buf_ref[slot, 0]
    for k in range(1, TOP_K):
        moe += p[:, k:k + 1] * buf_ref[slot, k]
    h = h_ref[...] + g2_ref[...] * moe
    if final_norm:
        h = h * lax.rsqrt(jnp.mean(h * h, axis=-1, keepdims=True) + EPS) * fg_ref[...]
    o_ref[...] = h


def _combine(dest_flat, ys, h, probs_t, mod3, final_g, seq, final_norm, tc=256):
    t, d = h.shape
    per_b = seq // tc
    nsteps = t // tc
    kern = functools.partial(_combine_kernel, tc=tc, nsteps=nsteps, final_norm=final_norm)
    return pl.pallas_call(
        kern,
        grid_spec=pltpu.PrefetchScalarGridSpec(
            num_scalar_prefetch=1,
            grid=(nsteps,),
            in_specs=[pl.BlockSpec(memory_space=pl.ANY),
                      pl.BlockSpec((tc, d), lambda i, s: (i, 0)),
                      pl.BlockSpec((tc, TOP_K), lambda i, s: (i, 0)),
                      pl.BlockSpec((None, 1, d), lambda i, s: ((i // per_b) * N_MOD + 5, 0, 0)),
                      pl.BlockSpec((1, d), lambda i, s: (0, 0))],
            out_specs=pl.BlockSpec((tc, d), lambda i, s: (i, 0)),
            scratch_shapes=[pltpu.VMEM((2, TOP_K, tc, d), F32),
                            pltpu.SemaphoreType.DMA((2,))]),
        out_shape=jax.ShapeDtypeStruct((t, d), F32),
        compiler_params=_params(("arbitrary",)),
        name="combine",
    )(dest_flat, ys, h, probs_t, mod3, final_g)


def _routing_plan(counts, t):
    nblk_total = -(-t * TOP_K // ROW_BLOCK) + N_EXPERTS
    nblk = (counts + ROW_BLOCK - 1) // ROW_BLOCK
    blk_end = jnp.cumsum(nblk)
    blk_start = blk_end - nblk
    ntail_max = nblk_total - t * TOP_K // ROW_BLOCK
    fill_blocks = jnp.concatenate([jnp.maximum(blk_end - 1, 0),
                                   jnp.minimum(blk_end[-1] + jnp.arange(ntail_max), nblk_total - 1)])
    nfill = N_EXPERTS + nblk_total - blk_end[-1]
    max_chunks = -(-nblk_total // CHUNK_BLOCKS) + N_EXPERTS
    nch = (nblk + CHUNK_BLOCKS - 1) // CHUNK_BLOCKS
    ch_end = jnp.cumsum(nch)
    ch_start = ch_end - nch
    nvalid = ch_end[-1]
    cid = jnp.minimum(jnp.arange(max_chunks, dtype=I32), nvalid - 1)
    che =jnp.clip(jnp.searchsorted(ch_end, cid, side='right'), 0, N_EXPERTS - 1).astype(I32)
    local = cid - ch_start[che]
    chblk = (blk_start[che] + local * CHUNK_BLOCKS).astype(I32)
    chn = jnp.minimum(CHUNK_BLOCKS, nblk[che] - local * CHUNK_BLOCKS).astype(I32)
    return (fill_blocks.astype(I32), nfill.astype(I32).reshape(1), che, chblk, chn,
            nvalid.astype(I32).reshape(1), nblk_total * ROW_BLOCK)


def kernel(x, c, w_ada, b_ada, norm1_g, w_in, lam_q1, lam_k1, lam_q2, lam_k2, subln_g, w_out_a, w_out_b, w_o, norm2_g, w_router, b_router, w_gate, b_gate, w_up, b_up, w_down, b_down, final_g):
    batch, seq, d = x.shape
    t = batch * seq
    depth = w_ada.shape[0]
    wa = N_HEADS_A * HEAD_DIM
    wb = N_HEADS_B * 2 * HEAD_DIM
    slopes = _alibi_slopes(N_HEADS_A + N_HEADS_B)
    slopes_a = jnp.asarray(slopes[:N_HEADS_A])
    slopes_b = jnp.asarray(slopes[N_HEADS_A:])
    c8 = jnp.pad(c, ((0, 8 - batch), (0, 0)))

    h = x.reshape(t, d)
    for l in range(depth):
        lambda_init = 0.8 - 0.6 * math.exp(-0.3 * l)
        mod = _ada(c8, w_ada[l], b_ada[l].reshape(1, -1))[:batch]
        mod3 = mod.reshape(batch * N_MOD, 1, d)

        u = _norm_mod(h, norm1_g[l].reshape(1, d), mod3, 1, 0, seq)
        proj = _inproj(u, w_in[l], q_tiles=(0, 3 * wa // 1024))
        y_a = _dil_attention(proj, slopes_a, batch, seq, 0, wa // HEAD_DIM, 2 * wa // HEAD_DIM)
        off_b = 3 * wa // (2 * HEAD_DIM)
        nb = wb // (2 * HEAD_DIM)
        y_b = _diff_attention(proj, slopes_b,
                              [v[l].reshape(1, HEAD_DIM) for v in (lam_q1, lam_k1, lam_q2, lam_k2)],
                              subln_g[l].reshape(1, -1), lambda_init, batch, seq,
                              off_b, off_b + nb, off_b + 2 * nb)
        wr_hi, wr_lo = _split_bf16(w_router[l].T)
        gate_col = (3 * wa + 3 * wb) // d
        h, u2p, logits_t = _merge(y_a, y_b, proj, h, mod3, norm2_g[l].reshape(1, d),
                                  w_out_a[l].astype(BF16), w_out_b[l].astype(BF16),
                                  w_o[l].astype(BF16), wr_hi, wr_lo, b_router[l].reshape(-1, 1),
                                  seq, gate_col, gate_col + 1)
        probs, dest, cnt = _route(logits_t)
        (fill_blocks, nfill, che, chblk, chn, nvalid, total_rows) = _routing_plan(cnt[:, 0], t)
        xs = _dispatch(dest.reshape(-1), fill_blocks, nfill, u2p, total_rows)
        ys = _moe(che, chblk, chn, nvalid, xs, w_gate[l], w_up[l], w_down[l],
                  b_gate[l].reshape(N_EXPERTS, 1, -1), b_up[l].reshape(N_EXPERTS, 1, -1),
                  b_down[l].reshape(N_EXPERTS, 1, -1))
        h = _combine(dest.reshape(-1), ys, h, probs.T, mod3, final_g.reshape(1, d), seq,
                     final_norm=(l == depth - 1))
    return h.reshape(batch, seq, d)
```

```python
import functools
import math

import numpy as np
import jax
import jax.numpy as jnp
from jax import lax
from jax.experimental import pallas as pl
from jax.experimental.pallas import tpu as pltpu

F32 = jnp.float32
BF16 = jnp.bfloat16
U32 = jnp.uint32
I32 = jnp.int32

HEAD_DIM = 128
N_HEADS_A = 8
N_HEADS_B = 4
DILATED_PATTERNS = ((128, 1), (512, 4), (2048, 16))
N_EXPERTS = 32
TOP_K = 4
SWIGLU_LIMIT = 7.0
SWIGLU_ALPHA = 1.702
N_MOD = 6
EPS = 1e-5
NEG = -1e30
LOG2E = math.log2(math.e)

LANES = 128
SUBLANES = 8
V7X_VMEM_LIMIT = 56 * 1024 * 1024

ROW_BLOCK = 128
CHUNK_BLOCKS = 9
MM_ROW_BLOCKS = 3
FF_TILE = 512


def _alibi_slopes(n):
    return np.array([2.0 ** (-8.0 * (i + 1) / n) for i in range(n)], dtype=np.float32)


def _nt_dot(a, b):
    return lax.dot_general(a, b, (((1,), (1,)), ((), ())), preferred_element_type=F32)


def _split_bf16(x):
    hi = x.astype(BF16)
    lo = (x - hi.astype(F32)).astype(BF16)
    return hi, lo


def _params(sem, vmem=V7X_VMEM_LIMIT):
    return pltpu.CompilerParams(dimension_semantics=sem, vmem_limit_bytes=vmem)


def _ada_kernel(c_ref, w_ref, b_ref, o_ref):
    c = c_ref[...]
    a = c * jax.nn.sigmoid(c)
    a_hi, a_lo = _split_bf16(a)
    w_hi, w_lo = _split_bf16(w_ref[...])
    acc = jnp.dot(a_hi, w_hi, preferred_element_type=F32)
    acc += jnp.dot(a_lo, w_hi, preferred_element_type=F32)
    acc += jnp.dot(a_hi, w_lo, preferred_element_type=F32)
    o_ref[...] = acc + b_ref[...]


def _ada(c8, w, b, tn=1024):
    m, d = c8.shape
    n = w.shape[1]
    return pl.pallas_call(
        _ada_kernel,
        grid=(n // tn,),
        in_specs=[pl.BlockSpec((m, d), lambda j: (0, 0)),
                  pl.BlockSpec((d, tn), lambda j: (0, j)),
                  pl.BlockSpec((1, tn), lambda j: (0, j))],
        out_specs=pl.BlockSpec((m, tn), lambda j: (0, j)),
        out_shape=jax.ShapeDtypeStruct((m, n), F32),
        compiler_params=_params(("arbitrary",)),
        name="ada",
    )(c8, w, b)


def _norm_mod_kernel(x_ref, g_ref, sc_ref, sh_ref, o_ref):
    x = x_ref[...]
    y = x * lax.rsqrt(jnp.mean(x * x, axis=-1, keepdims=True) + EPS) * g_ref[...]
    o_ref[...] = (y * (1.0 + sc_ref[...]) + sh_ref[...]).astype(o_ref.dtype)


def _norm_mod(x2, g, mod3, i_scale, i_shift, seq, tm=512):
    t, d = x2.shape
    per_b = seq // tm
    return pl.pallas_call(
        _norm_mod_kernel,
        grid=(t // tm,),
        in_specs=[pl.BlockSpec((tm, d), lambda i: (i, 0)),
                  pl.BlockSpec((1, d), lambda i: (0, 0)),
                  pl.BlockSpec((None, 1, d), lambda i: ((i // per_b) * N_MOD + i_scale, 0, 0)),
                  pl.BlockSpec((None, 1, d), lambda i: ((i // per_b) * N_MOD + i_shift, 0, 0))],
        out_specs=pl.BlockSpec((tm, d), lambda i: (i, 0)),
        out_shape=jax.ShapeDtypeStruct((t, d), BF16),
        compiler_params=_params(("arbitrary",)),
        name="norm1",
    )(x2, g, mod3, mod3)


def _inproj_kernel(u_ref, w_ref, o_ref, wbf_ref, *, q_tiles, scale):
    n = pl.program_id(0)

    @pl.when(pl.program_id(1) == 0)
    def _():
        wbf_ref[...] = w_ref[...].astype(BF16)

    acc = jnp.dot(u_ref[...], wbf_ref[...], preferred_element_type=F32)
    is_q = functools.reduce(jnp.logical_or, [n == q for q in q_tiles])
    o_ref[...] = (acc * jnp.where(is_q, scale, 1.0)).astype(BF16)


def _inproj(u, w, q_tiles, tm=1024, tn=1024):
    t, d = u.shape
    n = w.shape[1]
    return pl.pallas_call(
        functools.partial(_inproj_kernel, q_tiles=q_tiles, scale=HEAD_DIM ** -0.5 * LOG2E),
        grid=(n // tn, t // tm),
        in_specs=[pl.BlockSpec((tm, d), lambda j, i: (i, 0)),
                  pl.BlockSpec((d, tn), lambda j, i: (0, j))],
        out_specs=pl.BlockSpec((tm, tn), lambda j, i: (i, j)),
        out_shape=jax.ShapeDtypeStruct((t, n), BF16),
        scratch_shapes=[pltpu.VMEM((d, tn), BF16)],
        compiler_params=_params(("arbitrary", "arbitrary")),
        name="inproj",
    )(u, w)


def _dil_tables(tq, span):
    nside = span // tq
    o = lax.broadcasted_iota(I32, (2 * nside + 1, tq, tq), 0) - nside
    i = lax.broadcasted_iota(I32, (2 * nside + 1, tq, tq), 1)
    j = lax.broadcasted_iota(I32, (2 * nside + 1, tq, tq), 2)
    ad = jnp.abs(o * tq + j - i)
    mult = jnp.zeros_like(ad)
    for window, dil in DILATED_PATTERNS:
        mult += ((ad % dil == 0) & (ad // dil <= window // (2 * dil))).astype(I32)
    lm = jnp.where(mult > 0, jnp.log2(jnp.maximum(mult, 1).astype(F32)), NEG)
    return ad.astype(F32) * LOG2E, lm


def _dil_kernel(slopes_ref, q_ref, k_ref, v_ref, ad_ref, lm_ref, o_ref, bias_ref, vt_ref, st_ref,
                *, tq, nside, nq):
    h = pl.program_id(1)
    i = pl.program_id(2)

    @pl.when(i == 0)
    def _():
        bias_ref[:2 * nside + 1] = lm_ref[...] - slopes_ref[h] * ad_ref[...]
        bias_ref[2 * nside + 1] = jnp.full((tq, tq), NEG, F32)
        for kb in range(nq):
            vt_ref[kb] = v_ref[kb * tq:(kb + 1) * tq, :].astype(F32).T.astype(BF16)

    nwin = 2 * nside + 1
    b0 = jnp.clip(i - nside, 0, nq - nwin)
    q = q_ref[...]
    m = None
    for j in range(nwin):
        o = b0 + j - i
        plane = jnp.where(jnp.abs(o) <= nside, nside - o, nwin)
        kb = k_ref[pl.ds(pl.multiple_of((b0 + j) * tq, tq), tq), :]
        s = _nt_dot(kb, q) + bias_ref[plane]
        st_ref[j] = s
        cm = jnp.max(s, axis=0, keepdims=True)
        m = cm if j == 0 else jnp.maximum(m, cm)
    l = acc = None
    for j in range(nwin):
        p = jnp.exp2(st_ref[j] - m)
        cl = jnp.sum(p, axis=0, keepdims=True)
        ca = jnp.dot(vt_ref[b0 + j], p.astype(BF16), preferred_element_type=F32)
        l = cl if j == 0 else l + cl
        acc = ca if j == 0 else acc + ca
    o_ref[...] = (acc / l).T.astype(o_ref.dtype)


def _dil_attention(proj, slopes, batch, seq, q_col, k_col, v_col, tq=256):
    span = max(w // 2 for w, _ in DILATED_PATTERNS)
    nside = span // tq
    nq = seq // tq
    ad, lm = _dil_tables(tq, span)
    tab_spec = pl.BlockSpec((2 * nside + 1, tq, tq), lambda b, h, i, s: (0, 0, 0))
    kern = functools.partial(_dil_kernel, tq=tq, nside=nside, nq=nq)
    return pl.pallas_call(
        kern,
        grid_spec=pltpu.PrefetchScalarGridSpec(
            num_scalar_prefetch=1,
            grid=(batch, N_HEADS_A, nq),
            in_specs=[pl.BlockSpec((tq, HEAD_DIM), lambda b, h, i, s: (b * nq + i, q_col + h)),
                      pl.BlockSpec((seq, HEAD_DIM), lambda b, h, i, s: (b, k_col + h)),
                      pl.BlockSpec((seq, HEAD_DIM), lambda b, h, i, s: (b, v_col + h)),
                      tab_spec, tab_spec],
            out_specs=pl.BlockSpec((tq, HEAD_DIM), lambda b, h, i, s: (b * nq + i, h)),
            scratch_shapes=[pltpu.VMEM((2 * nside + 2, tq, tq), F32),
                            pltpu.VMEM((nq, HEAD_DIM, tq), BF16),
                            pltpu.VMEM((2 * nside + 1, tq, tq), F32)]),
        out_shape=jax.ShapeDtypeStruct((batch * seq, N_HEADS_A * HEAD_DIM), BF16),
        compiler_params=_params(("arbitrary", "arbitrary", "arbitrary")),
        name="dil_attn",
    )(slopes, proj, proj, proj, ad, lm)


def _diff_kernel(slopes_ref, lq1_ref, lk1_ref, lq2_ref, lk2_ref, g_ref, q_ref, k_ref, v_ref, o_ref,
                 vt_ref, bias_ref, st1_ref, st2_ref, *, tq, tk, nk, lambda_init):
    h = pl.program_id(1)
    i = pl.program_id(2)

    seq = k_ref.shape[0]
    nq = seq // tq

    @pl.when(i == 0)
    def _():
        for kb in range(nk):
            cols = slice(kb * tk, (kb + 1) * tk)
            vt_ref[:, cols] = v_ref[cols, :].astype(F32).T.astype(BF16)

        neg_c = -slopes_ref[h] * LOG2E
        base = (lax.broadcasted_iota(I32, (tq, tq), 0) - lax.broadcasted_iota(I32, (tq, tq), 1)
                - (seq - tq))

        def fill(r, carry):
            rows = pl.ds(pl.multiple_of(r * tq, tq), tq)
            bias_ref[rows, :] = jnp.abs(base + r * tq).astype(F32) * neg_c
            return carry

        lax.fori_loop(0, 2 * nq - 1, fill, 0)

    q = q_ref[...]
    qs = (q[:, :HEAD_DIM], q[:, HEAD_DIM:])
    st_refs = (st1_ref, st2_ref)
    w0 = (nq - 1 - i) * tq
    mx = [None, None]
    for c in range(nk):
        rows = slice(c * tk, (c + 1) * tk)
        kc = k_ref[rows, :]
        b = bias_ref[pl.ds(pl.multiple_of(w0 + c * tk, tq), tk), :]
        for j in range(2):
            s = _nt_dot(kc[:, j * HEAD_DIM:(j + 1) * HEAD_DIM], qs[j]) + b
            st_refs[j][rows, :] = s
            cm = jnp.max(s, axis=0, keepdims=True)
            mx[j] = cm if c == 0 else jnp.maximum(mx[j], cm)
    ls = [None, None]
    accs = [None, None]
    for c in range(nk):
        rows = slice(c * tk, (c + 1) * tk)
        vt = vt_ref[:, rows]
        for j in range(2):
            p = jnp.exp2(st_refs[j][rows, :] - mx[j])
            cl = jnp.sum(p, axis=0, keepdims=True)
            ca = jnp.dot(vt, p.astype(BF16), preferred_element_type=F32)
            ls[j] = cl if c == 0 else ls[j] + cl
            accs[j] = ca if c == 0 else accs[j] + ca
    (a1, a2), (l1, l2) = accs, ls

    lam = (jnp.exp(jnp.sum(lq1_ref[...] * lk1_ref[...], axis=-1, keepdims=True))
           - jnp.exp(jnp.sum(lq2_ref[...] * lk2_ref[...], axis=-1, keepdims=True)) + lambda_init)
    ot = a1 / l1 - lam * (a2 / l2)
    yt = ot * lax.rsqrt(jnp.mean(ot * ot, axis=0, keepdims=True) + EPS)
    o_ref[...] = (yt.T * (g_ref[...] * (1.0 - lambda_init))).astype(o_ref.dtype)


def _diff_attention(proj, slopes, lam_vecs, subln_g, lambda_init, batch, seq, q_col, k_col, v_col,
                    tq=256, tk=512):
    nq = seq // tq
    w = 2 * HEAD_DIM
    vec_spec = pl.BlockSpec((1, HEAD_DIM), lambda b, h, i, s: (0, 0))
    kern = functools.partial(_diff_kernel, tq=tq, tk=tk, nk=seq // tk, lambda_init=lambda_init)
    return pl.pallas_call(
        kern,
        grid_spec=pltpu.PrefetchScalarGridSpec(
            num_scalar_prefetch=1,
            grid=(batch, N_HEADS_B, nq),
            in_specs=[vec_spec, vec_spec, vec_spec, vec_spec,
                      pl.BlockSpec((1, w), lambda b, h, i, s: (0, 0)),
                      pl.BlockSpec((tq, w), lambda b, h, i, s: (b * nq + i, q_col + h)),
                      pl.BlockSpec((seq, w), lambda b, h, i, s: (b, k_col + h)),
                      pl.BlockSpec((seq, w), lambda b, h, i, s: (b, v_col + h))],
            out_specs=pl.BlockSpec((tq, w), lambda b, h, i, s: (b * nq + i, h)),
            scratch_shapes=[pltpu.VMEM((w, seq), BF16),
                            pltpu.VMEM((2 * seq - tq, tq), F32),
                            pltpu.VMEM((seq, tq), F32),
                            pltpu.VMEM((seq, tq), F32)]),
        out_shape=jax.ShapeDtypeStruct((batch * seq, N_HEADS_B * w), BF16),
        compiler_params=_params(("arbitrary", "arbitrary", "arbitrary")),
        name="diff_attn",
    )(slopes, *lam_vecs, subln_g, proj, proj, proj)


def _merge_kernel(ya_ref, yb_ref, ga_ref, gb_ref, x_ref, g1_ref, sc2_ref, sh2_ref, n2g_ref,
                  woa_ref, wob_ref, wo_ref, wrh_ref, wrl_ref, br_ref,
                  h_ref, u2p_ref, lg_ref):
    a = jnp.dot(ya_ref[...], woa_ref[...], preferred_element_type=F32)
    b = jnp.dot(yb_ref[...], wob_ref[...], preferred_element_type=F32)
    merged = (jax.nn.sigmoid(ga_ref[...].astype(F32)) * a
              + jax.nn.sigmoid(gb_ref[...].astype(F32)) * b)
    h = x_ref[...] + g1_ref[...] * jnp.dot(merged.astype(BF16), wo_ref[...],
                                           preferred_element_type=F32)
    h_ref[...] = h
    y = h * lax.rsqrt(jnp.mean(h * h, axis=-1, keepdims=True) + EPS) * n2g_ref[...]
    u2 = y * (1.0 + sc2_ref[...]) + sh2_ref[...]
    hi, lo = _split_bf16(u2)
    lg = _nt_dot(wrh_ref[...], hi) + _nt_dot(wrl_ref[...], hi) + _nt_dot(wrh_ref[...], lo)
    lg_ref[...] = lg + br_ref[...]
    bits = pltpu.bitcast(hi.astype(F32), U32)
    half = bits.shape[1] // 2
    u2p_ref[...] = (bits[:, :half] >> 16) | (bits[:, half:] & jnp.uint32(0xFFFF0000))


def _merge(ya, yb, proj, x2, mod3, n2g, woa, wob, wo, wr_hi, wr_lo, br, seq, ga_col, gb_col, tm=256):
    t, d = x2.shape
    per_b = seq // tm
    wa = ya.shape[1]
    wb = yb.shape[1]
    ne = wr_hi.shape[0]

    def mod_spec(idx):
        return pl.BlockSpec((None, 1, d), lambda i: ((i // per_b) * N_MOD + idx, 0, 0))

    def const_spec(shape):
        return pl.BlockSpec(shape, lambda i: (0,) * len(shape), pipeline_mode=pl.Buffered(1))

    return pl.pallas_call(
        _merge_kernel,
        grid=(t // tm,),
        in_specs=[pl.BlockSpec((tm, wa), lambda i: (i, 0)),
                  pl.BlockSpec((tm, wb), lambda i: (i, 0)),
                  pl.BlockSpec((tm, d), lambda i: (i, ga_col)),
                  pl.BlockSpec((tm, d), lambda i: (i, gb_col)),
                  pl.BlockSpec((tm, d), lambda i: (i, 0)),
                  mod_spec(2), mod_spec(4), mod_spec(3),
                  const_spec((1, d)),
                  const_spec((wa, d)), const_spec((wb, d)), const_spec((d, d)),
                  const_spec((ne, d)), const_spec((ne, d)), const_spec((ne, 1))],
        out_specs=[pl.BlockSpec((tm, d), lambda i: (i, 0)),
                   pl.BlockSpec((tm, d // 2), lambda i: (i, 0)),
                   pl.BlockSpec((ne, tm), lambda i: (0, i))],
        out_shape=[jax.ShapeDtypeStruct((t, d), F32),
                   jax.ShapeDtypeStruct((t, d // 2), U32),
                   jax.ShapeDtypeStruct((ne, t), F32)],
        compiler_params=_params(("arbitrary",)),
        name="merge",
    )(ya, yb, proj, proj, x2, mod3, mod3, mod3, n2g, woa, wob, wo, wr_hi, wr_lo, br)


def _route_kernel(lg_ref, p_ref, dest_ref, cnt_ref, carry_ref, total_ref):
    sweep = pl.program_id(0)

    @pl.when(pl.program_id(1) == 0)
    def _():
        @pl.when(sweep == 0)
        def _():
            total_ref[...] = jnp.zeros_like(total_ref)

        @pl.when(sweep == 1)
        def _():
            total_ref[...] = carry_ref[...]

        carry_ref[...] = jnp.zeros_like(carry_ref)

    lg = lg_ref[...]
    ne, tr = lg.shape
    eio = lax.broadcasted_iota(I32, (ne, tr), 0)
    work = lg
    vals, hots = [], []
    for k in range(TOP_K):
        mx = jnp.max(work, axis=0, keepdims=True)
        am = jnp.min(jnp.where(work == mx, eio, ne), axis=0, keepdims=True)
        hot = eio == am
        vals.append(mx)
        hots.append(hot)
        work = jnp.where(hot, -jnp.inf, work)
    exps = [jnp.exp(v - vals[0]) for v in vals]
    denom = functools.reduce(jnp.add, exps)
    for k in range(TOP_K):
        p_ref[k:k + 1, :] = exps[k] / denom
    chosen = functools.reduce(jnp.logical_or, hots)
    sel = jnp.where(chosen, 1.0, 0.0)
    tri = (lax.broadcasted_iota(I32, (tr, tr), 0) < lax.broadcasted_iota(I32, (tr, tr), 1))
    before = jnp.dot(sel.astype(BF16), jnp.where(tri, 1.0, 0.0).astype(BF16),
                     preferred_element_type=F32)
    carry = carry_ref[...]
    nblk = jnp.floor((total_ref[...][:, 0:1] + (ROW_BLOCK - 1)) * (1.0 / ROW_BLOCK))
    e_row = lax.broadcasted_iota(I32, (ne, ne), 0)
    e_col = lax.broadcasted_iota(I32, (ne, ne), 1)
    nblk_lanes = jnp.sum(jnp.where(e_row == e_col, nblk, 0.0), axis=0, keepdims=True)
    first_row = jnp.sum(jnp.where(e_col < e_row, nblk_lanes, 0.0), axis=1,
                        keepdims=True) * ROW_BLOCK
    place = before + (carry[:, 0:1] + first_row)
    for k in range(TOP_K):
        dest_ref[k:k + 1, :] = jnp.sum(jnp.where(hots[k], place, 0.0), axis=0,
                                       keepdims=True).astype(I32)
    carry = carry + jnp.sum(sel, axis=1, keepdims=True)
    carry_ref[...] = carry
    cnt_ref[...] = carry.astype(I32)


def _route(logits_t, tr=512):
    ne, t = logits_t.shape
    slot_spec = pl.BlockSpec((None, TOP_K, tr), lambda s, i: (s, 0, i))
    probs, dest, cnt = pl.pallas_call(
        _route_kernel,
        grid=(2, t // tr),
        in_specs=[pl.BlockSpec((ne, tr), lambda s, i: (0, i))],
        out_specs=[slot_spec, slot_spec,
                   pl.BlockSpec((None, ne, LANES), lambda s, i: (s, 0, 0))],
        out_shape=[jax.ShapeDtypeStruct((2, TOP_K, t), F32),
                   jax.ShapeDtypeStruct((2, TOP_K, t), I32),
                   jax.ShapeDtypeStruct((2, ne, LANES), I32)],
        scratch_shapes=[pltpu.VMEM((ne, LANES), F32), pltpu.VMEM((ne, LANES), F32)],
        compiler_params=_params(("arbitrary", "arbitrary")),
        name="route",
    )(logits_t)
    return probs[1], dest[1], cnt[1]


def _dispatch_kernel(dest_ref, fill_ref, nfill_ref, src_ref, dst_ref, sem, fill_sem, *, tm, t_total):
    i = pl.program_id(0)

    @pl.when(i == 0)
    def _():
        def fill(b):
            rows = pl.ds(pl.multiple_of(fill_ref[b] * ROW_BLOCK, ROW_BLOCK), ROW_BLOCK)
            return pltpu.make_async_copy(src_ref.at[pl.ds(0, ROW_BLOCK)], dst_ref.at[rows], fill_sem)

        def fill_start(b, c):
            fill(b).start()
            return c

        def fill_wait(b, c):
            fill(b).wait()
            return c

        lax.fori_loop(0, nfill_ref[0], fill_start, 0)
        lax.fori_loop(0, nfill_ref[0], fill_wait, 0)

    for k in range(TOP_K):
        def group(g, c):
            j0 = pl.multiple_of(g * SUBLANES, SUBLANES)
            tile = src_ref.at[pl.ds(j0, SUBLANES)]
            for u in range(SUBLANES):
                row = dest_ref[k * t_total + i * tm + j0 + u]
                pltpu.make_async_copy(tile.at[pl.ds(u, 1)], dst_ref.at[pl.ds(row, 1)], sem).start()
            return c
        lax.fori_loop(0, tm // SUBLANES, group, 0)
    for k in range(TOP_K):
        pltpu.make_async_copy(src_ref, dst_ref.at[pl.ds(0, tm)], sem).wait()


def _dispatch(dest_flat, fill_blocks, nfill, u2p, total_rows, tm=512):
    t, w = u2p.shape
    kern = functools.partial(_dispatch_kernel, tm=tm, t_total=t)
    return pl.pallas_call(
        kern,
        grid_spec=pltpu.PrefetchScalarGridSpec(
            num_scalar_prefetch=3,
            grid=(t // tm,),
            in_specs=[pl.BlockSpec((tm, w), lambda i, d, fb, nf: (i, 0))],
            out_specs=pl.BlockSpec(memory_space=pl.ANY),
            scratch_shapes=[pltpu.SemaphoreType.DMA(()), pltpu.SemaphoreType.DMA(())]),
        out_shape=jax.ShapeDtypeStruct((total_rows, w), U32),
        compiler_params=_params(("arbitrary",)),
        name="dispatch",
    )(dest_flat, fill_blocks, nfill, u2p)


def _moe_kernel(che_ref, chblk_ref, chn_ref, nvalid_ref,
                xs_ref, wg_ref, wu_ref, wd_ref, bg_ref, bu_ref, bd_ref,
                ys_ref,
                xin_ref, xbf_ref, yacc_ref, act_ref, wgb_ref, wub_ref, wdb_ref, in_sem, out_sem,
                *, nff, nsplit, mm_rows):
    c = pl.program_id(0)
    f = pl.program_id(1)
    nvalid = nvalid_ref[0]
    half = xin_ref.shape[1]

    def in_copy(cc, j):
        row = pl.multiple_of((chblk_ref[cc] + j) * ROW_BLOCK, ROW_BLOCK)
        return pltpu.make_async_copy(xs_ref.at[pl.ds(row, ROW_BLOCK)],
                                     xin_ref.at[pl.ds(j * ROW_BLOCK, ROW_BLOCK)], in_sem)

    def out_copy(cc, j):
        row = pl.multiple_of((chblk_ref[cc] + j) * ROW_BLOCK, ROW_BLOCK)
        return pltpu.make_async_copy(yacc_ref.at[pl.ds(j * ROW_BLOCK, ROW_BLOCK)],
                                     ys_ref.at[pl.ds(row, ROW_BLOCK)], out_sem)

    def for_blocks(cc, fn):
        def one(j, carry):
            fn(cc, j)
            return carry
        lax.fori_loop(0, chn_ref[cc], one, 0)

    def chunk_step():
        @pl.when(f == 0)
        def _():
            @pl.when(c == 0)
            def _():
                xbf_ref[...] = jnp.zeros_like(xbf_ref)
                for_blocks(c, lambda cc, j: in_copy(cc, j).start())

            for_blocks(c, lambda cc, j: in_copy(cc, j).wait())

            def unpack(cc, j):
                rows = pl.ds(pl.multiple_of(j * ROW_BLOCK, ROW_BLOCK), ROW_BLOCK)
                w = xin_ref[rows, :]
                xbf_ref[rows, :half] = pltpu.bitcast(w << 16, F32).astype(BF16)
                xbf_ref[rows, half:] = pltpu.bitcast(w & jnp.uint32(0xFFFF0000), F32).astype(BF16)

            for_blocks(c, unpack)

            @pl.when(c + 1 < nvalid)
            def _():
                for_blocks(c + 1, lambda cc, j: in_copy(cc, j).start())

        wgb_ref[...] = wg_ref[...].astype(BF16)
        wub_ref[...] = wu_ref[...].astype(BF16)
        nrows = xbf_ref.shape[0]
        for r0 in range(0, nrows, mm_rows):
            rows = slice(r0, r0 + mm_rows)
            x = xbf_ref[rows, :]
            g = jnp.minimum(jnp.dot(x, wgb_ref[...], preferred_element_type=F32) + bg_ref[...],
                            SWIGLU_LIMIT)
            u = jnp.clip(jnp.dot(x, wub_ref[...], preferred_element_type=F32) + bu_ref[...],
                         -SWIGLU_LIMIT, SWIGLU_LIMIT)
            act_ref[rows, :] = ((u + 1.0) * (g * jax.nn.sigmoid(SWIGLU_ALPHA * g))).astype(BF16)
        wdb_ref[...] = wd_ref[...].astype(BF16)

        @pl.when((f == 0) & (c > 0))
        def _():
            for_blocks(c - 1, lambda cc, j: out_copy(cc, j).wait())

        first = f == 0
        ncol = yacc_ref.shape[1] // nsplit
        for r0 in range(0, nrows, mm_rows):
            rows = slice(r0, r0 + mm_rows)
            a = act_ref[rows, :]
            for s in range(nsplit):
                cols = slice(s * ncol, (s + 1) * ncol)
                part = jnp.dot(a, wdb_ref[:, cols], preferred_element_type=F32)
                base = jnp.where(first, jnp.broadcast_to(bd_ref[:, cols], part.shape),
                                 yacc_ref[rows, cols])
                yacc_ref[rows, cols] = base + part

        @pl.when(f == nff - 1)
        def _():
            for_blocks(c, lambda cc, j: out_copy(cc, j).start())

            @pl.when(c == nvalid - 1)
            def _():
                for_blocks(c, lambda cc, j: out_copy(cc, j).wait())

                def fill(b):
                    rows = pl.ds(pl.multiple_of(b * ROW_BLOCK, ROW_BLOCK), ROW_BLOCK)
                    return pltpu.make_async_copy(yacc_ref.at[pl.ds(0, ROW_BLOCK)], ys_ref.at[rows],
                                                 out_sem)

                def fill_start(b, carry):
                    fill(b).start()
                    return carry

                def fill_wait(b, carry):
                    fill(b).wait()
                    return carry

                nused = chblk_ref[c] + chn_ref[c]
                ntotal = ys_ref.shape[0] // ROW_BLOCK
                lax.fori_loop(nused, ntotal, fill_start, 0)
                lax.fori_loop(nused, ntotal, fill_wait, 0)

    pl.when(c < nvalid)(chunk_step)


def _moe(che, chblk, chn, nvalid, xs, wg, wu, wd, bg, bu, bd, chunk_steps):
    ne, d, dff = wg.shape
    rows = CHUNK_BLOCKS * ROW_BLOCK
    nff = dff // FF_TILE

    kern = functools.partial(_moe_kernel, nff=nff, nsplit=4, mm_rows=MM_ROW_BLOCKS * ROW_BLOCK)
    return pl.pallas_call(
        kern,
        grid_spec=pltpu.PrefetchScalarGridSpec(
            num_scalar_prefetch=4,
            grid=(chunk_steps, nff),
            in_specs=[pl.BlockSpec(memory_space=pl.ANY),
                      pl.BlockSpec((None, d, FF_TILE), lambda c, f, e, b, n, nv: (e[c], 0, f)),
                      pl.BlockSpec((None, d, FF_TILE), lambda c, f, e, b, n, nv: (e[c], 0, f)),
                      pl.BlockSpec((None, FF_TILE, d), lambda c, f, e, b, n, nv: (e[c], f, 0)),
                      pl.BlockSpec((None, 1, FF_TILE), lambda c, f, e, b, n, nv: (e[c], 0, f)),
                      pl.BlockSpec((None, 1, FF_TILE), lambda c, f, e, b, n, nv: (e[c], 0, f)),
                      pl.BlockSpec((None, 1, d), lambda c, f, e, b, n, nv: (e[c], 0, 0))],
            out_specs=pl.BlockSpec(memory_space=pl.ANY),
            scratch_shapes=[pltpu.VMEM((rows, d // 2), U32),
                            pltpu.VMEM((rows, d), BF16),
                            pltpu.VMEM((rows, d), F32),
                            pltpu.VMEM((rows, FF_TILE), BF16),
                            pltpu.VMEM((d, FF_TILE), BF16),
                            pltpu.VMEM((d, FF_TILE), BF16),
                            pltpu.VMEM((FF_TILE, d), BF16),
                            pltpu.SemaphoreType.DMA(()),
                            pltpu.SemaphoreType.DMA(())]),
        out_shape=jax.ShapeDtypeStruct((xs.shape[0], d), F32),
        compiler_params=_params(("arbitrary", "arbitrary")),
        name="moe",
    )(che, chblk, chn, nvalid, xs, wg, wu, wd, bg, bu, bd)


def _combine_kernel(dest_ref, ys_ref, h_ref, p_ref, g2_ref, fg_ref, o_ref, buf_ref, sem,
                    *, tc, nsteps, final_norm):
    i = pl.program_id(0)
    t_total = nsteps * tc

    def issue(step, slot):
        for k in range(TOP_K):
            def group(g, c):
                j0 = pl.multiple_of(g * SUBLANES, SUBLANES)
                tile = buf_ref.at[slot, k, pl.ds(j0, SUBLANES)]
                for u in range(SUBLANES):
                    src = dest_ref[k * t_total + step * tc + j0 + u]
                    pltpu.make_async_copy(ys_ref.at[pl.ds(src, 1)], tile.at[pl.ds(u, 1)],
                                          sem.at[slot]).start()
                return c
            lax.fori_loop(0, tc // SUBLANES, group, 0)

    def drain(slot):
        for k in range(TOP_K):
            pltpu.make_async_copy(ys_ref.at[pl.ds(0, tc)], buf_ref.at[slot, k], sem.at[slot]).wait()

    slot = i % 2

    @pl.when(i == 0)
    def _():
        issue(0, 0)

    @pl.when(i + 1 < nsteps)
    def _():
        issue(i + 1, 1 - slot)

    drain(slot)
    p = p_ref[...]
    moe = p[:, 0:1] * buf_ref[slot, 0]
    for k in range(1, TOP_K):
        moe += p[:, k:k + 1] * buf_ref[slot, k]
    h = h_ref[...] + g2_ref[...] * moe
    if final_norm:
        h = h * lax.rsqrt(jnp.mean(h * h, axis=-1, keepdims=True) + EPS) * fg_ref[...]
    o_ref[...] = h


def _combine(dest_flat, ys, h, probs_t, mod3, final_g, seq, final_norm, tc=256):
    t, d = h.shape
    per_b = seq // tc
    nsteps = t // tc
    kern = functools.partial(_combine_kernel, tc=tc, nsteps=nsteps, final_norm=final_norm)
    return pl.pallas_call(
        kern,
        grid_spec=pltpu.PrefetchScalarGridSpec(
            num_scalar_prefetch=1,
            grid=(nsteps,),
            in_specs=[pl.BlockSpec(memory_space=pl.ANY),
                      pl.BlockSpec((tc, d), lambda i, s: (i, 0)),
                      pl.BlockSpec((tc, TOP_K), lambda i, s: (i, 0)),
                      pl.BlockSpec((None, 1, d), lambda i, s: ((i // per_b) * N_MOD + 5, 0, 0)),
                      pl.BlockSpec((1, d), lambda i, s: (0, 0))],
            out_specs=pl.BlockSpec((tc, d), lambda i, s: (i, 0)),
            scratch_shapes=[pltpu.VMEM((2, TOP_K, tc, d), F32),
                            pltpu.SemaphoreType.DMA((2,))]),
        out_shape=jax.ShapeDtypeStruct((t, d), F32),
        compiler_params=_params(("arbitrary",)),
        name="combine",
    )(dest_flat, ys, h, probs_t, mod3, final_g)


def _routing_plan(counts, t):
    nblk_total = -(-t * TOP_K // ROW_BLOCK) + N_EXPERTS
    nblk = (counts + ROW_BLOCK - 1) // ROW_BLOCK
    blk_end = jnp.cumsum(nblk)
    blk_start = blk_end - nblk
    ntail_max = nblk_total - t * TOP_K // ROW_BLOCK
    fill_blocks = jnp.concatenate([jnp.maximum(blk_end - 1, 0),
                                   jnp.minimum(blk_end[-1] + jnp.arange(ntail_max), nblk_total - 1)])
    nfill = N_EXPERTS + nblk_total - blk_end[-1]
    max_chunks = -(-nblk_total // CHUNK_BLOCKS) + N_EXPERTS
    nch = (nblk + CHUNK_BLOCKS - 1) // CHUNK_BLOCKS
    ch_end = jnp.cumsum(nch)
    ch_start = ch_end - nch
    nvalid = ch_end[-1]
    cid = jnp.minimum(jnp.arange(max_chunks, dtype=I32), nvalid - 1)
    che =jnp.clip(jnp.searchsorted(ch_end, cid, side='right'), 0, N_EXPERTS - 1).astype(I32)
    local = cid - ch_start[che]
    chblk = (blk_start[che] + local * CHUNK_BLOCKS).astype(I32)
    chn = jnp.minimum(CHUNK_BLOCKS, nblk[che] - local * CHUNK_BLOCKS).astype(I32)
    return (fill_blocks.astype(I32), nfill.astype(I32).reshape(1), che, chblk, chn,
            nvalid.astype(I32).reshape(1), nblk_total * ROW_BLOCK)


def kernel(x, c, w_ada, b_ada, norm1_g, w_in, lam_q1, lam_k1, lam_q2, lam_k2, subln_g, w_out_a, w_out_b, w_o, norm2_g, w_router, b_router, w_gate, b_gate, w_up, b_up, w_down, b_down, final_g):
    batch, seq, d = x.shape
    t = batch * seq
    depth = w_ada.shape[0]
    wa = N_HEADS_A * HEAD_DIM
    wb = N_HEADS_B * 2 * HEAD_DIM
    slopes = _alibi_slopes(N_HEADS_A + N_HEADS_B)
    slopes_a = jnp.asarray(slopes[:N_HEADS_A])
    slopes_b = jnp.asarray(slopes[N_HEADS_A:])
    c8 = jnp.pad(c, ((0, 8 - batch), (0, 0)))

    h = x.reshape(t, d)
    for l in range(depth):
        lambda_init = 0.8 - 0.6 * math.exp(-0.3 * l)
        mod = _ada(c8, w_ada[l], b_ada[l].reshape(1, -1))[:batch]
        mod3 = mod.reshape(batch * N_MOD, 1, d)

        u = _norm_mod(h, norm1_g[l].reshape(1, d), mod3, 1, 0, seq)
        proj = _inproj(u, w_in[l], q_tiles=(0, 3 * wa // 1024))
        y_a = _dil_attention(proj, slopes_a, batch, seq, 0, wa // HEAD_DIM, 2 * wa // HEAD_DIM)
        off_b = 3 * wa // (2 * HEAD_DIM)
        nb = wb // (2 * HEAD_DIM)
        y_b = _diff_attention(proj, slopes_b,
                              [v[l].reshape(1, HEAD_DIM) for v in (lam_q1, lam_k1, lam_q2, lam_k2)],
                              subln_g[l].reshape(1, -1), lambda_init, batch, seq,
                              off_b, off_b + nb, off_b + 2 * nb)
        wr_hi, wr_lo = _split_bf16(w_router[l].T)
        gate_col = (3 * wa + 3 * wb) // d
        h, u2p, logits_t = _merge(y_a, y_b, proj, h, mod3, norm2_g[l].reshape(1, d),
                                  w_out_a[l].astype(BF16), w_out_b[l].astype(BF16),
                                  w_o[l].astype(BF16), wr_hi, wr_lo, b_router[l].reshape(-1, 1),
                                  seq, gate_col, gate_col + 1)
        probs, dest, cnt = _route(logits_t)
        (fill_blocks, nfill, che, chblk, chn, nvalid, total_rows) = _routing_plan(cnt[:, 0], t)
        xs = _dispatch(dest.reshape(-1), fill_blocks, nfill, u2p, total_rows)
        moe_args = (che, chblk, chn, nvalid, xs, w_gate[l], w_up[l], w_down[l],
                    b_gate[l].reshape(N_EXPERTS, 1, -1), b_up[l].reshape(N_EXPERTS, 1, -1),
                    b_down[l].reshape(N_EXPERTS, 1, -1))
        ys = lax.cond(nvalid[0] <= N_EXPERTS,
                      lambda args: _moe(*args, chunk_steps=N_EXPERTS),
                      lambda args: _moe(*args, chunk_steps=args[3][0]),
                      moe_args)
        h = _combine(dest.reshape(-1), ys, h, probs.T, mod3, final_g.reshape(1, d), seq,
                     final_norm=(l == depth - 1))
    return h.reshape(batch, seq, d)
```

```python
import functools
import math

import numpy as np
import jax
import jax.numpy as jnp
from jax import lax
from jax.experimental import pallas as pl
from jax.experimental.pallas import tpu as pltpu

F32 = jnp.float32
BF16 = jnp.bfloat16
U32 = jnp.uint32
I32 = jnp.int32

HEAD_DIM = 128
N_HEADS_A = 8
N_HEADS_B = 4
DILATED_PATTERNS = ((128, 1), (512, 4), (2048, 16))
N_EXPERTS = 32
TOP_K = 4
SWIGLU_LIMIT = 7.0
SWIGLU_ALPHA = 1.702
N_MOD = 6
EPS = 1e-5
NEG = -1e30
LOG2E = math.log2(math.e)

LANES = 128
SUBLANES = 8
V7X_VMEM_LIMIT = 56 * 1024 * 1024

ROW_BLOCK = 128
CHUNK_BLOCKS = 10
MM_ROW_BLOCKS = 2
FF_TILE = 512


def _alibi_slopes(n):
    return np.array([2.0 ** (-8.0 * (i + 1) / n) for i in range(n)], dtype=np.float32)


def _nt_dot(a, b):
    return lax.dot_general(a, b, (((1,), (1,)), ((), ())), preferred_element_type=F32)


def _split_bf16(x):
    hi = x.astype(BF16)
    lo = (x - hi.astype(F32)).astype(BF16)
    return hi, lo


def _params(sem, vmem=V7X_VMEM_LIMIT):
    return pltpu.CompilerParams(dimension_semantics=sem, vmem_limit_bytes=vmem)


def _ada_kernel(c_ref, w_ref, b_ref, o_ref):
    c = c_ref[...]
    a = c * jax.nn.sigmoid(c)
    a_hi, a_lo = _split_bf16(a)
    w_hi, w_lo = _split_bf16(w_ref[...])
    acc = jnp.dot(a_hi, w_hi, preferred_element_type=F32)
    acc += jnp.dot(a_lo, w_hi, preferred_element_type=F32)
    acc += jnp.dot(a_hi, w_lo, preferred_element_type=F32)
    o_ref[...] = acc + b_ref[...]


def _ada(c8, w, b, tn=1024):
    m, d = c8.shape
    n = w.shape[1]
    return pl.pallas_call(
        _ada_kernel,
        grid=(n // tn,),
        in_specs=[pl.BlockSpec((m, d), lambda j: (0, 0)),
                  pl.BlockSpec((d, tn), lambda j: (0, j)),
                  pl.BlockSpec((1, tn), lambda j: (0, j))],
        out_specs=pl.BlockSpec((m, tn), lambda j: (0, j)),
        out_shape=jax.ShapeDtypeStruct((m, n), F32),
        compiler_params=_params(("arbitrary",)),
        name="ada",
    )(c8, w, b)


def _norm_mod_kernel(x_ref, g_ref, sc_ref, sh_ref, o_ref):
    x = x_ref[...]
    y = x * lax.rsqrt(jnp.mean(x * x, axis=-1, keepdims=True) + EPS) * g_ref[...]
    o_ref[...] = (y * (1.0 + sc_ref[...]) + sh_ref[...]).astype(o_ref.dtype)


def _norm_mod(x2, g, mod3, i_scale, i_shift, seq, tm=512):
    t, d = x2.shape
    per_b = seq // tm
    return pl.pallas_call(
        _norm_mod_kernel,
        grid=(t // tm,),
        in_specs=[pl.BlockSpec((tm, d), lambda i: (i, 0)),
                  pl.BlockSpec((1, d), lambda i: (0, 0)),
                  pl.BlockSpec((None, 1, d), lambda i: ((i // per_b) * N_MOD + i_scale, 0, 0)),
                  pl.BlockSpec((None, 1, d), lambda i: ((i // per_b) * N_MOD + i_shift, 0, 0))],
        out_specs=pl.BlockSpec((tm, d), lambda i: (i, 0)),
        out_shape=jax.ShapeDtypeStruct((t, d), BF16),
        compiler_params=_params(("arbitrary",)),
        name="norm1",
    )(x2, g, mod3, mod3)


def _inproj_kernel(u_ref, w_ref, o_ref, wbf_ref, *, q_tiles, scale):
    n = pl.program_id(0)

    @pl.when(pl.program_id(1) == 0)
    def _():
        wbf_ref[...] = w_ref[...].astype(BF16)

    acc = jnp.dot(u_ref[...], wbf_ref[...], preferred_element_type=F32)
    is_q = functools.reduce(jnp.logical_or, [n == q for q in q_tiles])
    o_ref[...] = (acc * jnp.where(is_q, scale, 1.0)).astype(BF16)


def _inproj(u, w, q_tiles, tm=1024, tn=1024):
    t, d = u.shape
    n = w.shape[1]
    return pl.pallas_call(
        functools.partial(_inproj_kernel, q_tiles=q_tiles, scale=HEAD_DIM ** -0.5 * LOG2E),
        grid=(n // tn, t // tm),
        in_specs=[pl.BlockSpec((tm, d), lambda j, i: (i, 0)),
                  pl.BlockSpec((d, tn), lambda j, i: (0, j))],
        out_specs=pl.BlockSpec((tm, tn), lambda j, i: (i, j)),
        out_shape=jax.ShapeDtypeStruct((t, n), BF16),
        scratch_shapes=[pltpu.VMEM((d, tn), BF16)],
        compiler_params=_params(("arbitrary", "arbitrary")),
        name="inproj",
    )(u, w)


def _dil_tables(tq, span):
    nside = span // tq
    o = lax.broadcasted_iota(I32, (2 * nside + 1, tq, tq), 0) - nside
    i = lax.broadcasted_iota(I32, (2 * nside + 1, tq, tq), 1)
    j = lax.broadcasted_iota(I32, (2 * nside + 1, tq, tq), 2)
    ad = jnp.abs(o * tq + j - i)
    mult = jnp.zeros_like(ad)
    for window, dil in DILATED_PATTERNS:
        mult += ((ad % dil == 0) & (ad // dil <= window // (2 * dil))).astype(I32)
    lm = jnp.where(mult > 0, jnp.log2(jnp.maximum(mult, 1).astype(F32)), NEG)
    return ad.astype(F32) * LOG2E, lm


def _dil_kernel(slopes_ref, q_ref, k_ref, v_ref, ad_ref, lm_ref, o_ref, bias_ref, vt_ref, st_ref,
                *, tq, nside, nq):
    h = pl.program_id(1)
    i = pl.program_id(2)

    @pl.when(i == 0)
    def _():
        bias_ref[:2 * nside + 1] = lm_ref[...] - slopes_ref[h] * ad_ref[...]
        bias_ref[2 * nside + 1] = jnp.full((tq, tq), NEG, F32)
        for kb in range(nq):
            vt_ref[kb] = v_ref[kb * tq:(kb + 1) * tq, :].astype(F32).T.astype(BF16)

    nwin = 2 * nside + 1
    b0 = jnp.clip(i - nside, 0, nq - nwin)
    q = q_ref[...]
    m = None
    for j in range(nwin):
        o = b0 + j - i
        plane = jnp.where(jnp.abs(o) <= nside, nside - o, nwin)
        kb = k_ref[pl.ds(pl.multiple_of((b0 + j) * tq, tq), tq), :]
        s = _nt_dot(kb, q) + bias_ref[plane]
        st_ref[j] = s
        cm = jnp.max(s, axis=0, keepdims=True)
        m = cm if j == 0 else jnp.maximum(m, cm)
    l = acc = None
    for j in range(nwin):
        p = jnp.exp2(st_ref[j] - m)
        cl = jnp.sum(p, axis=0, keepdims=True)
        ca = jnp.dot(vt_ref[b0 + j], p.astype(BF16), preferred_element_type=F32)
        l = cl if j == 0 else l + cl
        acc = ca if j == 0 else acc + ca
    o_ref[...] = (acc / l).T.astype(o_ref.dtype)


def _dil_attention(proj, slopes, batch, seq, q_col, k_col, v_col, tq=256):
    span = max(w // 2 for w, _ in DILATED_PATTERNS)
    nside = span // tq
    nq = seq // tq
    ad, lm = _dil_tables(tq, span)
    tab_spec = pl.BlockSpec((2 * nside + 1, tq, tq), lambda b, h, i, s: (0, 0, 0))
    kern = functools.partial(_dil_kernel, tq=tq, nside=nside, nq=nq)
    return pl.pallas_call(
        kern,
        grid_spec=pltpu.PrefetchScalarGridSpec(
            num_scalar_prefetch=1,
            grid=(batch, N_HEADS_A, nq),
            in_specs=[pl.BlockSpec((tq, HEAD_DIM), lambda b, h, i, s: (b * nq + i, q_col + h)),
                      pl.BlockSpec((seq, HEAD_DIM), lambda b, h, i, s: (b, k_col + h)),
                      pl.BlockSpec((seq, HEAD_DIM), lambda b, h, i, s: (b, v_col + h)),
                      tab_spec, tab_spec],
            out_specs=pl.BlockSpec((tq, HEAD_DIM), lambda b, h, i, s: (b * nq + i, h)),
            scratch_shapes=[pltpu.VMEM((2 * nside + 2, tq, tq), F32),
                            pltpu.VMEM((nq, HEAD_DIM, tq), BF16),
                            pltpu.VMEM((2 * nside + 1, tq, tq), F32)]),
        out_shape=jax.ShapeDtypeStruct((batch * seq, N_HEADS_A * HEAD_DIM), BF16),
        compiler_params=_params(("arbitrary", "arbitrary", "arbitrary")),
        name="dil_attn",
    )(slopes, proj, proj, proj, ad, lm)


def _diff_kernel(slopes_ref, lq1_ref, lk1_ref, lq2_ref, lk2_ref, g_ref, q_ref, k_ref, v_ref, o_ref,
                 vt_ref, bias_ref, st1_ref, st2_ref, *, tq, tk, nk, lambda_init):
    h = pl.program_id(1)
    i = pl.program_id(2)

    seq = k_ref.shape[0]
    nq = seq // tq

    @pl.when(i == 0)
    def _():
        for kb in range(nk):
            cols = slice(kb * tk, (kb + 1) * tk)
            vt_ref[:, cols] = v_ref[cols, :].astype(F32).T.astype(BF16)

        neg_c = -slopes_ref[h] * LOG2E
        base = (lax.broadcasted_iota(I32, (tq, tq), 0) - lax.broadcasted_iota(I32, (tq, tq), 1)
                - (seq - tq))

        def fill(r, carry):
            rows = pl.ds(pl.multiple_of(r * tq, tq), tq)
            bias_ref[rows, :] = jnp.abs(base + r * tq).astype(F32) * neg_c
            return carry

        lax.fori_loop(0, 2 * nq - 1, fill, 0)

    q = q_ref[...]
    qs = (q[:, :HEAD_DIM], q[:, HEAD_DIM:])
    st_refs = (st1_ref, st2_ref)
    w0 = (nq - 1 - i) * tq
    mx = [None, None]
    for c in range(nk):
        rows = slice(c * tk, (c + 1) * tk)
        kc = k_ref[rows, :]
        b = bias_ref[pl.ds(pl.multiple_of(w0 + c * tk, tq), tk), :]
        for j in range(2):
            s = _nt_dot(kc[:, j * HEAD_DIM:(j + 1) * HEAD_DIM], qs[j]) + b
            st_refs[j][rows, :] = s
            cm = jnp.max(s, axis=0, keepdims=True)
            mx[j] = cm if c == 0 else jnp.maximum(mx[j], cm)
    ls = [None, None]
    accs = [None, None]
    for c in range(nk):
        rows = slice(c * tk, (c + 1) * tk)
        vt = vt_ref[:, rows]
        for j in range(2):
            p = jnp.exp2(st_refs[j][rows, :] - mx[j])
            cl = jnp.sum(p, axis=0, keepdims=True)
            ca = jnp.dot(vt, p.astype(BF16), preferred_element_type=F32)
            ls[j] = cl if c == 0 else ls[j] + cl
            accs[j] = ca if c == 0 else accs[j] + ca
    (a1, a2), (l1, l2) = accs, ls

    lam = (jnp.exp(jnp.sum(lq1_ref[...] * lk1_ref[...], axis=-1, keepdims=True))
           - jnp.exp(jnp.sum(lq2_ref[...] * lk2_ref[...], axis=-1, keepdims=True)) + lambda_init)
    ot = a1 / l1 - lam * (a2 / l2)
    yt = ot * lax.rsqrt(jnp.mean(ot * ot, axis=0, keepdims=True) + EPS)
    o_ref[...] = (yt.T * (g_ref[...] * (1.0 - lambda_init))).astype(o_ref.dtype)


def _diff_attention(proj, slopes, lam_vecs, subln_g, lambda_init, batch, seq, q_col, k_col, v_col,
                    tq=256, tk=512):
    nq = seq // tq
    w = 2 * HEAD_DIM
    vec_spec = pl.BlockSpec((1, HEAD_DIM), lambda b, h, i, s: (0, 0))
    kern = functools.partial(_diff_kernel, tq=tq, tk=tk, nk=seq // tk, lambda_init=lambda_init)
    return pl.pallas_call(
        kern,
        grid_spec=pltpu.PrefetchScalarGridSpec(
            num_scalar_prefetch=1,
            grid=(batch, N_HEADS_B, nq),
            in_specs=[vec_spec, vec_spec, vec_spec, vec_spec,
                      pl.BlockSpec((1, w), lambda b, h, i, s: (0, 0)),
                      pl.BlockSpec((tq, w), lambda b, h, i, s: (b * nq + i, q_col + h)),
                      pl.BlockSpec((seq, w), lambda b, h, i, s: (b, k_col + h)),
                      pl.BlockSpec((seq, w), lambda b, h, i, s: (b, v_col + h))],
            out_specs=pl.BlockSpec((tq, w), lambda b, h, i, s: (b * nq + i, h)),
            scratch_shapes=[pltpu.VMEM((w, seq), BF16),
                            pltpu.VMEM((2 * seq - tq, tq), F32),
                            pltpu.VMEM((seq, tq), F32),
                            pltpu.VMEM((seq, tq), F32)]),
        out_shape=jax.ShapeDtypeStruct((batch * seq, N_HEADS_B * w), BF16),
        compiler_params=_params(("arbitrary", "arbitrary", "arbitrary")),
        name="diff_attn",
    )(slopes, *lam_vecs, subln_g, proj, proj, proj)


def _merge_kernel(ya_ref, yb_ref, ga_ref, gb_ref, x_ref, g1_ref, sc2_ref, sh2_ref, n2g_ref,
                  woa_ref, wob_ref, wo_ref, wrh_ref, wrl_ref, br_ref,
                  h_ref, u2p_ref, lg_ref):
    a = jnp.dot(ya_ref[...], woa_ref[...], preferred_element_type=F32)
    b = jnp.dot(yb_ref[...], wob_ref[...], preferred_element_type=F32)
    merged = (jax.nn.sigmoid(ga_ref[...].astype(F32)) * a
              + jax.nn.sigmoid(gb_ref[...].astype(F32)) * b)
    h = x_ref[...] + g1_ref[...] * jnp.dot(merged.astype(BF16), wo_ref[...],
                                           preferred_element_type=F32)
    h_ref[...] = h
    y = h * lax.rsqrt(jnp.mean(h * h, axis=-1, keepdims=True) + EPS) * n2g_ref[...]
    u2 = y * (1.0 + sc2_ref[...]) + sh2_ref[...]
    hi, lo = _split_bf16(u2)
    lg = _nt_dot(wrh_ref[...], hi) + _nt_dot(wrl_ref[...], hi) + _nt_dot(wrh_ref[...], lo)
    lg_ref[...] = lg + br_ref[...]
    bits = pltpu.bitcast(hi.astype(F32), U32)
    half = bits.shape[1] // 2
    u2p_ref[...] = (bits[:, :half] >> 16) | (bits[:, half:] & jnp.uint32(0xFFFF0000))


def _merge(ya, yb, proj, x2, mod3, n2g, woa, wob, wo, wr_hi, wr_lo, br, seq, ga_col, gb_col, tm=256):
    t, d = x2.shape
    per_b = seq // tm
    wa = ya.shape[1]
    wb = yb.shape[1]
    ne = wr_hi.shape[0]

    def mod_spec(idx):
        return pl.BlockSpec((None, 1, d), lambda i: ((i // per_b) * N_MOD + idx, 0, 0))

    def const_spec(shape):
        return pl.BlockSpec(shape, lambda i: (0,) * len(shape), pipeline_mode=pl.Buffered(1))

    return pl.pallas_call(
        _merge_kernel,
        grid=(t // tm,),
        in_specs=[pl.BlockSpec((tm, wa), lambda i: (i, 0)),
                  pl.BlockSpec((tm, wb), lambda i: (i, 0)),
                  pl.BlockSpec((tm, d), lambda i: (i, ga_col)),
                  pl.BlockSpec((tm, d), lambda i: (i, gb_col)),
                  pl.BlockSpec((tm, d), lambda i: (i, 0)),
                  mod_spec(2), mod_spec(4), mod_spec(3),
                  const_spec((1, d)),
                  const_spec((wa, d)), const_spec((wb, d)), const_spec((d, d)),
                  const_spec((ne, d)), const_spec((ne, d)), const_spec((ne, 1))],
        out_specs=[pl.BlockSpec((tm, d), lambda i: (i, 0)),
                   pl.BlockSpec((tm, d // 2), lambda i: (i, 0)),
                   pl.BlockSpec((ne, tm), lambda i: (0, i))],
        out_shape=[jax.ShapeDtypeStruct((t, d), F32),
                   jax.ShapeDtypeStruct((t, d // 2), U32),
                   jax.ShapeDtypeStruct((ne, t), F32)],
        compiler_params=_params(("arbitrary",)),
        name="merge",
    )(ya, yb, proj, proj, x2, mod3, mod3, mod3, n2g, woa, wob, wo, wr_hi, wr_lo, br)


def _route_kernel(lg_ref, p_ref, dest_ref, cnt_ref, carry_ref, total_ref):
    sweep = pl.program_id(0)

    @pl.when(pl.program_id(1) == 0)
    def _():
        @pl.when(sweep == 0)
        def _():
            total_ref[...] = jnp.zeros_like(total_ref)

        @pl.when(sweep == 1)
        def _():
            total_ref[...] = carry_ref[...]

        carry_ref[...] = jnp.zeros_like(carry_ref)

    lg = lg_ref[...]
    ne, tr = lg.shape
    eio = lax.broadcasted_iota(I32, (ne, tr), 0)
    work = lg
    vals, hots = [], []
    for k in range(TOP_K):
        mx = jnp.max(work, axis=0, keepdims=True)
        am = jnp.min(jnp.where(work == mx, eio, ne), axis=0, keepdims=True)
        hot = eio == am
        vals.append(mx)
        hots.append(hot)
        work = jnp.where(hot, -jnp.inf, work)
    exps = [jnp.exp(v - vals[0]) for v in vals]
    denom = functools.reduce(jnp.add, exps)
    for k in range(TOP_K):
        p_ref[k:k + 1, :] = exps[k] / denom
    chosen = functools.reduce(jnp.logical_or, hots)
    sel = jnp.where(chosen, 1.0, 0.0)
    tri = (lax.broadcasted_iota(I32, (tr, tr), 0) < lax.broadcasted_iota(I32, (tr, tr), 1))
    before = jnp.dot(sel.astype(BF16), jnp.where(tri, 1.0, 0.0).astype(BF16),
                     preferred_element_type=F32)
    carry = carry_ref[...]
    nblk = jnp.floor((total_ref[...][:, 0:1] + (ROW_BLOCK - 1)) * (1.0 / ROW_BLOCK))
    e_row = lax.broadcasted_iota(I32, (ne, ne), 0)
    e_col = lax.broadcasted_iota(I32, (ne, ne), 1)
    nblk_lanes = jnp.sum(jnp.where(e_row == e_col, nblk, 0.0), axis=0, keepdims=True)
    first_row = jnp.sum(jnp.where(e_col < e_row, nblk_lanes, 0.0), axis=1,
                        keepdims=True) * ROW_BLOCK
    place = before + (carry[:, 0:1] + first_row)
    for k in range(TOP_K):
        dest_ref[k:k + 1, :] = jnp.sum(jnp.where(hots[k], place, 0.0), axis=0,
                                       keepdims=True).astype(I32)
    carry = carry + jnp.sum(sel, axis=1, keepdims=True)
    carry_ref[...] = carry
    cnt_ref[...] = carry.astype(I32)


def _route(logits_t, tr=512):
    ne, t = logits_t.shape
    slot_spec = pl.BlockSpec((None, TOP_K, tr), lambda s, i: (s, 0, i))
    probs, dest, cnt = pl.pallas_call(
        _route_kernel,
        grid=(2, t // tr),
        in_specs=[pl.BlockSpec((ne, tr), lambda s, i: (0, i))],
        out_specs=[slot_spec, slot_spec,
                   pl.BlockSpec((None, ne, LANES), lambda s, i: (s, 0, 0))],
        out_shape=[jax.ShapeDtypeStruct((2, TOP_K, t), F32),
                   jax.ShapeDtypeStruct((2, TOP_K, t), I32),
                   jax.ShapeDtypeStruct((2, ne, LANES), I32)],
        scratch_shapes=[pltpu.VMEM((ne, LANES), F32), pltpu.VMEM((ne, LANES), F32)],
        compiler_params=_params(("arbitrary", "arbitrary")),
        name="route",
    )(logits_t)
    return probs[1], dest[1], cnt[1]


def _dispatch_kernel(dest_ref, fill_ref, nfill_ref, src_ref, dst_ref, sem, fill_sem, *, tm, t_total):
    i = pl.program_id(0)

    @pl.when(i == 0)
    def _():
        def fill(b):
            rows = pl.ds(pl.multiple_of(fill_ref[b] * ROW_BLOCK, ROW_BLOCK), ROW_BLOCK)
            return pltpu.make_async_copy(src_ref.at[pl.ds(0, ROW_BLOCK)], dst_ref.at[rows], fill_sem)

        def fill_start(b, c):
            fill(b).start()
            return c

        def fill_wait(b, c):
            fill(b).wait()
            return c

        lax.fori_loop(0, nfill_ref[0], fill_start, 0)
        lax.fori_loop(0, nfill_ref[0], fill_wait, 0)

    for k in range(TOP_K):
        def group(g, c):
            j0 = pl.multiple_of(g * SUBLANES, SUBLANES)
            tile = src_ref.at[pl.ds(j0, SUBLANES)]
            for u in range(SUBLANES):
                row = dest_ref[k * t_total + i * tm + j0 + u]
                pltpu.make_async_copy(tile.at[pl.ds(u, 1)], dst_ref.at[pl.ds(row, 1)], sem).start()
            return c
        lax.fori_loop(0, tm // SUBLANES, group, 0)
    for k in range(TOP_K):
        pltpu.make_async_copy(src_ref, dst_ref.at[pl.ds(0, tm)], sem).wait()


def _dispatch(dest_flat, fill_blocks, nfill, u2p, total_rows, tm=512):
    t, w = u2p.shape
    kern = functools.partial(_dispatch_kernel, tm=tm, t_total=t)
    return pl.pallas_call(
        kern,
        grid_spec=pltpu.PrefetchScalarGridSpec(
            num_scalar_prefetch=3,
            grid=(t // tm,),
            in_specs=[pl.BlockSpec((tm, w), lambda i, d, fb, nf: (i, 0))],
            out_specs=pl.BlockSpec(memory_space=pl.ANY),
            scratch_shapes=[pltpu.SemaphoreType.DMA(()), pltpu.SemaphoreType.DMA(())]),
        out_shape=jax.ShapeDtypeStruct((total_rows, w), U32),
        compiler_params=_params(("arbitrary",)),
        name="dispatch",
    )(dest_flat, fill_blocks, nfill, u2p)


def _moe_kernel(che_ref, chblk_ref, chn_ref, nvalid_ref,
                xs_ref, wg_ref, wu_ref, wd_ref, bg_ref, bu_ref, bd_ref,
                ys_ref,
                xin_ref, xbf_ref, yacc_ref, act_ref, wgb_ref, wub_ref, wdb_ref, in_sem, out_sem,
                *, nff, nsplit, mm_rows):
    c = pl.program_id(0)
    f = pl.program_id(1)
    nvalid = nvalid_ref[0]
    half = xin_ref.shape[1]

    def in_copy(cc, j):
        row = pl.multiple_of((chblk_ref[cc] + j) * ROW_BLOCK, ROW_BLOCK)
        return pltpu.make_async_copy(xs_ref.at[pl.ds(row, ROW_BLOCK)],
                                     xin_ref.at[pl.ds(j * ROW_BLOCK, ROW_BLOCK)], in_sem)

    def out_copy(cc, j):
        row = pl.multiple_of((chblk_ref[cc] + j) * ROW_BLOCK, ROW_BLOCK)
        return pltpu.make_async_copy(yacc_ref.at[pl.ds(j * ROW_BLOCK, ROW_BLOCK)],
                                     ys_ref.at[pl.ds(row, ROW_BLOCK)], out_sem)

    def for_blocks(cc, fn):
        def one(j, carry):
            fn(cc, j)
            return carry
        lax.fori_loop(0, chn_ref[cc], one, 0)

    def chunk_step():
        @pl.when(f == 0)
        def _():
            @pl.when(c == 0)
            def _():
                xbf_ref[...] = jnp.zeros_like(xbf_ref)
                for_blocks(c, lambda cc, j: in_copy(cc, j).start())

            for_blocks(c, lambda cc, j: in_copy(cc, j).wait())

            def unpack(cc, j):
                rows = pl.ds(pl.multiple_of(j * ROW_BLOCK, ROW_BLOCK), ROW_BLOCK)
                w = xin_ref[rows, :]
                xbf_ref[rows, :half] = pltpu.bitcast(w << 16, F32).astype(BF16)
                xbf_ref[rows, half:] = pltpu.bitcast(w & jnp.uint32(0xFFFF0000), F32).astype(BF16)

            for_blocks(c, unpack)

            @pl.when(c + 1 < nvalid)
            def _():
                for_blocks(c + 1, lambda cc, j: in_copy(cc, j).start())

        mm_blocks = mm_rows // ROW_BLOCK
        n_mm = (chn_ref[c] + (mm_blocks - 1)) // mm_blocks

        def gate_up(rows, wg, wu):
            x = xbf_ref[rows, :]
            g = jnp.minimum(jnp.dot(x, wg, preferred_element_type=F32) + bg_ref[...], SWIGLU_LIMIT)
            u = jnp.clip(jnp.dot(x, wu, preferred_element_type=F32) + bu_ref[...],
                         -SWIGLU_LIMIT, SWIGLU_LIMIT)
            act_ref[rows, :] = ((u + 1.0) * (g * jax.nn.sigmoid(SWIGLU_ALPHA * g))).astype(BF16)

        wg = wg_ref[...].astype(BF16)
        wu = wu_ref[...].astype(BF16)
        wgb_ref[...] = wg
        wub_ref[...] = wu
        gate_up(slice(0, mm_rows), wg, wu)
        wdb_ref[...] = wd_ref[...].astype(BF16)

        def gate_up_body(rb, carry):
            gate_up(pl.ds(pl.multiple_of(rb * mm_rows, mm_rows), mm_rows), wgb_ref[...], wub_ref[...])
            return carry

        lax.fori_loop(1, n_mm, gate_up_body, 0)

        @pl.when((f == 0) & (c > 0))
        def _():
            for_blocks(c - 1, lambda cc, j: out_copy(cc, j).wait())

        first = f == 0
        ncol = yacc_ref.shape[1] // nsplit

        def down(rb, carry):
            rows = pl.ds(pl.multiple_of(rb * mm_rows, mm_rows), mm_rows)
            a = act_ref[rows, :]
            for s in range(nsplit):
                cols = slice(s * ncol, (s + 1) * ncol)
                part = jnp.dot(a, wdb_ref[:, cols], preferred_element_type=F32)
                base = jnp.where(first, jnp.broadcast_to(bd_ref[:, cols], part.shape),
                                 yacc_ref[rows, cols])
                yacc_ref[rows, cols] = base + part
            return carry

        lax.fori_loop(0, n_mm, down, 0)

        @pl.when(f == nff - 1)
        def _():
            for_blocks(c, lambda cc, j: out_copy(cc, j).start())

            @pl.when(c == nvalid - 1)
            def _():
                for_blocks(c, lambda cc, j: out_copy(cc, j).wait())

                def fill(b):
                    rows = pl.ds(pl.multiple_of(b * ROW_BLOCK, ROW_BLOCK), ROW_BLOCK)
                    return pltpu.make_async_copy(yacc_ref.at[pl.ds(0, ROW_BLOCK)], ys_ref.at[rows],
                                                 out_sem)

                def fill_start(b, carry):
                    fill(b).start()
                    return carry

                def fill_wait(b, carry):
                    fill(b).wait()
                    return carry

                nused = chblk_ref[c] + chn_ref[c]
                ntotal = ys_ref.shape[0] // ROW_BLOCK
                lax.fori_loop(nused, ntotal, fill_start, 0)
                lax.fori_loop(nused, ntotal, fill_wait, 0)

    chunk_step()


def _moe(che, chblk, chn, nvalid, xs, wg, wu, wd, bg, bu, bd):
    ne, d, dff = wg.shape
    rows = CHUNK_BLOCKS * ROW_BLOCK
    nff = dff // FF_TILE

    kern = functools.partial(_moe_kernel, nff=nff, nsplit=4, mm_rows=MM_ROW_BLOCKS * ROW_BLOCK)
    return pl.pallas_call(
        kern,
        grid_spec=pltpu.PrefetchScalarGridSpec(
            num_scalar_prefetch=4,
            grid=(nvalid[0], nff),
            in_specs=[pl.BlockSpec(memory_space=pl.ANY),
                      pl.BlockSpec((None, d, FF_TILE), lambda c, f, e, b, n, nv: (e[c], 0, f)),
                      pl.BlockSpec((None, d, FF_TILE), lambda c, f, e, b, n, nv: (e[c], 0, f)),
                      pl.BlockSpec((None, FF_TILE, d), lambda c, f, e, b, n, nv: (e[c], f, 0)),
                      pl.BlockSpec((None, 1, FF_TILE), lambda c, f, e, b, n, nv: (e[c], 0, f)),
                      pl.BlockSpec((None, 1, FF_TILE), lambda c, f, e, b, n, nv: (e[c], 0, f)),
                      pl.BlockSpec((None, 1, d), lambda c, f, e, b, n, nv: (e[c], 0, 0))],
            out_specs=pl.BlockSpec(memory_space=pl.ANY),
            scratch_shapes=[pltpu.VMEM((rows, d // 2), U32),
                            pltpu.VMEM((rows, d), BF16),
                            pltpu.VMEM((rows, d), F32),
                            pltpu.VMEM((rows, FF_TILE), BF16),
                            pltpu.VMEM((d, FF_TILE), BF16),
                            pltpu.VMEM((d, FF_TILE), BF16),
                            pltpu.VMEM((FF_TILE, d), BF16),
                            pltpu.SemaphoreType.DMA(()),
                            pltpu.SemaphoreType.DMA(())]),
        out_shape=jax.ShapeDtypeStruct((xs.shape[0], d), F32),
        compiler_params=_params(("arbitrary", "arbitrary")),
        name="moe",
    )(che, chblk, chn, nvalid, xs, wg, wu, wd, bg, bu, bd)


def _combine_kernel(dest_ref, ys_ref, h_ref, p_ref, g2_ref, fg_ref, o_ref, buf_ref, sem,
                    *, tc, nsteps, final_norm):
    i = pl.program_id(0)
    t_total = nsteps * tc

    def issue(step, slot):
        for k in range(TOP_K):
            def group(g, c):
                j0 = pl.multiple_of(g * SUBLANES, SUBLANES)
                tile = buf_ref.at[slot, k, pl.ds(j0, SUBLANES)]
                for u in range(SUBLANES):
                    src = dest_ref[k * t_total + step * tc + j0 + u]
                    pltpu.make_async_copy(ys_ref.at[pl.ds(src, 1)], tile.at[pl.ds(u, 1)],
                                          sem.at[slot]).start()
                return c
            lax.fori_loop(0, tc // SUBLANES, group, 0)

    def drain(slot):
        for k in range(TOP_K):
            pltpu.make_async_copy(ys_ref.at[pl.ds(0, tc)], buf_ref.at[slot, k], sem.at[slot]).wait()

    slot = i % 2

    @pl.when(i == 0)
    def _():
        issue(0, 0)

    @pl.when(i + 1 < nsteps)
    def _():
        issue(i + 1, 1 - slot)

    drain(slot)
    p = p_ref[...]
    moe = p[:, 0:1] * buf_ref[slot, 0]
    for k in range(1, TOP_K):
        moe += p[:, k:k + 1] * buf_ref[slot, k]
    h = h_ref[...] + g2_ref[...] * moe
    if final_norm:
        h = h * lax.rsqrt(jnp.mean(h * h, axis=-1, keepdims=True) + EPS) * fg_ref[...]
    o_ref[...] = h


def _combine(dest_flat, ys, h, probs_t, mod3, final_g, seq, final_norm, tc=256):
    t, d = h.shape
    per_b = seq // tc
    nsteps = t // tc
    kern = functools.partial(_combine_kernel, tc=tc, nsteps=nsteps, final_norm=final_norm)
    return pl.pallas_call(
        kern,
        grid_spec=pltpu.PrefetchScalarGridSpec(
            num_scalar_prefetch=1,
            grid=(nsteps,),
            in_specs=[pl.BlockSpec(memory_space=pl.ANY),
                      pl.BlockSpec((tc, d), lambda i, s: (i, 0)),
                      pl.BlockSpec((tc, TOP_K), lambda i, s: (i, 0)),
                      pl.BlockSpec((None, 1, d), lambda i, s: ((i // per_b) * N_MOD + 5, 0, 0)),
                      pl.BlockSpec((1, d), lambda i, s: (0, 0))],
            out_specs=pl.BlockSpec((tc, d), lambda i, s: (i, 0)),
            scratch_shapes=[pltpu.VMEM((2, TOP_K, tc, d), F32),
                            pltpu.SemaphoreType.DMA((2,))]),
        out_shape=jax.ShapeDtypeStruct((t, d), F32),
        compiler_params=_params(("arbitrary",)),
        name="combine",
    )(dest_flat, ys, h, probs_t, mod3, final_g)


def _routing_plan(counts, t):
    nblk_total = -(-t * TOP_K // ROW_BLOCK) + N_EXPERTS
    nblk = (counts + ROW_BLOCK - 1) // ROW_BLOCK
    blk_end = jnp.cumsum(nblk)
    blk_start = blk_end - nblk
    ntail_max = nblk_total - t * TOP_K // ROW_BLOCK
    fill_blocks = jnp.concatenate([jnp.maximum(blk_end - 1, 0),
                                   jnp.minimum(blk_end[-1] + jnp.arange(ntail_max), nblk_total - 1)])
    nfill = N_EXPERTS + nblk_total - blk_end[-1]
    max_chunks = -(-nblk_total // CHUNK_BLOCKS) + N_EXPERTS
    nch = (nblk + CHUNK_BLOCKS - 1) // CHUNK_BLOCKS
    ch_end = jnp.cumsum(nch)
    ch_start = ch_end - nch
    nvalid = ch_end[-1]
    cid = jnp.minimum(jnp.arange(max_chunks, dtype=I32), nvalid - 1)
    che =jnp.clip(jnp.searchsorted(ch_end, cid, side='right'), 0, N_EXPERTS - 1).astype(I32)
    local = cid - ch_start[che]
    nch_e = jnp.maximum(nch[che], 1)
    n_mm = (nblk[che] + MM_ROW_BLOCKS - 1) // MM_ROW_BLOCKS
    size = n_mm // nch_e
    extra = n_mm % nch_e
    first_mm = local * size + jnp.minimum(local, extra)
    chblk = (blk_start[che] + first_mm * MM_ROW_BLOCKS).astype(I32)
    chn = jnp.minimum((size + (local < extra)) * MM_ROW_BLOCKS,
                      nblk[che] - first_mm * MM_ROW_BLOCKS).astype(I32)
    return (fill_blocks.astype(I32), nfill.astype(I32).reshape(1), che, chblk, chn,
            nvalid.astype(I32).reshape(1), nblk_total * ROW_BLOCK)


def kernel(x, c, w_ada, b_ada, norm1_g, w_in, lam_q1, lam_k1, lam_q2, lam_k2, subln_g, w_out_a, w_out_b, w_o, norm2_g, w_router, b_router, w_gate, b_gate, w_up, b_up, w_down, b_down, final_g):
    batch, seq, d = x.shape
    t = batch * seq
    depth = w_ada.shape[0]
    wa = N_HEADS_A * HEAD_DIM
    wb = N_HEADS_B * 2 * HEAD_DIM
    slopes = _alibi_slopes(N_HEADS_A + N_HEADS_B)
    slopes_a = jnp.asarray(slopes[:N_HEADS_A])
    slopes_b = jnp.asarray(slopes[N_HEADS_A:])
    c8 = jnp.pad(c, ((0, 8 - batch), (0, 0)))

    h = x.reshape(t, d)
    for l in range(depth):
        lambda_init = 0.8 - 0.6 * math.exp(-0.3 * l)
        mod = _ada(c8, w_ada[l], b_ada[l].reshape(1, -1))[:batch]
        mod3 = mod.reshape(batch * N_MOD, 1, d)

        u = _norm_mod(h, norm1_g[l].reshape(1, d), mod3, 1, 0, seq)
        proj = _inproj(u, w_in[l], q_tiles=(0, 3 * wa // 1024))
        y_a = _dil_attention(proj, slopes_a, batch, seq, 0, wa // HEAD_DIM, 2 * wa // HEAD_DIM)
        off_b = 3 * wa // (2 * HEAD_DIM)
        nb = wb // (2 * HEAD_DIM)
        y_b = _diff_attention(proj, slopes_b,
                              [v[l].reshape(1, HEAD_DIM) for v in (lam_q1, lam_k1, lam_q2, lam_k2)],
                              subln_g[l].reshape(1, -1), lambda_init, batch, seq,
                              off_b, off_b + nb, off_b + 2 * nb)
        wr_hi, wr_lo = _split_bf16(w_router[l].T)
        gate_col = (3 * wa + 3 * wb) // d
        h, u2p, logits_t = _merge(y_a, y_b, proj, h, mod3, norm2_g[l].reshape(1, d),
                                  w_out_a[l].astype(BF16), w_out_b[l].astype(BF16),
                                  w_o[l].astype(BF16), wr_hi, wr_lo, b_router[l].reshape(-1, 1),
                                  seq, gate_col, gate_col + 1)
        probs, dest, cnt = _route(logits_t)
        (fill_blocks, nfill, che, chblk, chn, nvalid, total_rows) = _routing_plan(cnt[:, 0], t)
        xs = _dispatch(dest.reshape(-1), fill_blocks, nfill, u2p, total_rows)
        ys = _moe(che, chblk, chn, nvalid, xs, w_gate[l], w_up[l], w_down[l],
                  b_gate[l].reshape(N_EXPERTS, 1, -1), b_up[l].reshape(N_EXPERTS, 1, -1),
                  b_down[l].reshape(N_EXPERTS, 1, -1))
        h = _combine(dest.reshape(-1), ys, h, probs.T, mod3, final_g.reshape(1, d), seq,
                     final_norm=(l == depth - 1))
    return h.reshape(batch, seq, d)
```

```python
import functools
import math

import numpy as np
import jax
import jax.numpy as jnp
from jax import lax
from jax.experimental import pallas as pl
from jax.experimental.pallas import tpu as pltpu

F32 = jnp.float32
BF16 = jnp.bfloat16
U32 = jnp.uint32
I32 = jnp.int32

HEAD_DIM = 128
N_HEADS_A = 8
N_HEADS_B = 4
DILATED_PATTERNS = ((128, 1), (512, 4), (2048, 16))
N_EXPERTS = 32
TOP_K = 4
SWIGLU_LIMIT = 7.0
SWIGLU_ALPHA = 1.702
N_MOD = 6
EPS = 1e-5
NEG = -1e30
LOG2E = math.log2(math.e)

LANES = 128
SUBLANES = 8
V7X_VMEM_LIMIT = 56 * 1024 * 1024

ROW_BLOCK = 128
CHUNK_BLOCKS = 10
MM_ROW_BLOCKS = 2
FF_TILE = 512


def _alibi_slopes(n):
    return np.array([2.0 ** (-8.0 * (i + 1) / n) for i in range(n)], dtype=np.float32)


def _nt_dot(a, b):
    return lax.dot_general(a, b, (((1,), (1,)), ((), ())), preferred_element_type=F32)


def _split_bf16(x):
    hi = x.astype(BF16)
    lo = (x - hi.astype(F32)).astype(BF16)
    return hi, lo


def _params(sem, vmem=V7X_VMEM_LIMIT):
    return pltpu.CompilerParams(dimension_semantics=sem, vmem_limit_bytes=vmem)


def _ada_kernel(c_ref, w_ref, b_ref, o_ref):
    c = c_ref[...]
    a = c * jax.nn.sigmoid(c)
    a_hi, a_lo = _split_bf16(a)
    w_hi, w_lo = _split_bf16(w_ref[...])
    acc = jnp.dot(a_hi, w_hi, preferred_element_type=F32)
    acc += jnp.dot(a_lo, w_hi, preferred_element_type=F32)
    acc += jnp.dot(a_hi, w_lo, preferred_element_type=F32)
    o_ref[...] = acc + b_ref[...]


def _ada(c8, w, b, tn=1024):
    m, d = c8.shape
    n = w.shape[1]
    return pl.pallas_call(
        _ada_kernel,
        grid=(n // tn,),
        in_specs=[pl.BlockSpec((m, d), lambda j: (0, 0)),
                  pl.BlockSpec((d, tn), lambda j: (0, j)),
                  pl.BlockSpec((1, tn), lambda j: (0, j))],
        out_specs=pl.BlockSpec((m, tn), lambda j: (0, j)),
        out_shape=jax.ShapeDtypeStruct((m, n), F32),
        compiler_params=_params(("arbitrary",)),
        name="ada",
    )(c8, w, b)


def _norm_mod_kernel(x_ref, g_ref, sc_ref, sh_ref, o_ref):
    x = x_ref[...]
    y = x * lax.rsqrt(jnp.mean(x * x, axis=-1, keepdims=True) + EPS) * g_ref[...]
    o_ref[...] = (y * (1.0 + sc_ref[...]) + sh_ref[...]).astype(o_ref.dtype)


def _norm_mod(x2, g, mod3, i_scale, i_shift, seq, tm=512):
    t, d = x2.shape
    per_b = seq // tm
    return pl.pallas_call(
        _norm_mod_kernel,
        grid=(t // tm,),
        in_specs=[pl.BlockSpec((tm, d), lambda i: (i, 0)),
                  pl.BlockSpec((1, d), lambda i: (0, 0)),
                  pl.BlockSpec((None, 1, d), lambda i: ((i // per_b) * N_MOD + i_scale, 0, 0)),
                  pl.BlockSpec((None, 1, d), lambda i: ((i // per_b) * N_MOD + i_shift, 0, 0))],
        out_specs=pl.BlockSpec((tm, d), lambda i: (i, 0)),
        out_shape=jax.ShapeDtypeStruct((t, d), BF16),
        compiler_params=_params(("arbitrary",)),
        name="norm1",
    )(x2, g, mod3, mod3)


def _inproj_kernel(u_ref, w_ref, o_ref, wbf_ref, *, q_tiles, scale):
    n = pl.program_id(0)

    @pl.when(pl.program_id(1) == 0)
    def _():
        wbf_ref[...] = w_ref[...].astype(BF16)

    acc = jnp.dot(u_ref[...], wbf_ref[...], preferred_element_type=F32)
    is_q = functools.reduce(jnp.logical_or, [n == q for q in q_tiles])
    o_ref[...] = (acc * jnp.where(is_q, scale, 1.0)).astype(BF16)


def _inproj(u, w, q_tiles, tm=1024, tn=1024):
    t, d = u.shape
    n = w.shape[1]
    return pl.pallas_call(
        functools.partial(_inproj_kernel, q_tiles=q_tiles, scale=HEAD_DIM ** -0.5 * LOG2E),
        grid=(n // tn, t // tm),
        in_specs=[pl.BlockSpec((tm, d), lambda j, i: (i, 0)),
                  pl.BlockSpec((d, tn), lambda j, i: (0, j))],
        out_specs=pl.BlockSpec((tm, tn), lambda j, i: (i, j)),
        out_shape=jax.ShapeDtypeStruct((t, n), BF16),
        scratch_shapes=[pltpu.VMEM((d, tn), BF16)],
        compiler_params=_params(("arbitrary", "arbitrary")),
        name="inproj",
    )(u, w)


def _dil_tables(tq, span):
    nside = span // tq
    o = lax.broadcasted_iota(I32, (2 * nside + 1, tq, tq), 0) - nside
    i = lax.broadcasted_iota(I32, (2 * nside + 1, tq, tq), 1)
    j = lax.broadcasted_iota(I32, (2 * nside + 1, tq, tq), 2)
    ad = jnp.abs(o * tq + j - i)
    mult = jnp.zeros_like(ad)
    for window, dil in DILATED_PATTERNS:
        mult += ((ad % dil == 0) & (ad // dil <= window // (2 * dil))).astype(I32)
    lm = jnp.where(mult > 0, jnp.log2(jnp.maximum(mult, 1).astype(F32)), NEG)
    return ad.astype(F32) * LOG2E, lm


def _dil_kernel(slopes_ref, q_ref, k_ref, v_ref, ad_ref, lm_ref, o_ref, bias_ref, vt_ref, st_ref,
                *, tq, nside, nq):
    h = pl.program_id(1)
    i = pl.program_id(2)

    @pl.when(i == 0)
    def _():
        bias_ref[:2 * nside + 1] = lm_ref[...] - slopes_ref[h] * ad_ref[...]
        bias_ref[2 * nside + 1] = jnp.full((tq, tq), NEG, F32)
        for kb in range(nq):
            vt_ref[kb] = v_ref[kb * tq:(kb + 1) * tq, :].astype(F32).T.astype(BF16)

    nwin = 2 * nside + 1
    b0 = jnp.clip(i - nside, 0, nq - nwin)
    q = q_ref[...]

    def score_block(j, m):
        o = b0 + j - i
        plane = jnp.where(jnp.abs(o) <= nside, nside - o, nwin)
        kb = k_ref[pl.ds(pl.multiple_of((b0 + j) * tq, tq), tq), :]
        s = _nt_dot(kb, q) + bias_ref[plane]
        st_ref[j] = s
        return jnp.maximum(m, jnp.max(s, axis=0, keepdims=True))

    m = lax.fori_loop(0, nwin, score_block, jnp.full((1, tq), NEG, F32))
    l = acc = None
    for j in range(nwin):
        p = jnp.exp2(st_ref[j] - m)
        cl = jnp.sum(p, axis=0, keepdims=True)
        ca = jnp.dot(vt_ref[b0 + j], p.astype(BF16), preferred_element_type=F32)
        l = cl if j == 0 else l + cl
        acc = ca if j == 0 else acc + ca
    o_ref[...] = (acc / l).T.astype(o_ref.dtype)


def _dil_attention(proj, slopes, batch, seq, q_col, k_col, v_col, tq=512):
    span = max(w // 2 for w, _ in DILATED_PATTERNS)
    nside = span // tq
    nq = seq // tq
    ad, lm = _dil_tables(tq, span)
    tab_spec = pl.BlockSpec((2 * nside + 1, tq, tq), lambda b, h, i, s: (0, 0, 0))
    kern = functools.partial(_dil_kernel, tq=tq, nside=nside, nq=nq)
    return pl.pallas_call(
        kern,
        grid_spec=pltpu.PrefetchScalarGridSpec(
            num_scalar_prefetch=1,
            grid=(batch, N_HEADS_A, nq),
            in_specs=[pl.BlockSpec((tq, HEAD_DIM), lambda b, h, i, s: (b * nq + i, q_col + h)),
                      pl.BlockSpec((seq, HEAD_DIM), lambda b, h, i, s: (b, k_col + h)),
                      pl.BlockSpec((seq, HEAD_DIM), lambda b, h, i, s: (b, v_col + h)),
                      tab_spec, tab_spec],
            out_specs=pl.BlockSpec((tq, HEAD_DIM), lambda b, h, i, s: (b * nq + i, h)),
            scratch_shapes=[pltpu.VMEM((2 * nside + 2, tq, tq), F32),
                            pltpu.VMEM((nq, HEAD_DIM, tq), BF16),
                            pltpu.VMEM((2 * nside + 1, tq, tq), F32)]),
        out_shape=jax.ShapeDtypeStruct((batch * seq, N_HEADS_A * HEAD_DIM), BF16),
        compiler_params=_params(("arbitrary", "arbitrary", "arbitrary")),
        name="dil_attn",
    )(slopes, proj, proj, proj, ad, lm)


def _diff_kernel(slopes_ref, lq1_ref, lk1_ref, lq2_ref, lk2_ref, g_ref, q_ref, k_ref, v_ref, o_ref,
                 vt_ref, bias_ref, st1_ref, st2_ref, *, tq, tk, nk, lambda_init):
    h = pl.program_id(1)
    i = pl.program_id(2)

    seq = k_ref.shape[0]
    nq = seq // tq

    @pl.when(i == 0)
    def _():
        for kb in range(nk):
            cols = slice(kb * tk, (kb + 1) * tk)
            vt_ref[:, cols] = v_ref[cols, :].astype(F32).T.astype(BF16)

        neg_c = -slopes_ref[h] * LOG2E
        base = (lax.broadcasted_iota(I32, (tq, tq), 0) - lax.broadcasted_iota(I32, (tq, tq), 1)
                - (seq - tq))

        def fill(r, carry):
            rows = pl.ds(pl.multiple_of(r * tq, tq), tq)
            bias_ref[rows, :] = jnp.abs(base + r * tq).astype(F32) * neg_c
            return carry

        lax.fori_loop(0, 2 * nq - 1, fill, 0)

    q = q_ref[...]
    qs = (q[:, :HEAD_DIM], q[:, HEAD_DIM:])
    st_refs = (st1_ref, st2_ref)
    w0 = (nq - 1 - i) * tq
    def score_chunk(c, mx):
        rows = pl.ds(pl.multiple_of(c * tk, tk), tk)
        kc = k_ref[rows, :]
        b = bias_ref[pl.ds(pl.multiple_of(w0 + c * tk, tq), tk), :]
        out = []
        for j in range(2):
            s = _nt_dot(kc[:, j * HEAD_DIM:(j + 1) * HEAD_DIM], qs[j]) + b
            st_refs[j][rows, :] = s
            out.append(jnp.maximum(mx[j], jnp.max(s, axis=0, keepdims=True)))
        return tuple(out)

    neg = jnp.full((1, tq), NEG, F32)
    mx = lax.fori_loop(0, nk, score_chunk, (neg, neg), unroll=2)
    ls = [None, None]
    accs = [None, None]
    for c in range(nk):
        rows = slice(c * tk, (c + 1) * tk)
        vt = vt_ref[:, rows]
        for j in range(2):
            p = jnp.exp2(st_refs[j][rows, :] - mx[j])
            cl = jnp.sum(p, axis=0, keepdims=True)
            ca = jnp.dot(vt, p.astype(BF16), preferred_element_type=F32)
            ls[j] = cl if c == 0 else ls[j] + cl
            accs[j] = ca if c == 0 else accs[j] + ca
    (a1, a2), (l1, l2) = accs, ls

    lam = (jnp.exp(jnp.sum(lq1_ref[...] * lk1_ref[...], axis=-1, keepdims=True))
           - jnp.exp(jnp.sum(lq2_ref[...] * lk2_ref[...], axis=-1, keepdims=True)) + lambda_init)
    ot = a1 / l1 - lam * (a2 / l2)
    yt = ot * lax.rsqrt(jnp.mean(ot * ot, axis=0, keepdims=True) + EPS)
    o_ref[...] = (yt.T * (g_ref[...] * (1.0 - lambda_init))).astype(o_ref.dtype)


def _diff_attention(proj, slopes, lam_vecs, subln_g, lambda_init, batch, seq, q_col, k_col, v_col,
                    tq=512, tk=512):
    nq = seq // tq
    w = 2 * HEAD_DIM
    vec_spec = pl.BlockSpec((1, HEAD_DIM), lambda b, h, i, s: (0, 0))
    kern = functools.partial(_diff_kernel, tq=tq, tk=tk, nk=seq // tk, lambda_init=lambda_init)
    return pl.pallas_call(
        kern,
        grid_spec=pltpu.PrefetchScalarGridSpec(
            num_scalar_prefetch=1,
            grid=(batch, N_HEADS_B, nq),
            in_specs=[vec_spec, vec_spec, vec_spec, vec_spec,
                      pl.BlockSpec((1, w), lambda b, h, i, s: (0, 0)),
                      pl.BlockSpec((tq, w), lambda b, h, i, s: (b * nq + i, q_col + h)),
                      pl.BlockSpec((seq, w), lambda b, h, i, s: (b, k_col + h)),
                      pl.BlockSpec((seq, w), lambda b, h, i, s: (b, v_col + h))],
            out_specs=pl.BlockSpec((tq, w), lambda b, h, i, s: (b * nq + i, h)),
            scratch_shapes=[pltpu.VMEM((w, seq), BF16),
                            pltpu.VMEM((2 * seq - tq, tq), F32),
                            pltpu.VMEM((seq, tq), F32),
                            pltpu.VMEM((seq, tq), F32)]),
        out_shape=jax.ShapeDtypeStruct((batch * seq, N_HEADS_B * w), BF16),
        compiler_params=_params(("arbitrary", "arbitrary", "arbitrary")),
        name="diff_attn",
    )(slopes, *lam_vecs, subln_g, proj, proj, proj)


def _merge_kernel(ya_ref, yb_ref, ga_ref, gb_ref, x_ref, g1_ref, sc2_ref, sh2_ref, n2g_ref,
                  woa_ref, wob_ref, wo_ref, wrh_ref, wrl_ref, br_ref,
                  h_ref, u2p_ref, lg_ref):
    a = jnp.dot(ya_ref[...], woa_ref[...], preferred_element_type=F32)
    b = jnp.dot(yb_ref[...], wob_ref[...], preferred_element_type=F32)
    merged = (jax.nn.sigmoid(ga_ref[...].astype(F32)) * a
              + jax.nn.sigmoid(gb_ref[...].astype(F32)) * b)
    h = x_ref[...] + g1_ref[...] * jnp.dot(merged.astype(BF16), wo_ref[...],
                                           preferred_element_type=F32)
    h_ref[...] = h
    y = h * lax.rsqrt(jnp.mean(h * h, axis=-1, keepdims=True) + EPS) * n2g_ref[...]
    u2 = y * (1.0 + sc2_ref[...]) + sh2_ref[...]
    hi, lo = _split_bf16(u2)
    lg = _nt_dot(wrh_ref[...], hi) + _nt_dot(wrl_ref[...], hi) + _nt_dot(wrh_ref[...], lo)
    lg_ref[...] = lg + br_ref[...]
    bits = pltpu.bitcast(hi.astype(F32), U32)
    half = bits.shape[1] // 2
    u2p_ref[...] = (bits[:, :half] >> 16) | (bits[:, half:] & jnp.uint32(0xFFFF0000))


def _merge(ya, yb, proj, x2, mod3, n2g, woa, wob, wo, wr_hi, wr_lo, br, seq, ga_col, gb_col, tm=256):
    t, d = x2.shape
    per_b = seq // tm
    wa = ya.shape[1]
    wb = yb.shape[1]
    ne = wr_hi.shape[0]

    def mod_spec(idx):
        return pl.BlockSpec((None, 1, d), lambda i: ((i // per_b) * N_MOD + idx, 0, 0))

    def const_spec(shape):
        return pl.BlockSpec(shape, lambda i: (0,) * len(shape), pipeline_mode=pl.Buffered(1))

    return pl.pallas_call(
        _merge_kernel,
        grid=(t // tm,),
        in_specs=[pl.BlockSpec((tm, wa), lambda i: (i, 0)),
                  pl.BlockSpec((tm, wb), lambda i: (i, 0)),
                  pl.BlockSpec((tm, d), lambda i: (i, ga_col)),
                  pl.BlockSpec((tm, d), lambda i: (i, gb_col)),
                  pl.BlockSpec((tm, d), lambda i: (i, 0)),
                  mod_spec(2), mod_spec(4), mod_spec(3),
                  const_spec((1, d)),
                  const_spec((wa, d)), const_spec((wb, d)), const_spec((d, d)),
                  const_spec((ne, d)), const_spec((ne, d)), const_spec((ne, 1))],
        out_specs=[pl.BlockSpec((tm, d), lambda i: (i, 0)),
                   pl.BlockSpec((tm, d // 2), lambda i: (i, 0)),
                   pl.BlockSpec((ne, tm), lambda i: (0, i))],
        out_shape=[jax.ShapeDtypeStruct((t, d), F32),
                   jax.ShapeDtypeStruct((t, d // 2), U32),
                   jax.ShapeDtypeStruct((ne, t), F32)],
        compiler_params=_params(("arbitrary",)),
        name="merge",
    )(ya, yb, proj, proj, x2, mod3, mod3, mod3, n2g, woa, wob, wo, wr_hi, wr_lo, br)


def _route_kernel(lg_ref, p_ref, dest_ref, cnt_ref, carry_ref, total_ref):
    sweep = pl.program_id(0)

    @pl.when(pl.program_id(1) == 0)
    def _():
        @pl.when(sweep == 0)
        def _():
            total_ref[...] = jnp.zeros_like(total_ref)

        @pl.when(sweep == 1)
        def _():
            total_ref[...] = carry_ref[...]

        carry_ref[...] = jnp.zeros_like(carry_ref)

    lg = lg_ref[...]
    ne, tr = lg.shape
    eio = lax.broadcasted_iota(I32, (ne, tr), 0)
    work = lg
    vals, hots = [], []
    for k in range(TOP_K):
        mx = jnp.max(work, axis=0, keepdims=True)
        am = jnp.min(jnp.where(work == mx, eio, ne), axis=0, keepdims=True)
        hot = eio == am
        vals.append(mx)
        hots.append(hot)
        work = jnp.where(hot, -jnp.inf, work)
    exps = [jnp.exp(v - vals[0]) for v in vals]
    denom = functools.reduce(jnp.add, exps)
    for k in range(TOP_K):
        p_ref[k:k + 1, :] = exps[k] / denom
    chosen = functools.reduce(jnp.logical_or, hots)
    sel = jnp.where(chosen, 1.0, 0.0)
    tri = (lax.broadcasted_iota(I32, (tr, tr), 0) < lax.broadcasted_iota(I32, (tr, tr), 1))
    before = jnp.dot(sel.astype(BF16), jnp.where(tri, 1.0, 0.0).astype(BF16),
                     preferred_element_type=F32)
    carry = carry_ref[...]
    nblk = jnp.floor((total_ref[...][:, 0:1] + (ROW_BLOCK - 1)) * (1.0 / ROW_BLOCK))
    e_row = lax.broadcasted_iota(I32, (ne, ne), 0)
    e_col = lax.broadcasted_iota(I32, (ne, ne), 1)
    nblk_lanes = jnp.sum(jnp.where(e_row == e_col, nblk, 0.0), axis=0, keepdims=True)
    first_row = jnp.sum(jnp.where(e_col < e_row, nblk_lanes, 0.0), axis=1,
                        keepdims=True) * ROW_BLOCK
    place = before + (carry[:, 0:1] + first_row)
    for k in range(TOP_K):
        dest_ref[k:k + 1, :] = jnp.sum(jnp.where(hots[k], place, 0.0), axis=0,
                                       keepdims=True).astype(I32)
    carry = carry + jnp.sum(sel, axis=1, keepdims=True)
    carry_ref[...] = carry
    cnt_ref[...] = carry.astype(I32)


def _route(logits_t, tr=512):
    ne, t = logits_t.shape
    slot_spec = pl.BlockSpec((None, TOP_K, tr), lambda s, i: (s, 0, i))
    probs, dest, cnt = pl.pallas_call(
        _route_kernel,
        grid=(2, t // tr),
        in_specs=[pl.BlockSpec((ne, tr), lambda s, i: (0, i))],
        out_specs=[slot_spec, slot_spec,
                   pl.BlockSpec((None, ne, LANES), lambda s, i: (s, 0, 0))],
        out_shape=[jax.ShapeDtypeStruct((2, TOP_K, t), F32),
                   jax.ShapeDtypeStruct((2, TOP_K, t), I32),
                   jax.ShapeDtypeStruct((2, ne, LANES), I32)],
        scratch_shapes=[pltpu.VMEM((ne, LANES), F32), pltpu.VMEM((ne, LANES), F32)],
        compiler_params=_params(("arbitrary", "arbitrary")),
        name="route",
    )(logits_t)
    return probs[1], dest[1], cnt[1]


def _dispatch_kernel(dest_ref, fill_ref, nfill_ref, src_ref, dst_ref, sem, fill_sem, *, tm, t_total):
    i = pl.program_id(0)

    @pl.when(i == 0)
    def _():
        def fill(b):
            rows = pl.ds(pl.multiple_of(fill_ref[b] * ROW_BLOCK, ROW_BLOCK), ROW_BLOCK)
            return pltpu.make_async_copy(src_ref.at[pl.ds(0, ROW_BLOCK)], dst_ref.at[rows], fill_sem)

        def fill_start(b, c):
            fill(b).start()
            return c

        def fill_wait(b, c):
            fill(b).wait()
            return c

        lax.fori_loop(0, nfill_ref[0], fill_start, 0)
        lax.fori_loop(0, nfill_ref[0], fill_wait, 0)

    for k in range(TOP_K):
        def group(g, c):
            j0 = pl.multiple_of(g * SUBLANES, SUBLANES)
            tile = src_ref.at[pl.ds(j0, SUBLANES)]
            for u in range(SUBLANES):
                row = dest_ref[k * t_total + i * tm + j0 + u]
                pltpu.make_async_copy(tile.at[pl.ds(u, 1)], dst_ref.at[pl.ds(row, 1)], sem).start()
            return c
        lax.fori_loop(0, tm // SUBLANES, group, 0)
    for k in range(TOP_K):
        pltpu.make_async_copy(src_ref, dst_ref.at[pl.ds(0, tm)], sem).wait()


def _dispatch(dest_flat, fill_blocks, nfill, u2p, total_rows, tm=512):
    t, w = u2p.shape
    kern = functools.partial(_dispatch_kernel, tm=tm, t_total=t)
    return pl.pallas_call(
        kern,
        grid_spec=pltpu.PrefetchScalarGridSpec(
            num_scalar_prefetch=3,
            grid=(t // tm,),
            in_specs=[pl.BlockSpec((tm, w), lambda i, d, fb, nf: (i, 0))],
            out_specs=pl.BlockSpec(memory_space=pl.ANY),
            scratch_shapes=[pltpu.SemaphoreType.DMA(()), pltpu.SemaphoreType.DMA(())]),
        out_shape=jax.ShapeDtypeStruct((total_rows, w), U32),
        compiler_params=_params(("arbitrary",)),
        name="dispatch",
    )(dest_flat, fill_blocks, nfill, u2p)


def _moe_kernel(che_ref, chblk_ref, chn_ref, nvalid_ref,
                xs_ref, wg_ref, wu_ref, wd_ref, bg_ref, bu_ref, bd_ref,
                ys_ref,
                xin_ref, xbf_ref, yacc_ref, act_ref, wgb_ref, wub_ref, wdb_ref, in_sem, out_sem,
                *, nff, nsplit, mm_rows):
    c = pl.program_id(0)
    f = pl.program_id(1)
    nvalid = nvalid_ref[0]
    half = xin_ref.shape[1]

    def in_copy(cc, j):
        row = pl.multiple_of((chblk_ref[cc] + j) * ROW_BLOCK, ROW_BLOCK)
        return pltpu.make_async_copy(xs_ref.at[pl.ds(row, ROW_BLOCK)],
                                     xin_ref.at[pl.ds(j * ROW_BLOCK, ROW_BLOCK)], in_sem)

    def out_copy(cc, j):
        row = pl.multiple_of((chblk_ref[cc] + j) * ROW_BLOCK, ROW_BLOCK)
        return pltpu.make_async_copy(yacc_ref.at[pl.ds(j * ROW_BLOCK, ROW_BLOCK)],
                                     ys_ref.at[pl.ds(row, ROW_BLOCK)], out_sem)

    def for_blocks(cc, fn):
        def one(j, carry):
            fn(cc, j)
            return carry
        lax.fori_loop(0, chn_ref[cc], one, 0)

    def chunk_step():
        @pl.when(f == 0)
        def _():
            @pl.when(c == 0)
            def _():
                xbf_ref[...] = jnp.zeros_like(xbf_ref)
                for_blocks(c, lambda cc, j: in_copy(cc, j).start())

            for_blocks(c, lambda cc, j: in_copy(cc, j).wait())

            def unpack(cc, j):
                rows = pl.ds(pl.multiple_of(j * ROW_BLOCK, ROW_BLOCK), ROW_BLOCK)
                w = xin_ref[rows, :]
                xbf_ref[rows, :half] = pltpu.bitcast(w << 16, F32).astype(BF16)
                xbf_ref[rows, half:] = pltpu.bitcast(w & jnp.uint32(0xFFFF0000), F32).astype(BF16)

            for_blocks(c, unpack)

            @pl.when(c + 1 < nvalid)
            def _():
                for_blocks(c + 1, lambda cc, j: in_copy(cc, j).start())

        mm_blocks = mm_rows // ROW_BLOCK
        n_mm = (chn_ref[c] + (mm_blocks - 1)) // mm_blocks

        def gate_up(rows, wg, wu):
            x = xbf_ref[rows, :]
            g = jnp.minimum(jnp.dot(x, wg, preferred_element_type=F32) + bg_ref[...], SWIGLU_LIMIT)
            u = jnp.clip(jnp.dot(x, wu, preferred_element_type=F32) + bu_ref[...],
                         -SWIGLU_LIMIT, SWIGLU_LIMIT)
            act_ref[rows, :] = ((u + 1.0) * (g * jax.nn.sigmoid(SWIGLU_ALPHA * g))).astype(BF16)

        wg = wg_ref[...].astype(BF16)
        wu = wu_ref[...].astype(BF16)
        wgb_ref[...] = wg
        wub_ref[...] = wu
        gate_up(slice(0, mm_rows), wg, wu)
        wdb_ref[...] = wd_ref[...].astype(BF16)

        def gate_up_body(rb, carry):
            gate_up(pl.ds(pl.multiple_of(rb * mm_rows, mm_rows), mm_rows), wgb_ref[...], wub_ref[...])
            return carry

        lax.fori_loop(1, n_mm, gate_up_body, 0)

        @pl.when((f == 0) & (c > 0))
        def _():
            for_blocks(c - 1, lambda cc, j: out_copy(cc, j).wait())

        first = f == 0
        ncol = yacc_ref.shape[1] // nsplit

        def down(rb, carry):
            rows = pl.ds(pl.multiple_of(rb * mm_rows, mm_rows), mm_rows)
            a = act_ref[rows, :]
            for s in range(nsplit):
                cols = slice(s * ncol, (s + 1) * ncol)
                part = jnp.dot(a, wdb_ref[:, cols], preferred_element_type=F32)
                base = jnp.where(first, jnp.broadcast_to(bd_ref[:, cols], part.shape),
                                 yacc_ref[rows, cols])
                yacc_ref[rows, cols] = base + part

            @pl.when(f == nff - 1)
            def _():
                for u in range(mm_blocks):
                    @pl.when(rb * mm_blocks + u < chn_ref[c])
                    def _():
                        out_copy(c, rb * mm_blocks + u).start()

            return carry

        lax.fori_loop(0, n_mm, down, 0)

        @pl.when(f == nff - 1)
        def _():
            @pl.when(c == nvalid - 1)
            def _():
                for_blocks(c, lambda cc, j: out_copy(cc, j).wait())

                def fill(b):
                    rows = pl.ds(pl.multiple_of(b * ROW_BLOCK, ROW_BLOCK), ROW_BLOCK)
                    return pltpu.make_async_copy(yacc_ref.at[pl.ds(0, ROW_BLOCK)], ys_ref.at[rows],
                                                 out_sem)

                def fill_start(b, carry):
                    fill(b).start()
                    return carry

                def fill_wait(b, carry):
                    fill(b).wait()
                    return carry

                nused = chblk_ref[c] + chn_ref[c]
                ntotal = ys_ref.shape[0] // ROW_BLOCK
                lax.fori_loop(nused, ntotal, fill_start, 0)
                lax.fori_loop(nused, ntotal, fill_wait, 0)

    chunk_step()


def _moe(che, chblk, chn, nvalid, xs, wg, wu, wd, bg, bu, bd):
    ne, d, dff = wg.shape
    rows = CHUNK_BLOCKS * ROW_BLOCK
    nff = dff // FF_TILE

    kern = functools.partial(_moe_kernel, nff=nff, nsplit=4, mm_rows=MM_ROW_BLOCKS * ROW_BLOCK)
    return pl.pallas_call(
        kern,
        grid_spec=pltpu.PrefetchScalarGridSpec(
            num_scalar_prefetch=4,
            grid=(nvalid[0], nff),
            in_specs=[pl.BlockSpec(memory_space=pl.ANY),
                      pl.BlockSpec((None, d, FF_TILE), lambda c, f, e, b, n, nv: (e[c], 0, f)),
                      pl.BlockSpec((None, d, FF_TILE), lambda c, f, e, b, n, nv: (e[c], 0, f)),
                      pl.BlockSpec((None, FF_TILE, d), lambda c, f, e, b, n, nv: (e[c], f, 0)),
                      pl.BlockSpec((None, 1, FF_TILE), lambda c, f, e, b, n, nv: (e[c], 0, f)),
                      pl.BlockSpec((None, 1, FF_TILE), lambda c, f, e, b, n, nv: (e[c], 0, f)),
                      pl.BlockSpec((None, 1, d), lambda c, f, e, b, n, nv: (e[c], 0, 0))],
            out_specs=pl.BlockSpec(memory_space=pl.ANY),
            scratch_shapes=[pltpu.VMEM((rows, d // 2), U32),
                            pltpu.VMEM((rows, d), BF16),
                            pltpu.VMEM((rows, d), F32),
                            pltpu.VMEM((rows, FF_TILE), BF16),
                            pltpu.VMEM((d, FF_TILE), BF16),
                            pltpu.VMEM((d, FF_TILE), BF16),
                            pltpu.VMEM((FF_TILE, d), BF16),
                            pltpu.SemaphoreType.DMA(()),
                            pltpu.SemaphoreType.DMA(())]),
        out_shape=jax.ShapeDtypeStruct((xs.shape[0], d), F32),
        compiler_params=_params(("arbitrary", "arbitrary")),
        name="moe",
    )(che, chblk, chn, nvalid, xs, wg, wu, wd, bg, bu, bd)


def _combine_kernel(dest_ref, ys_ref, h_ref, p_ref, g2_ref, fg_ref, o_ref, buf_ref, sem,
                    *, tc, nsteps, final_norm):
    i = pl.program_id(0)
    t_total = nsteps * tc

    def issue(step, slot):
        for k in range(TOP_K):
            def group(g, c):
                j0 = pl.multiple_of(g * SUBLANES, SUBLANES)
                tile = buf_ref.at[slot, k, pl.ds(j0, SUBLANES)]
                for u in range(SUBLANES):
                    src = dest_ref[k * t_total + step * tc + j0 + u]
                    pltpu.make_async_copy(ys_ref.at[pl.ds(src, 1)], tile.at[pl.ds(u, 1)],
                                          sem.at[slot]).start()
                return c
            lax.fori_loop(0, tc // SUBLANES, group, 0)

    def drain(slot):
        for k in range(TOP_K):
            pltpu.make_async_copy(ys_ref.at[pl.ds(0, tc)], buf_ref.at[slot, k], sem.at[slot]).wait()

    slot = i % 2

    @pl.when(i == 0)
    def _():
        issue(0, 0)

    @pl.when(i + 1 < nsteps)
    def _():
        issue(i + 1, 1 - slot)

    drain(slot)
    p = p_ref[...]
    moe = p[:, 0:1] * buf_ref[slot, 0]
    for k in range(1, TOP_K):
        moe += p[:, k:k + 1] * buf_ref[slot, k]
    h = h_ref[...] + g2_ref[...] * moe
    if final_norm:
        h = h * lax.rsqrt(jnp.mean(h * h, axis=-1, keepdims=True) + EPS) * fg_ref[...]
    o_ref[...] = h


def _combine(dest_flat, ys, h, probs_t, mod3, final_g, seq, final_norm, tc=256):
    t, d = h.shape
    per_b = seq // tc
    nsteps = t // tc
    kern = functools.partial(_combine_kernel, tc=tc, nsteps=nsteps, final_norm=final_norm)
    return pl.pallas_call(
        kern,
        grid_spec=pltpu.PrefetchScalarGridSpec(
            num_scalar_prefetch=1,
            grid=(nsteps,),
            in_specs=[pl.BlockSpec(memory_space=pl.ANY),
                      pl.BlockSpec((tc, d), lambda i, s: (i, 0)),
                      pl.BlockSpec((tc, TOP_K), lambda i, s: (i, 0)),
                      pl.BlockSpec((None, 1, d), lambda i, s: ((i // per_b) * N_MOD + 5, 0, 0)),
                      pl.BlockSpec((1, d), lambda i, s: (0, 0))],
            out_specs=pl.BlockSpec((tc, d), lambda i, s: (i, 0)),
            scratch_shapes=[pltpu.VMEM((2, TOP_K, tc, d), F32),
                            pltpu.SemaphoreType.DMA((2,))]),
        out_shape=jax.ShapeDtypeStruct((t, d), F32),
        compiler_params=_params(("arbitrary",)),
        name="combine",
    )(dest_flat, ys, h, probs_t, mod3, final_g)


def _routing_plan(counts, t):
    nblk_total = -(-t * TOP_K // ROW_BLOCK) + N_EXPERTS
    nblk = (counts + ROW_BLOCK - 1) // ROW_BLOCK
    blk_end = jnp.cumsum(nblk)
    blk_start = blk_end - nblk
    ntail_max = nblk_total - t * TOP_K // ROW_BLOCK
    fill_blocks = jnp.concatenate([jnp.maximum(blk_end - 1, 0),
                                   jnp.minimum(blk_end[-1] + jnp.arange(ntail_max), nblk_total - 1)])
    nfill = N_EXPERTS + nblk_total - blk_end[-1]
    max_chunks = -(-nblk_total // CHUNK_BLOCKS) + N_EXPERTS
    nch = (nblk + CHUNK_BLOCKS - 1) // CHUNK_BLOCKS
    ch_end = jnp.cumsum(nch)
    ch_start = ch_end - nch
    nvalid = ch_end[-1]
    cid = jnp.minimum(jnp.arange(max_chunks, dtype=I32), nvalid - 1)
    che =jnp.clip(jnp.searchsorted(ch_end, cid, side='right'), 0, N_EXPERTS - 1).astype(I32)
    local = cid - ch_start[che]
    nch_e = jnp.maximum(nch[che], 1)
    n_mm = (nblk[che] + MM_ROW_BLOCKS - 1) // MM_ROW_BLOCKS
    size = n_mm // nch_e
    extra = n_mm % nch_e
    first_mm = local * size + jnp.minimum(local, extra)
    chblk = (blk_start[che] + first_mm * MM_ROW_BLOCKS).astype(I32)
    chn = jnp.minimum((size + (local < extra)) * MM_ROW_BLOCKS,
                      nblk[che] - first_mm * MM_ROW_BLOCKS).astype(I32)
    return (fill_blocks.astype(I32), nfill.astype(I32).reshape(1), che, chblk, chn,
            nvalid.astype(I32).reshape(1), nblk_total * ROW_BLOCK)


def kernel(x, c, w_ada, b_ada, norm1_g, w_in, lam_q1, lam_k1, lam_q2, lam_k2, subln_g, w_out_a, w_out_b, w_o, norm2_g, w_router, b_router, w_gate, b_gate, w_up, b_up, w_down, b_down, final_g):
    batch, seq, d = x.shape
    t = batch * seq
    depth = w_ada.shape[0]
    wa = N_HEADS_A * HEAD_DIM
    wb = N_HEADS_B * 2 * HEAD_DIM
    slopes = _alibi_slopes(N_HEADS_A + N_HEADS_B)
    slopes_a = jnp.asarray(slopes[:N_HEADS_A])
    slopes_b = jnp.asarray(slopes[N_HEADS_A:])
    c8 = jnp.pad(c, ((0, 8 - batch), (0, 0)))

    h = x.reshape(t, d)
    for l in range(depth):
        lambda_init = 0.8 - 0.6 * math.exp(-0.3 * l)
        mod = _ada(c8, w_ada[l], b_ada[l].reshape(1, -1))[:batch]
        mod3 = mod.reshape(batch * N_MOD, 1, d)

        u = _norm_mod(h, norm1_g[l].reshape(1, d), mod3, 1, 0, seq)
        proj = _inproj(u, w_in[l], q_tiles=(0, 3 * wa // 1024))
        y_a = _dil_attention(proj, slopes_a, batch, seq, 0, wa // HEAD_DIM, 2 * wa // HEAD_DIM)
        off_b = 3 * wa // (2 * HEAD_DIM)
        nb = wb // (2 * HEAD_DIM)
        y_b = _diff_attention(proj, slopes_b,
                              [v[l].reshape(1, HEAD_DIM) for v in (lam_q1, lam_k1, lam_q2, lam_k2)],
                              subln_g[l].reshape(1, -1), lambda_init, batch, seq,
                              off_b, off_b + nb, off_b + 2 * nb)
        wr_hi, wr_lo = _split_bf16(w_router[l].T)
        gate_col = (3 * wa + 3 * wb) // d
        h, u2p, logits_t = _merge(y_a, y_b, proj, h, mod3, norm2_g[l].reshape(1, d),
                                  w_out_a[l].astype(BF16), w_out_b[l].astype(BF16),
                                  w_o[l].astype(BF16), wr_hi, wr_lo, b_router[l].reshape(-1, 1),
                                  seq, gate_col, gate_col + 1)
        probs, dest, cnt = _route(logits_t)
        (fill_blocks, nfill, che, chblk, chn, nvalid, total_rows) = _routing_plan(cnt[:, 0], t)
        xs = _dispatch(dest.reshape(-1), fill_blocks, nfill, u2p, total_rows)
        ys = _moe(che, chblk, chn, nvalid, xs, w_gate[l], w_up[l], w_down[l],
                  b_gate[l].reshape(N_EXPERTS, 1, -1), b_up[l].reshape(N_EXPERTS, 1, -1),
                  b_down[l].reshape(N_EXPERTS, 1, -1))
        h = _combine(dest.reshape(-1), ys, h, probs.T, mod3, final_g.reshape(1, d), seq,
                     final_norm=(l == depth - 1))
    return h.reshape(batch, seq, d)
```

```python
import functools
import math

import numpy as np
import jax
import jax.numpy as jnp
from jax import lax
from jax.experimental import pallas as pl
from jax.experimental.pallas import tpu as pltpu

F32 = jnp.float32
BF16 = jnp.bfloat16
U32 = jnp.uint32
I32 = jnp.int32

HEAD_DIM = 128
N_HEADS_A = 8
N_HEADS_B = 4
DILATED_PATTERNS = ((128, 1), (512, 4), (2048, 16))
N_EXPERTS = 32
TOP_K = 4
SWIGLU_LIMIT = 7.0
SWIGLU_ALPHA = 1.702
N_MOD = 6
EPS = 1e-5
NEG = -1e30
LOG2E = math.log2(math.e)

LANES = 128
SUBLANES = 8
V7X_VMEM_LIMIT = 56 * 1024 * 1024

ROW_BLOCK = 128
CHUNK_BLOCKS = 9
MM_ROW_BLOCKS = 3
FF_TILE = 512


def _alibi_slopes(n):
    return np.array([2.0 ** (-8.0 * (i + 1) / n) for i in range(n)], dtype=np.float32)


def _nt_dot(a, b):
    return lax.dot_general(a, b, (((1,), (1,)), ((), ())), preferred_element_type=F32)


def _split_bf16(x):
    hi = x.astype(BF16)
    lo = (x - hi.astype(F32)).astype(BF16)
    return hi, lo


def _params(sem, vmem=V7X_VMEM_LIMIT):
    return pltpu.CompilerParams(dimension_semantics=sem, vmem_limit_bytes=vmem)


def _ada_kernel(c_ref, w_ref, b_ref, o_ref):
    c = c_ref[...]
    a = c * jax.nn.sigmoid(c)
    a_hi, a_lo = _split_bf16(a)
    w_hi, w_lo = _split_bf16(w_ref[...])
    acc = jnp.dot(a_hi, w_hi, preferred_element_type=F32)
    acc += jnp.dot(a_lo, w_hi, preferred_element_type=F32)
    acc += jnp.dot(a_hi, w_lo, preferred_element_type=F32)
    o_ref[...] = acc + b_ref[...]


def _ada(c8, w, b, tn=1024):
    m, d = c8.shape
    n = w.shape[1]
    return pl.pallas_call(
        _ada_kernel,
        grid=(n // tn,),
        in_specs=[pl.BlockSpec((m, d), lambda j: (0, 0)),
                  pl.BlockSpec((d, tn), lambda j: (0, j)),
                  pl.BlockSpec((1, tn), lambda j: (0, j))],
        out_specs=pl.BlockSpec((m, tn), lambda j: (0, j)),
        out_shape=jax.ShapeDtypeStruct((m, n), F32),
        compiler_params=_params(("arbitrary",)),
        name="ada",
    )(c8, w, b)


def _norm_mod_kernel(x_ref, g_ref, sc_ref, sh_ref, o_ref):
    x = x_ref[...]
    y = x * lax.rsqrt(jnp.mean(x * x, axis=-1, keepdims=True) + EPS) * g_ref[...]
    o_ref[...] = (y * (1.0 + sc_ref[...]) + sh_ref[...]).astype(o_ref.dtype)


def _norm_mod(x2, g, mod3, i_scale, i_shift, seq, tm=512):
    t, d = x2.shape
    per_b = seq // tm
    return pl.pallas_call(
        _norm_mod_kernel,
        grid=(t // tm,),
        in_specs=[pl.BlockSpec((tm, d), lambda i: (i, 0)),
                  pl.BlockSpec((1, d), lambda i: (0, 0)),
                  pl.BlockSpec((None, 1, d), lambda i: ((i // per_b) * N_MOD + i_scale, 0, 0)),
                  pl.BlockSpec((None, 1, d), lambda i: ((i // per_b) * N_MOD + i_shift, 0, 0))],
        out_specs=pl.BlockSpec((tm, d), lambda i: (i, 0)),
        out_shape=jax.ShapeDtypeStruct((t, d), BF16),
        compiler_params=_params(("arbitrary",)),
        name="norm1",
    )(x2, g, mod3, mod3)


def _inproj_kernel(u_ref, w_ref, o_ref, wbf_ref, *, q_tiles, scale):
    n = pl.program_id(0)

    @pl.when(pl.program_id(1) == 0)
    def _():
        wbf_ref[...] = w_ref[...].astype(BF16)

    acc = jnp.dot(u_ref[...], wbf_ref[...], preferred_element_type=F32)
    is_q = functools.reduce(jnp.logical_or, [n == q for q in q_tiles])
    o_ref[...] = (acc * jnp.where(is_q, scale, 1.0)).astype(BF16)


def _inproj(u, w, q_tiles, tm=1024, tn=1024):
    t, d = u.shape
    n = w.shape[1]
    return pl.pallas_call(
        functools.partial(_inproj_kernel, q_tiles=q_tiles, scale=HEAD_DIM ** -0.5 * LOG2E),
        grid=(n // tn, t // tm),
        in_specs=[pl.BlockSpec((tm, d), lambda j, i: (i, 0)),
                  pl.BlockSpec((d, tn), lambda j, i: (0, j))],
        out_specs=pl.BlockSpec((tm, tn), lambda j, i: (i, j)),
        out_shape=jax.ShapeDtypeStruct((t, n), BF16),
        scratch_shapes=[pltpu.VMEM((d, tn), BF16)],
        compiler_params=_params(("arbitrary", "arbitrary")),
        name="inproj",
    )(u, w)


def _dil_tables(tq, span):
    nside = span // tq
    o = lax.broadcasted_iota(I32, (2 * nside + 1, tq, tq), 0) - nside
    i = lax.broadcasted_iota(I32, (2 * nside + 1, tq, tq), 1)
    j = lax.broadcasted_iota(I32, (2 * nside + 1, tq, tq), 2)
    ad = jnp.abs(o * tq + j - i)
    mult = jnp.zeros_like(ad)
    for window, dil in DILATED_PATTERNS:
        mult += ((ad % dil == 0) & (ad // dil <= window // (2 * dil))).astype(I32)
    lm = jnp.where(mult > 0, jnp.log2(jnp.maximum(mult, 1).astype(F32)), NEG)
    return ad.astype(F32) * LOG2E, lm


def _dil_kernel(slopes_ref, q_ref, k_ref, v_ref, ad_ref, lm_ref, o_ref, bias_ref, vt_ref, st_ref,
                *, tq, nside, nq):
    h = pl.program_id(1)
    i = pl.program_id(2)

    @pl.when(i == 0)
    def _():
        bias_ref[:2 * nside + 1] = lm_ref[...] - slopes_ref[h] * ad_ref[...]
        bias_ref[2 * nside + 1] = jnp.full((tq, tq), NEG, F32)
        for kb in range(nq):
            vt_ref[kb] = v_ref[kb * tq:(kb + 1) * tq, :].astype(F32).T.astype(BF16)

    nwin = 2 * nside + 1
    b0 = jnp.clip(i - nside, 0, nq - nwin)
    q = q_ref[...]

    def score_block(j, m):
        o = b0 + j - i
        plane = jnp.where(jnp.abs(o) <= nside, nside - o, nwin)
        kb = k_ref[pl.ds(pl.multiple_of((b0 + j) * tq, tq), tq), :]
        s = _nt_dot(kb, q) + bias_ref[plane]
        st_ref[j] = s
        return jnp.maximum(m, jnp.max(s, axis=0, keepdims=True))

    m = lax.fori_loop(0, nwin, score_block, jnp.full((1, tq), NEG, F32), unroll=True)
    l = acc = None
    for j in range(nwin):
        p = jnp.exp2(st_ref[j] - m)
        cl = jnp.sum(p, axis=0, keepdims=True)
        ca = jnp.dot(vt_ref[b0 + j], p.astype(BF16), preferred_element_type=F32)
        l = cl if j == 0 else l + cl
        acc = ca if j == 0 else acc + ca
    o_ref[...] = (acc / l).T.astype(o_ref.dtype)


def _dil_attention(proj, slopes, batch, seq, q_col, k_col, v_col, tq=512):
    span = max(w // 2 for w, _ in DILATED_PATTERNS)
    nside = span // tq
    nq = seq // tq
    ad, lm = _dil_tables(tq, span)
    tab_spec = pl.BlockSpec((2 * nside + 1, tq, tq), lambda b, h, i, s: (0, 0, 0))
    kern = functools.partial(_dil_kernel, tq=tq, nside=nside, nq=nq)
    return pl.pallas_call(
        kern,
        grid_spec=pltpu.PrefetchScalarGridSpec(
            num_scalar_prefetch=1,
            grid=(batch, N_HEADS_A, nq),
            in_specs=[pl.BlockSpec((tq, HEAD_DIM), lambda b, h, i, s: (b * nq + i, q_col + h)),
                      pl.BlockSpec((seq, HEAD_DIM), lambda b, h, i, s: (b, k_col + h)),
                      pl.BlockSpec((seq, HEAD_DIM), lambda b, h, i, s: (b, v_col + h)),
                      tab_spec, tab_spec],
            out_specs=pl.BlockSpec((tq, HEAD_DIM), lambda b, h, i, s: (b * nq + i, h)),
            scratch_shapes=[pltpu.VMEM((2 * nside + 2, tq, tq), F32),
                            pltpu.VMEM((nq, HEAD_DIM, tq), BF16),
                            pltpu.VMEM((2 * nside + 1, tq, tq), F32)]),
        out_shape=jax.ShapeDtypeStruct((batch * seq, N_HEADS_A * HEAD_DIM), BF16),
        compiler_params=_params(("arbitrary", "arbitrary", "arbitrary")),
        name="dil_attn",
    )(slopes, proj, proj, proj, ad, lm)


def _diff_kernel(slopes_ref, lq1_ref, lk1_ref, lq2_ref, lk2_ref, g_ref, q_ref, k_ref, v_ref, o_ref,
                 vt_ref, bias_ref, st1_ref, st2_ref, *, tq, tk, nk, lambda_init):
    h = pl.program_id(1)
    i = pl.program_id(2)

    seq = k_ref.shape[0]
    nq = seq // tq

    @pl.when(i == 0)
    def _():
        for kb in range(nk):
            cols = slice(kb * tk, (kb + 1) * tk)
            vt_ref[:, cols] = v_ref[cols, :].astype(F32).T.astype(BF16)

        neg_c = -slopes_ref[h] * LOG2E
        base = (lax.broadcasted_iota(I32, (tq, tq), 0) - lax.broadcasted_iota(I32, (tq, tq), 1)
                - (seq - tq))

        def fill(r, carry):
            rows = pl.ds(pl.multiple_of(r * tq, tq), tq)
            bias_ref[rows, :] = jnp.abs(base + r * tq).astype(F32) * neg_c
            return carry

        lax.fori_loop(0, 2 * nq - 1, fill, 0)

    q = q_ref[...]
    qs = (q[:, :HEAD_DIM], q[:, HEAD_DIM:])
    st_refs = (st1_ref, st2_ref)
    w0 = (nq - 1 - i) * tq
    def score_chunk(c, mx):
        rows = pl.ds(pl.multiple_of(c * tk, tk), tk)
        kc = k_ref[rows, :]
        b = bias_ref[pl.ds(pl.multiple_of(w0 + c * tk, tq), tk), :]
        out = []
        for j in range(2):
            s = _nt_dot(kc[:, j * HEAD_DIM:(j + 1) * HEAD_DIM], qs[j]) + b
            st_refs[j][rows, :] = s
            out.append(jnp.maximum(mx[j], jnp.max(s, axis=0, keepdims=True)))
        return tuple(out)

    neg = jnp.full((1, tq), NEG, F32)
    mx = lax.fori_loop(0, nk, score_chunk, (neg, neg), unroll=4)
    ls = [None, None]
    accs = [None, None]
    for c in range(nk):
        rows = slice(c * tk, (c + 1) * tk)
        vt = vt_ref[:, rows]
        for j in range(2):
            p = jnp.exp2(st_refs[j][rows, :] - mx[j])
            cl = jnp.sum(p, axis=0, keepdims=True)
            ca = jnp.dot(vt, p.astype(BF16), preferred_element_type=F32)
            ls[j] = cl if c == 0 else ls[j] + cl
            accs[j] = ca if c == 0 else accs[j] + ca
    (a1, a2), (l1, l2) = accs, ls

    lam = (jnp.exp(jnp.sum(lq1_ref[...] * lk1_ref[...], axis=-1, keepdims=True))
           - jnp.exp(jnp.sum(lq2_ref[...] * lk2_ref[...], axis=-1, keepdims=True)) + lambda_init)
    ot = a1 / l1 - lam * (a2 / l2)
    yt = ot * lax.rsqrt(jnp.mean(ot * ot, axis=0, keepdims=True) + EPS)
    o_ref[...] = (yt.T * (g_ref[...] * (1.0 - lambda_init))).astype(o_ref.dtype)


def _diff_attention(proj, slopes, lam_vecs, subln_g, lambda_init, batch, seq, q_col, k_col, v_col,
                    tq=512, tk=512):
    nq = seq // tq
    w = 2 * HEAD_DIM
    vec_spec = pl.BlockSpec((1, HEAD_DIM), lambda b, h, i, s: (0, 0))
    kern = functools.partial(_diff_kernel, tq=tq, tk=tk, nk=seq // tk, lambda_init=lambda_init)
    return pl.pallas_call(
        kern,
        grid_spec=pltpu.PrefetchScalarGridSpec(
            num_scalar_prefetch=1,
            grid=(batch, N_HEADS_B, nq),
            in_specs=[vec_spec, vec_spec, vec_spec, vec_spec,
                      pl.BlockSpec((1, w), lambda b, h, i, s: (0, 0)),
                      pl.BlockSpec((tq, w), lambda b, h, i, s: (b * nq + i, q_col + h)),
                      pl.BlockSpec((seq, w), lambda b, h, i, s: (b, k_col + h)),
                      pl.BlockSpec((seq, w), lambda b, h, i, s: (b, v_col + h))],
            out_specs=pl.BlockSpec((tq, w), lambda b, h, i, s: (b * nq + i, h)),
            scratch_shapes=[pltpu.VMEM((w, seq), BF16),
                            pltpu.VMEM((2 * seq - tq, tq), F32),
                            pltpu.VMEM((seq, tq), F32),
                            pltpu.VMEM((seq, tq), F32)]),
        out_shape=jax.ShapeDtypeStruct((batch * seq, N_HEADS_B * w), BF16),
        compiler_params=_params(("arbitrary", "arbitrary", "arbitrary")),
        name="diff_attn",
    )(slopes, *lam_vecs, subln_g, proj, proj, proj)


def _merge_kernel(ya_ref, yb_ref, ga_ref, gb_ref, x_ref, g1_ref, sc2_ref, sh2_ref, n2g_ref,
                  woa_ref, wob_ref, wo_ref, wrh_ref, wrl_ref, br_ref,
                  h_ref, u2p_ref, lg_ref):
    a = jnp.dot(ya_ref[...], woa_ref[...], preferred_element_type=F32)
    b = jnp.dot(yb_ref[...], wob_ref[...], preferred_element_type=F32)
    merged = (jax.nn.sigmoid(ga_ref[...].astype(F32)) * a
              + jax.nn.sigmoid(gb_ref[...].astype(F32)) * b)
    h = x_ref[...] + g1_ref[...] * jnp.dot(merged.astype(BF16), wo_ref[...],
                                           preferred_element_type=F32)
    h_ref[...] = h
    y = h * lax.rsqrt(jnp.mean(h * h, axis=-1, keepdims=True) + EPS) * n2g_ref[...]
    u2 = y * (1.0 + sc2_ref[...]) + sh2_ref[...]
    hi, lo = _split_bf16(u2)
    lg = _nt_dot(wrh_ref[...], hi) + _nt_dot(wrl_ref[...], hi) + _nt_dot(wrh_ref[...], lo)
    lg_ref[...] = lg + br_ref[...]
    bits = pltpu.bitcast(hi.astype(F32), U32)
    half = bits.shape[1] // 2
    u2p_ref[...] = (bits[:, :half] >> 16) | (bits[:, half:] & jnp.uint32(0xFFFF0000))


def _merge(ya, yb, proj, x2, mod3, n2g, woa, wob, wo, wr_hi, wr_lo, br, seq, ga_col, gb_col, tm=256):
    t, d = x2.shape
    per_b = seq // tm
    wa = ya.shape[1]
    wb = yb.shape[1]
    ne = wr_hi.shape[0]

    def mod_spec(idx):
        return pl.BlockSpec((None, 1, d), lambda i: ((i // per_b) * N_MOD + idx, 0, 0))

    def const_spec(shape):
        return pl.BlockSpec(shape, lambda i: (0,) * len(shape), pipeline_mode=pl.Buffered(1))

    return pl.pallas_call(
        _merge_kernel,
        grid=(t // tm,),
        in_specs=[pl.BlockSpec((tm, wa), lambda i: (i, 0)),
                  pl.BlockSpec((tm, wb), lambda i: (i, 0)),
                  pl.BlockSpec((tm, d), lambda i: (i, ga_col)),
                  pl.BlockSpec((tm, d), lambda i: (i, gb_col)),
                  pl.BlockSpec((tm, d), lambda i: (i, 0)),
                  mod_spec(2), mod_spec(4), mod_spec(3),
                  const_spec((1, d)),
                  const_spec((wa, d)), const_spec((wb, d)), const_spec((d, d)),
                  const_spec((ne, d)), const_spec((ne, d)), const_spec((ne, 1))],
        out_specs=[pl.BlockSpec((tm, d), lambda i: (i, 0)),
                   pl.BlockSpec((tm, d // 2), lambda i: (i, 0)),
                   pl.BlockSpec((ne, tm), lambda i: (0, i))],
        out_shape=[jax.ShapeDtypeStruct((t, d), F32),
                   jax.ShapeDtypeStruct((t, d // 2), U32),
                   jax.ShapeDtypeStruct((ne, t), F32)],
        compiler_params=_params(("arbitrary",)),
        name="merge",
    )(ya, yb, proj, proj, x2, mod3, mod3, mod3, n2g, woa, wob, wo, wr_hi, wr_lo, br)


def _route_kernel(lg_ref, p_ref, dest_ref, cnt_ref, carry_ref, total_ref):
    sweep = pl.program_id(0)

    @pl.when(pl.program_id(1) == 0)
    def _():
        @pl.when(sweep == 0)
        def _():
            total_ref[...] = jnp.zeros_like(total_ref)

        @pl.when(sweep == 1)
        def _():
            total_ref[...] = carry_ref[...]

        carry_ref[...] = jnp.zeros_like(carry_ref)

    lg = lg_ref[...]
    ne, tr = lg.shape
    eio = lax.broadcasted_iota(I32, (ne, tr), 0)
    work = lg
    vals, hots = [], []
    for k in range(TOP_K):
        mx = jnp.max(work, axis=0, keepdims=True)
        am = jnp.min(jnp.where(work == mx, eio, ne), axis=0, keepdims=True)
        hot = eio == am
        vals.append(mx)
        hots.append(hot)
        work = jnp.where(hot, -jnp.inf, work)
    exps = [jnp.exp(v - vals[0]) for v in vals]
    denom = functools.reduce(jnp.add, exps)
    for k in range(TOP_K):
        p_ref[k:k + 1, :] = exps[k] / denom
    chosen = functools.reduce(jnp.logical_or, hots)
    sel = jnp.where(chosen, 1.0, 0.0)
    tri = (lax.broadcasted_iota(I32, (tr, tr), 0) < lax.broadcasted_iota(I32, (tr, tr), 1))
    before = jnp.dot(sel.astype(BF16), jnp.where(tri, 1.0, 0.0).astype(BF16),
                     preferred_element_type=F32)
    carry = carry_ref[...]
    nblk = jnp.floor((total_ref[...][:, 0:1] + (ROW_BLOCK - 1)) * (1.0 / ROW_BLOCK))
    e_row = lax.broadcasted_iota(I32, (ne, ne), 0)
    e_col = lax.broadcasted_iota(I32, (ne, ne), 1)
    nblk_lanes = jnp.sum(jnp.where(e_row == e_col, nblk, 0.0), axis=0, keepdims=True)
    first_row = jnp.sum(jnp.where(e_col < e_row, nblk_lanes, 0.0), axis=1,
                        keepdims=True) * ROW_BLOCK
    place = before + (carry[:, 0:1] + first_row)
    for k in range(TOP_K):
        dest_ref[k:k + 1, :] = jnp.sum(jnp.where(hots[k], place, 0.0), axis=0,
                                       keepdims=True).astype(I32)
    carry = carry + jnp.sum(sel, axis=1, keepdims=True)
    carry_ref[...] = carry
    cnt_ref[...] = carry.astype(I32)


def _route(logits_t, tr=512):
    ne, t = logits_t.shape
    slot_spec = pl.BlockSpec((None, TOP_K, tr), lambda s, i: (s, 0, i))
    probs, dest, cnt = pl.pallas_call(
        _route_kernel,
        grid=(2, t // tr),
        in_specs=[pl.BlockSpec((ne, tr), lambda s, i: (0, i))],
        out_specs=[slot_spec, slot_spec,
                   pl.BlockSpec((None, ne, LANES), lambda s, i: (s, 0, 0))],
        out_shape=[jax.ShapeDtypeStruct((2, TOP_K, t), F32),
                   jax.ShapeDtypeStruct((2, TOP_K, t), I32),
                   jax.ShapeDtypeStruct((2, ne, LANES), I32)],
        scratch_shapes=[pltpu.VMEM((ne, LANES), F32), pltpu.VMEM((ne, LANES), F32)],
        compiler_params=_params(("arbitrary", "arbitrary")),
        name="route",
    )(logits_t)
    return probs[1], dest[1], cnt[1]


def _dispatch_kernel(dest_ref, fill_ref, nfill_ref, src_ref, dst_ref, sem, fill_sem, *, tm, t_total):
    i = pl.program_id(0)

    @pl.when(i == 0)
    def _():
        def fill(b):
            rows = pl.ds(pl.multiple_of(fill_ref[b] * ROW_BLOCK, ROW_BLOCK), ROW_BLOCK)
            return pltpu.make_async_copy(src_ref.at[pl.ds(0, ROW_BLOCK)], dst_ref.at[rows], fill_sem)

        def fill_start(b, c):
            fill(b).start()
            return c

        def fill_wait(b, c):
            fill(b).wait()
            return c

        lax.fori_loop(0, nfill_ref[0], fill_start, 0)
        lax.fori_loop(0, nfill_ref[0], fill_wait, 0)

    for k in range(TOP_K):
        def group(g, c):
            j0 = pl.multiple_of(g * SUBLANES, SUBLANES)
            tile = src_ref.at[pl.ds(j0, SUBLANES)]
            for u in range(SUBLANES):
                row = dest_ref[k * t_total + i * tm + j0 + u]
                pltpu.make_async_copy(tile.at[pl.ds(u, 1)], dst_ref.at[pl.ds(row, 1)], sem).start()
            return c
        lax.fori_loop(0, tm // SUBLANES, group, 0)
    for k in range(TOP_K):
        pltpu.make_async_copy(src_ref, dst_ref.at[pl.ds(0, tm)], sem).wait()


def _dispatch(dest_flat, fill_blocks, nfill, u2p, total_rows, tm=512):
    t, w = u2p.shape
    kern = functools.partial(_dispatch_kernel, tm=tm, t_total=t)
    return pl.pallas_call(
        kern,
        grid_spec=pltpu.PrefetchScalarGridSpec(
            num_scalar_prefetch=3,
            grid=(t // tm,),
            in_specs=[pl.BlockSpec((tm, w), lambda i, d, fb, nf: (i, 0))],
            out_specs=pl.BlockSpec(memory_space=pl.ANY),
            scratch_shapes=[pltpu.SemaphoreType.DMA(()), pltpu.SemaphoreType.DMA(())]),
        out_shape=jax.ShapeDtypeStruct((total_rows, w), U32),
        compiler_params=_params(("arbitrary",)),
        name="dispatch",
    )(dest_flat, fill_blocks, nfill, u2p)


def _moe_kernel(che_ref, chblk_ref, chn_ref, nvalid_ref,
                xs_ref, wg_ref, wu_ref, wd_ref, bg_ref, bu_ref, bd_ref,
                ys_ref,
                xin_ref, xbf_ref, yacc_ref, act_ref, wgb_ref, wub_ref, wdb_ref, in_sem, out_sem,
                *, nff, nsplit, mm_rows):
    c = pl.program_id(0)
    f = pl.program_id(1)
    nvalid = nvalid_ref[0]
    half = xin_ref.shape[1]

    def in_copy(cc, j):
        row = pl.multiple_of((chblk_ref[cc] + j) * ROW_BLOCK, ROW_BLOCK)
        return pltpu.make_async_copy(xs_ref.at[pl.ds(row, ROW_BLOCK)],
                                     xin_ref.at[pl.ds(j * ROW_BLOCK, ROW_BLOCK)], in_sem)

    def out_copy(cc, j):
        row = pl.multiple_of((chblk_ref[cc] + j) * ROW_BLOCK, ROW_BLOCK)
        return pltpu.make_async_copy(yacc_ref.at[pl.ds(j * ROW_BLOCK, ROW_BLOCK)],
                                     ys_ref.at[pl.ds(row, ROW_BLOCK)], out_sem)

    def for_blocks(cc, fn):
        def one(j, carry):
            fn(cc, j)
            return carry
        lax.fori_loop(0, chn_ref[cc], one, 0)

    def chunk_step():
        @pl.when(f == 0)
        def _():
            @pl.when(c == 0)
            def _():
                xbf_ref[...] = jnp.zeros_like(xbf_ref)
                for_blocks(c, lambda cc, j: in_copy(cc, j).start())

            for_blocks(c, lambda cc, j: in_copy(cc, j).wait())

            def unpack(cc, j):
                rows = pl.ds(pl.multiple_of(j * ROW_BLOCK, ROW_BLOCK), ROW_BLOCK)
                w = xin_ref[rows, :]
                xbf_ref[rows, :half] = pltpu.bitcast(w << 16, F32).astype(BF16)
                xbf_ref[rows, half:] = pltpu.bitcast(w & jnp.uint32(0xFFFF0000), F32).astype(BF16)

            for_blocks(c, unpack)

            @pl.when(c + 1 < nvalid)
            def _():
                for_blocks(c + 1, lambda cc, j: in_copy(cc, j).start())

        mm_blocks = mm_rows // ROW_BLOCK
        n_mm = (chn_ref[c] + (mm_blocks - 1)) // mm_blocks

        def gate_up(rows, wg, wu):
            x = xbf_ref[rows, :]
            g = jnp.minimum(jnp.dot(x, wg, preferred_element_type=F32) + bg_ref[...], SWIGLU_LIMIT)
            u = jnp.clip(jnp.dot(x, wu, preferred_element_type=F32) + bu_ref[...],
                         -SWIGLU_LIMIT, SWIGLU_LIMIT)
            act_ref[rows, :] = ((u + 1.0) * (g * jax.nn.sigmoid(SWIGLU_ALPHA * g))).astype(BF16)

        wg = wg_ref[...].astype(BF16)
        wu = wu_ref[...].astype(BF16)
        wgb_ref[...] = wg
        wub_ref[...] = wu
        gate_up(slice(0, mm_rows), wg, wu)
        wdb_ref[...] = wd_ref[...].astype(BF16)

        def gate_up_body(rb, carry):
            gate_up(pl.ds(pl.multiple_of(rb * mm_rows, mm_rows), mm_rows), wgb_ref[...], wub_ref[...])
            return carry

        lax.fori_loop(1, n_mm, gate_up_body, 0)

        @pl.when((f == 0) & (c > 0))
        def _():
            for_blocks(c - 1, lambda cc, j: out_copy(cc, j).wait())

        first = f == 0
        ncol = yacc_ref.shape[1] // nsplit

        def down(rb, carry):
            rows = pl.ds(pl.multiple_of(rb * mm_rows, mm_rows), mm_rows)
            a = act_ref[rows, :]
            for s in range(nsplit):
                cols = slice(s * ncol, (s + 1) * ncol)
                part = jnp.dot(a, wdb_ref[:, cols], preferred_element_type=F32)
                base = jnp.where(first, jnp.broadcast_to(bd_ref[:, cols], part.shape),
                                 yacc_ref[rows, cols])
                yacc_ref[rows, cols] = base + part
            return carry

        lax.fori_loop(0, n_mm, down, 0)

        @pl.when(f == nff - 1)
        def _():
            for_blocks(c, lambda cc, j: out_copy(cc, j).start())

            @pl.when(c == nvalid - 1)
            def _():
                for_blocks(c, lambda cc, j: out_copy(cc, j).wait())

                def fill(b):
                    rows = pl.ds(pl.multiple_of(b * ROW_BLOCK, ROW_BLOCK), ROW_BLOCK)
                    return pltpu.make_async_copy(yacc_ref.at[pl.ds(0, ROW_BLOCK)], ys_ref.at[rows],
                                                 out_sem)

                def fill_start(b, carry):
                    fill(b).start()
                    return carry

                def fill_wait(b, carry):
                    fill(b).wait()
                    return carry

                nused = chblk_ref[c] + chn_ref[c]
                ntotal = ys_ref.shape[0] // ROW_BLOCK
                lax.fori_loop(nused, ntotal, fill_start, 0)
                lax.fori_loop(nused, ntotal, fill_wait, 0)

    chunk_step()


def _moe(che, chblk, chn, nvalid, xs, wg, wu, wd, bg, bu, bd):
    ne, d, dff = wg.shape
    rows = CHUNK_BLOCKS * ROW_BLOCK
    nff = dff // FF_TILE

    kern = functools.partial(_moe_kernel, nff=nff, nsplit=4, mm_rows=MM_ROW_BLOCKS * ROW_BLOCK)
    return pl.pallas_call(
        kern,
        grid_spec=pltpu.PrefetchScalarGridSpec(
            num_scalar_prefetch=4,
            grid=(nvalid[0], nff),
            in_specs=[pl.BlockSpec(memory_space=pl.ANY),
                      pl.BlockSpec((None, d, FF_TILE), lambda c, f, e, b, n, nv: (e[c], 0, f)),
                      pl.BlockSpec((None, d, FF_TILE), lambda c, f, e, b, n, nv: (e[c], 0, f)),
                      pl.BlockSpec((None, FF_TILE, d), lambda c, f, e, b, n, nv: (e[c], f, 0)),
                      pl.BlockSpec((None, 1, FF_TILE), lambda c, f, e, b, n, nv: (e[c], 0, f)),
                      pl.BlockSpec((None, 1, FF_TILE), lambda c, f, e, b, n, nv: (e[c], 0, f)),
                      pl.BlockSpec((None, 1, d), lambda c, f, e, b, n, nv: (e[c], 0, 0))],
            out_specs=pl.BlockSpec(memory_space=pl.ANY),
            scratch_shapes=[pltpu.VMEM((rows, d // 2), U32),
                            pltpu.VMEM((rows, d), BF16),
                            pltpu.VMEM((rows, d), F32),
                            pltpu.VMEM((rows, FF_TILE), BF16),
                            pltpu.VMEM((d, FF_TILE), BF16),
                            pltpu.VMEM((d, FF_TILE), BF16),
                            pltpu.VMEM((FF_TILE, d), BF16),
                            pltpu.SemaphoreType.DMA(()),
                            pltpu.SemaphoreType.DMA(())]),
        out_shape=jax.ShapeDtypeStruct((xs.shape[0], d), F32),
        compiler_params=_params(("arbitrary", "arbitrary")),
        name="moe",
    )(che, chblk, chn, nvalid, xs, wg, wu, wd, bg, bu, bd)


def _combine_kernel(dest_ref, ys_ref, h_ref, p_ref, g2_ref, fg_ref, o_ref, buf_ref, sem,
                    *, tc, nsteps, final_norm):
    i = pl.program_id(0)
    t_total = nsteps * tc

    def issue(step, slot):
        for k in range(TOP_K):
            def group(g, c):
                j0 = pl.multiple_of(g * SUBLANES, SUBLANES)
                tile = buf_ref.at[slot, k, pl.ds(j0, SUBLANES)]
                for u in range(SUBLANES):
                    src = dest_ref[k * t_total + step * tc + j0 + u]
                    pltpu.make_async_copy(ys_ref.at[pl.ds(src, 1)], tile.at[pl.ds(u, 1)],
                                          sem.at[slot]).start()
                return c
            lax.fori_loop(0, tc // SUBLANES, group, 0)

    def drain(slot):
        for k in range(TOP_K):
            pltpu.make_async_copy(ys_ref.at[pl.ds(0, tc)], buf_ref.at[slot, k], sem.at[slot]).wait()

    slot = i % 2

    @pl.when(i == 0)
    def _():
        issue(0, 0)

    @pl.when(i + 1 < nsteps)
    def _():
        issue(i + 1, 1 - slot)

    drain(slot)
    p = p_ref[...]
    moe = p[:, 0:1] * buf_ref[slot, 0]
    for k in range(1, TOP_K):
        moe += p[:, k:k + 1] * buf_ref[slot, k]
    h = h_ref[...] + g2_ref[...] * moe
    if final_norm:
        h = h * lax.rsqrt(jnp.mean(h * h, axis=-1, keepdims=True) + EPS) * fg_ref[...]
    o_ref[...] = h


def _combine(dest_flat, ys, h, probs_t, mod3, final_g, seq, final_norm, tc=256):
    t, d = h.shape
    per_b = seq // tc
    nsteps = t // tc
    kern = functools.partial(_combine_kernel, tc=tc, nsteps=nsteps, final_norm=final_norm)
    return pl.pallas_call(
        kern,
        grid_spec=pltpu.PrefetchScalarGridSpec(
            num_scalar_prefetch=1,
            grid=(nsteps,),
            in_specs=[pl.BlockSpec(memory_space=pl.ANY),
                      pl.BlockSpec((tc, d), lambda i, s: (i, 0)),
                      pl.BlockSpec((tc, TOP_K), lambda i, s: (i, 0)),
                      pl.BlockSpec((None, 1, d), lambda i, s: ((i // per_b) * N_MOD + 5, 0, 0)),
                      pl.BlockSpec((1, d), lambda i, s: (0, 0))],
            out_specs=pl.BlockSpec((tc, d), lambda i, s: (i, 0)),
            scratch_shapes=[pltpu.VMEM((2, TOP_K, tc, d), F32),
                            pltpu.SemaphoreType.DMA((2,))]),
        out_shape=jax.ShapeDtypeStruct((t, d), F32),
        compiler_params=_params(("arbitrary",)),
        name="combine",
    )(dest_flat, ys, h, probs_t, mod3, final_g)


def _routing_plan(counts, t):
    nblk_total = -(-t * TOP_K // ROW_BLOCK) + N_EXPERTS
    nblk = (counts + ROW_BLOCK - 1) // ROW_BLOCK
    blk_end = jnp.cumsum(nblk)
    blk_start = blk_end - nblk
    ntail_max = nblk_total - t * TOP_K // ROW_BLOCK
    fill_blocks = jnp.concatenate([jnp.maximum(blk_end - 1, 0),
                                   jnp.minimum(blk_end[-1] + jnp.arange(ntail_max), nblk_total - 1)])
    nfill = N_EXPERTS + nblk_total - blk_end[-1]
    max_chunks = -(-nblk_total // CHUNK_BLOCKS) + N_EXPERTS
    nch = (nblk + CHUNK_BLOCKS - 1) // CHUNK_BLOCKS
    ch_end = jnp.cumsum(nch)
    ch_start = ch_end - nch
    nvalid = ch_end[-1]
    cid = jnp.minimum(jnp.arange(max_chunks, dtype=I32), nvalid - 1)
    che =jnp.clip(jnp.searchsorted(ch_end, cid, side='right'), 0, N_EXPERTS - 1).astype(I32)
    local = cid - ch_start[che]
    nch_e = jnp.maximum(nch[che], 1)
    n_mm = (nblk[che] + MM_ROW_BLOCKS - 1) // MM_ROW_BLOCKS
    size = n_mm // nch_e
    extra = n_mm % nch_e
    first_mm = local * size + jnp.minimum(local, extra)
    chblk = (blk_start[che] + first_mm * MM_ROW_BLOCKS).astype(I32)
    chn = jnp.minimum((size + (local < extra)) * MM_ROW_BLOCKS,
                      nblk[che] - first_mm * MM_ROW_BLOCKS).astype(I32)
    return (fill_blocks.astype(I32), nfill.astype(I32).reshape(1), che, chblk, chn,
            nvalid.astype(I32).reshape(1), nblk_total * ROW_BLOCK)


def kernel(x, c, w_ada, b_ada, norm1_g, w_in, lam_q1, lam_k1, lam_q2, lam_k2, subln_g, w_out_a, w_out_b, w_o, norm2_g, w_router, b_router, w_gate, b_gate, w_up, b_up, w_down, b_down, final_g):
    batch, seq, d = x.shape
    t = batch * seq
    depth = w_ada.shape[0]
    wa = N_HEADS_A * HEAD_DIM
    wb = N_HEADS_B * 2 * HEAD_DIM
    slopes = _alibi_slopes(N_HEADS_A + N_HEADS_B)
    slopes_a = jnp.asarray(slopes[:N_HEADS_A])
    slopes_b = jnp.asarray(slopes[N_HEADS_A:])
    c8 = jnp.pad(c, ((0, 8 - batch), (0, 0)))

    h = x.reshape(t, d)
    for l in range(depth):
        lambda_init = 0.8 - 0.6 * math.exp(-0.3 * l)
        mod = _ada(c8, w_ada[l], b_ada[l].reshape(1, -1))[:batch]
        mod3 = mod.reshape(batch * N_MOD, 1, d)

        u = _norm_mod(h, norm1_g[l].reshape(1, d), mod3, 1, 0, seq)
        proj = _inproj(u, w_in[l], q_tiles=(0, 3 * wa // 1024))
        y_a = _dil_attention(proj, slopes_a, batch, seq, 0, wa // HEAD_DIM, 2 * wa // HEAD_DIM)
        off_b = 3 * wa // (2 * HEAD_DIM)
        nb = wb // (2 * HEAD_DIM)
        y_b = _diff_attention(proj, slopes_b,
                              [v[l].reshape(1, HEAD_DIM) for v in (lam_q1, lam_k1, lam_q2, lam_k2)],
                              subln_g[l].reshape(1, -1), lambda_init, batch, seq,
                              off_b, off_b + nb, off_b + 2 * nb)
        wr_hi, wr_lo = _split_bf16(w_router[l].T)
        gate_col = (3 * wa + 3 * wb) // d
        h, u2p, logits_t = _merge(y_a, y_b, proj, h, mod3, norm2_g[l].reshape(1, d),
                                  w_out_a[l].astype(BF16), w_out_b[l].astype(BF16),
                                  w_o[l].astype(BF16), wr_hi, wr_lo, b_router[l].reshape(-1, 1),
                                  seq, gate_col, gate_col + 1)
        probs, dest, cnt = _route(logits_t)
        (fill_blocks, nfill, che, chblk, chn, nvalid, total_rows) = _routing_plan(cnt[:, 0], t)
        xs = _dispatch(dest.reshape(-1), fill_blocks, nfill, u2p, total_rows)
        ys = _moe(che, chblk, chn, nvalid, xs, w_gate[l], w_up[l], w_down[l],
                  b_gate[l].reshape(N_EXPERTS, 1, -1), b_up[l].reshape(N_EXPERTS, 1, -1),
                  b_down[l].reshape(N_EXPERTS, 1, -1))
        h = _combine(dest.reshape(-1), ys, h, probs.T, mod3, final_g.reshape(1, d), seq,
                     final_norm=(l == depth - 1))
    return h.reshape(batch, seq, d)
```

```python
import functools
import math

import numpy as np
import jax
import jax.numpy as jnp
from jax import lax
from jax.experimental import pallas as pl
from jax.experimental.pallas import tpu as pltpu

F32 = jnp.float32
BF16 = jnp.bfloat16
U32 = jnp.uint32
I32 = jnp.int32

HEAD_DIM = 128
N_HEADS_A = 8
N_HEADS_B = 4
DILATED_PATTERNS = ((128, 1), (512, 4), (2048, 16))
N_EXPERTS = 32
TOP_K = 4
SWIGLU_LIMIT = 7.0
SWIGLU_ALPHA = 1.702
N_MOD = 6
EPS = 1e-5
NEG = -1e30
LOG2E = math.log2(math.e)

LANES = 128
SUBLANES = 8
V7X_VMEM_LIMIT = 56 * 1024 * 1024

ROW_BLOCK = 128
CHUNK_BLOCKS = 9
MM_ROW_BLOCKS = 3
FF_TILE = 512


def _alibi_slopes(n):
    return np.array([2.0 ** (-8.0 * (i + 1) / n) for i in range(n)], dtype=np.float32)


def _nt_dot(a, b):
    return lax.dot_general(a, b, (((1,), (1,)), ((), ())), preferred_element_type=F32)


def _split_bf16(x):
    hi = x.astype(BF16)
    lo = (x - hi.astype(F32)).astype(BF16)
    return hi, lo


def _params(sem, vmem=V7X_VMEM_LIMIT):
    return pltpu.CompilerParams(dimension_semantics=sem, vmem_limit_bytes=vmem)


def _ada_kernel(c_ref, w_ref, b_ref, o_ref):
    c = c_ref[...]
    a = c * jax.nn.sigmoid(c)
    a_hi, a_lo = _split_bf16(a)
    w_hi, w_lo = _split_bf16(w_ref[...])
    acc = jnp.dot(a_hi, w_hi, preferred_element_type=F32)
    acc += jnp.dot(a_lo, w_hi, preferred_element_type=F32)
    acc += jnp.dot(a_hi, w_lo, preferred_element_type=F32)
    o_ref[...] = acc + b_ref[...]


def _ada(c8, w, b, tn=1024):
    m, d = c8.shape
    n = w.shape[1]
    return pl.pallas_call(
        _ada_kernel,
        grid=(n // tn,),
        in_specs=[pl.BlockSpec((m, d), lambda j: (0, 0)),
                  pl.BlockSpec((d, tn), lambda j: (0, j)),
                  pl.BlockSpec((1, tn), lambda j: (0, j))],
        out_specs=pl.BlockSpec((m, tn), lambda j: (0, j)),
        out_shape=jax.ShapeDtypeStruct((m, n), F32),
        compiler_params=_params(("arbitrary",)),
        name="ada",
    )(c8, w, b)


def _norm_mod_kernel(x_ref, g_ref, sc_ref, sh_ref, o_ref):
    x = x_ref[...]
    y = x * lax.rsqrt(jnp.mean(x * x, axis=-1, keepdims=True) + EPS) * g_ref[...]
    o_ref[...] = (y * (1.0 + sc_ref[...]) + sh_ref[...]).astype(o_ref.dtype)


def _norm_mod(x2, g, mod3, i_scale, i_shift, seq, tm=512):
    t, d = x2.shape
    per_b = seq // tm
    return pl.pallas_call(
        _norm_mod_kernel,
        grid=(t // tm,),
        in_specs=[pl.BlockSpec((tm, d), lambda i: (i, 0)),
                  pl.BlockSpec((1, d), lambda i: (0, 0)),
                  pl.BlockSpec((None, 1, d), lambda i: ((i // per_b) * N_MOD + i_scale, 0, 0)),
                  pl.BlockSpec((None, 1, d), lambda i: ((i // per_b) * N_MOD + i_shift, 0, 0))],
        out_specs=pl.BlockSpec((tm, d), lambda i: (i, 0)),
        out_shape=jax.ShapeDtypeStruct((t, d), BF16),
        compiler_params=_params(("arbitrary",)),
        name="norm1",
    )(x2, g, mod3, mod3)


def _inproj_kernel(u_ref, w_ref, o_ref, wbf_ref, *, q_tiles, scale):
    n = pl.program_id(0)

    @pl.when(pl.program_id(1) == 0)
    def _():
        wbf_ref[...] = w_ref[...].astype(BF16)

    acc = jnp.dot(u_ref[...], wbf_ref[...], preferred_element_type=F32)
    is_q = functools.reduce(jnp.logical_or, [n == q for q in q_tiles])
    o_ref[...] = (acc * jnp.where(is_q, scale, 1.0)).astype(BF16)


def _inproj(u, w, q_tiles, tm=1024, tn=1024):
    t, d = u.shape
    n = w.shape[1]
    return pl.pallas_call(
        functools.partial(_inproj_kernel, q_tiles=q_tiles, scale=HEAD_DIM ** -0.5 * LOG2E),
        grid=(n // tn, t // tm),
        in_specs=[pl.BlockSpec((tm, d), lambda j, i: (i, 0)),
                  pl.BlockSpec((d, tn), lambda j, i: (0, j))],
        out_specs=pl.BlockSpec((tm, tn), lambda j, i: (i, j)),
        out_shape=jax.ShapeDtypeStruct((t, n), BF16),
        scratch_shapes=[pltpu.VMEM((d, tn), BF16)],
        compiler_params=_params(("arbitrary", "arbitrary")),
        name="inproj",
    )(u, w)


def _dil_tables(tq, span):
    nside = span // tq
    o = lax.broadcasted_iota(I32, (2 * nside + 1, tq, tq), 0) - nside
    i = lax.broadcasted_iota(I32, (2 * nside + 1, tq, tq), 1)
    j = lax.broadcasted_iota(I32, (2 * nside + 1, tq, tq), 2)
    ad = jnp.abs(o * tq + j - i)
    mult = jnp.zeros_like(ad)
    for window, dil in DILATED_PATTERNS:
        mult += ((ad % dil == 0) & (ad // dil <= window // (2 * dil))).astype(I32)
    lm = jnp.where(mult > 0, jnp.log2(jnp.maximum(mult, 1).astype(F32)), NEG)
    return ad.astype(F32) * LOG2E, lm


def _dil_kernel(slopes_ref, q_ref, k_ref, v_ref, ad_ref, lm_ref, o_ref, bias_ref, vt_ref, st_ref,
                *, tq, nside, nq):
    h = pl.program_id(1)
    i = pl.program_id(2)

    @pl.when(i == 0)
    def _():
        bias_ref[:2 * nside + 1] = lm_ref[...] - slopes_ref[h] * ad_ref[...]
        bias_ref[2 * nside + 1] = jnp.full((tq, tq), NEG, F32)
        for kb in range(nq):
            vt_ref[kb] = v_ref[kb * tq:(kb + 1) * tq, :].astype(F32).T.astype(BF16)

    nwin = 2 * nside + 1
    b0 = jnp.clip(i - nside, 0, nq - nwin)
    q = q_ref[...]

    def score_block(j, m):
        o = b0 + j - i
        plane = jnp.where(jnp.abs(o) <= nside, nside - o, nwin)
        kb = k_ref[pl.ds(pl.multiple_of((b0 + j) * tq, tq), tq), :]
        s = _nt_dot(kb, q) + bias_ref[plane]
        st_ref[j] = s
        return jnp.maximum(m, jnp.max(s, axis=0, keepdims=True))

    m = lax.fori_loop(0, nwin, score_block, jnp.full((1, tq), NEG, F32), unroll=True)
    l = acc = None
    for j in range(nwin):
        p = jnp.exp2(st_ref[j] - m)
        cl = jnp.sum(p, axis=0, keepdims=True)
        ca = jnp.dot(vt_ref[b0 + j], p.astype(BF16), preferred_element_type=F32)
        l = cl if j == 0 else l + cl
        acc = ca if j == 0 else acc + ca
    o_ref[...] = (acc / l).T.astype(o_ref.dtype)


def _dil_attention(proj, slopes, batch, seq, q_col, k_col, v_col, tq=512):
    span = max(w // 2 for w, _ in DILATED_PATTERNS)
    nside = span // tq
    nq = seq // tq
    ad, lm = _dil_tables(tq, span)
    tab_spec = pl.BlockSpec((2 * nside + 1, tq, tq), lambda b, h, i, s: (0, 0, 0))
    kern = functools.partial(_dil_kernel, tq=tq, nside=nside, nq=nq)
    return pl.pallas_call(
        kern,
        grid_spec=pltpu.PrefetchScalarGridSpec(
            num_scalar_prefetch=1,
            grid=(batch, N_HEADS_A, nq),
            in_specs=[pl.BlockSpec((tq, HEAD_DIM), lambda b, h, i, s: (b * nq + i, q_col + h)),
                      pl.BlockSpec((seq, HEAD_DIM), lambda b, h, i, s: (b, k_col + h)),
                      pl.BlockSpec((seq, HEAD_DIM), lambda b, h, i, s: (b, v_col + h)),
                      tab_spec, tab_spec],
            out_specs=pl.BlockSpec((tq, HEAD_DIM), lambda b, h, i, s: (b * nq + i, h)),
            scratch_shapes=[pltpu.VMEM((2 * nside + 2, tq, tq), F32),
                            pltpu.VMEM((nq, HEAD_DIM, tq), BF16),
                            pltpu.VMEM((2 * nside + 1, tq, tq), F32)]),
        out_shape=jax.ShapeDtypeStruct((batch * seq, N_HEADS_A * HEAD_DIM), BF16),
        compiler_params=_params(("arbitrary", "arbitrary", "arbitrary")),
        name="dil_attn",
    )(slopes, proj, proj, proj, ad, lm)


def _diff_kernel(slopes_ref, lq1_ref, lk1_ref, lq2_ref, lk2_ref, g_ref, q_ref, k_ref, v_ref, o_ref,
                 vt_ref, bias_ref, st1_ref, st2_ref, *, tq, tk, nk, lambda_init):
    h = pl.program_id(1)
    i = pl.program_id(2)

    seq = k_ref.shape[0]
    nq = seq // tq

    @pl.when(i == 0)
    def _():
        for kb in range(nk):
            cols = slice(kb * tk, (kb + 1) * tk)
            vt_ref[:, cols] = v_ref[cols, :].astype(F32).T.astype(BF16)

        neg_c = -slopes_ref[h] * LOG2E
        base = (lax.broadcasted_iota(I32, (tq, tq), 0) - lax.broadcasted_iota(I32, (tq, tq), 1)
                - (seq - tq))

        def fill(r, carry):
            rows = pl.ds(pl.multiple_of(r * tq, tq), tq)
            bias_ref[rows, :] = jnp.abs(base + r * tq).astype(F32) * neg_c
            return carry

        lax.fori_loop(0, 2 * nq - 1, fill, 0)

    q = q_ref[...]
    qs = (q[:, :HEAD_DIM], q[:, HEAD_DIM:])
    st_refs = (st1_ref, st2_ref)
    w0 = (nq - 1 - i) * tq
    def score_chunk(c, mx):
        rows = pl.ds(pl.multiple_of(c * tk, tk), tk)
        kc = k_ref[rows, :]
        b = bias_ref[pl.ds(pl.multiple_of(w0 + c * tk, tq), tk), :]
        out = []
        for j in range(2):
            s = _nt_dot(kc[:, j * HEAD_DIM:(j + 1) * HEAD_DIM], qs[j]) + b
            st_refs[j][rows, :] = s
            out.append(jnp.maximum(mx[j], jnp.max(s, axis=0, keepdims=True)))
        return tuple(out)

    neg = jnp.full((1, tq), NEG, F32)
    mx = lax.fori_loop(0, nk, score_chunk, (neg, neg), unroll=4)
    ls = [None, None]
    accs = [None, None]
    for c in range(nk):
        rows = slice(c * tk, (c + 1) * tk)
        vt = vt_ref[:, rows]
        for j in range(2):
            p = jnp.exp2(st_refs[j][rows, :] - mx[j])
            cl = jnp.sum(p, axis=0, keepdims=True)
            ca = jnp.dot(vt, p.astype(BF16), preferred_element_type=F32)
            ls[j] = cl if c == 0 else ls[j] + cl
            accs[j] = ca if c == 0 else accs[j] + ca
    (a1, a2), (l1, l2) = accs, ls

    lam = (jnp.exp(jnp.sum(lq1_ref[...] * lk1_ref[...], axis=-1, keepdims=True))
           - jnp.exp(jnp.sum(lq2_ref[...] * lk2_ref[...], axis=-1, keepdims=True)) + lambda_init)
    ot = a1 / l1 - lam * (a2 / l2)
    yt = ot * lax.rsqrt(jnp.mean(ot * ot, axis=0, keepdims=True) + EPS)
    o_ref[...] = (yt.T * (g_ref[...] * (1.0 - lambda_init))).astype(o_ref.dtype)


def _diff_attention(proj, slopes, lam_vecs, subln_g, lambda_init, batch, seq, q_col, k_col, v_col,
                    tq=512, tk=512):
    nq = seq // tq
    w = 2 * HEAD_DIM
    vec_spec = pl.BlockSpec((1, HEAD_DIM), lambda b, h, i, s: (0, 0))
    kern = functools.partial(_diff_kernel, tq=tq, tk=tk, nk=seq // tk, lambda_init=lambda_init)
    return pl.pallas_call(
        kern,
        grid_spec=pltpu.PrefetchScalarGridSpec(
            num_scalar_prefetch=1,
            grid=(batch, N_HEADS_B, nq),
            in_specs=[vec_spec, vec_spec, vec_spec, vec_spec,
                      pl.BlockSpec((1, w), lambda b, h, i, s: (0, 0)),
                      pl.BlockSpec((tq, w), lambda b, h, i, s: (b * nq + i, q_col + h)),
                      pl.BlockSpec((seq, w), lambda b, h, i, s: (b, k_col + h)),
                      pl.BlockSpec((seq, w), lambda b, h, i, s: (b, v_col + h))],
            out_specs=pl.BlockSpec((tq, w), lambda b, h, i, s: (b * nq + i, h)),
            scratch_shapes=[pltpu.VMEM((w, seq), BF16),
                            pltpu.VMEM((2 * seq - tq, tq), F32),
                            pltpu.VMEM((seq, tq), F32),
                            pltpu.VMEM((seq, tq), F32)]),
        out_shape=jax.ShapeDtypeStruct((batch * seq, N_HEADS_B * w), BF16),
        compiler_params=_params(("arbitrary", "arbitrary", "arbitrary")),
        name="diff_attn",
    )(slopes, *lam_vecs, subln_g, proj, proj, proj)


def _merge_kernel(ya_ref, yb_ref, ga_ref, gb_ref, x_ref, g1_ref, sc2_ref, sh2_ref, n2g_ref,
                  woa_ref, wob_ref, wo_ref, wrh_ref, wrl_ref, br_ref,
                  h_ref, u2p_ref, lg_ref):
    a = jnp.dot(ya_ref[...], woa_ref[...], preferred_element_type=F32)
    b = jnp.dot(yb_ref[...], wob_ref[...], preferred_element_type=F32)
    merged = (jax.nn.sigmoid(ga_ref[...].astype(F32)) * a
              + jax.nn.sigmoid(gb_ref[...].astype(F32)) * b)
    h = x_ref[...] + g1_ref[...] * jnp.dot(merged.astype(BF16), wo_ref[...],
                                           preferred_element_type=F32)
    h_ref[...] = h
    y = h * lax.rsqrt(jnp.mean(h * h, axis=-1, keepdims=True) + EPS) * n2g_ref[...]
    u2 = y * (1.0 + sc2_ref[...]) + sh2_ref[...]
    hi, lo = _split_bf16(u2)
    lg = _nt_dot(wrh_ref[...], hi) + _nt_dot(wrl_ref[...], hi) + _nt_dot(wrh_ref[...], lo)
    lg_ref[...] = lg + br_ref[...]
    bits = pltpu.bitcast(hi.astype(F32), U32)
    half = bits.shape[1] // 2
    u2p_ref[...] = (bits[:, :half] >> 16) | (bits[:, half:] & jnp.uint32(0xFFFF0000))


def _merge(ya, yb, proj, x2, mod3, n2g, woa, wob, wo, wr_hi, wr_lo, br, seq, ga_col, gb_col, tm=512):
    t, d = x2.shape
    per_b = seq // tm
    wa = ya.shape[1]
    wb = yb.shape[1]
    ne = wr_hi.shape[0]

    def mod_spec(idx):
        return pl.BlockSpec((None, 1, d), lambda i: ((i // per_b) * N_MOD + idx, 0, 0))

    def const_spec(shape):
        return pl.BlockSpec(shape, lambda i: (0,) * len(shape), pipeline_mode=pl.Buffered(1))

    return pl.pallas_call(
        _merge_kernel,
        grid=(t // tm,),
        in_specs=[pl.BlockSpec((tm, wa), lambda i: (i, 0)),
                  pl.BlockSpec((tm, wb), lambda i: (i, 0)),
                  pl.BlockSpec((tm, d), lambda i: (i, ga_col)),
                  pl.BlockSpec((tm, d), lambda i: (i, gb_col)),
                  pl.BlockSpec((tm, d), lambda i: (i, 0)),
                  mod_spec(2), mod_spec(4), mod_spec(3),
                  const_spec((1, d)),
                  const_spec((wa, d)), const_spec((wb, d)), const_spec((d, d)),
                  const_spec((ne, d)), const_spec((ne, d)), const_spec((ne, 1))],
        out_specs=[pl.BlockSpec((tm, d), lambda i: (i, 0)),
                   pl.BlockSpec((tm, d // 2), lambda i: (i, 0)),
                   pl.BlockSpec((ne, tm), lambda i: (0, i))],
        out_shape=[jax.ShapeDtypeStruct((t, d), F32),
                   jax.ShapeDtypeStruct((t, d // 2), U32),
                   jax.ShapeDtypeStruct((ne, t), F32)],
        compiler_params=_params(("arbitrary",)),
        name="merge",
    )(ya, yb, proj, proj, x2, mod3, mod3, mod3, n2g, woa, wob, wo, wr_hi, wr_lo, br)


def _route_kernel(lg_ref, p_ref, dest_ref, cnt_ref, carry_ref, total_ref):
    sweep = pl.program_id(0)

    @pl.when(pl.program_id(1) == 0)
    def _():
        @pl.when(sweep == 0)
        def _():
            total_ref[...] = jnp.zeros_like(total_ref)

        @pl.when(sweep == 1)
        def _():
            total_ref[...] = carry_ref[...]

        carry_ref[...] = jnp.zeros_like(carry_ref)

    lg = lg_ref[...]
    ne, tr = lg.shape
    eio = lax.broadcasted_iota(I32, (ne, tr), 0)
    work = lg
    vals, hots = [], []
    for k in range(TOP_K):
        mx = jnp.max(work, axis=0, keepdims=True)
        am = jnp.min(jnp.where(work == mx, eio, ne), axis=0, keepdims=True)
        hot = eio == am
        vals.append(mx)
        hots.append(hot)
        work = jnp.where(hot, -jnp.inf, work)
    exps = [jnp.exp(v - vals[0]) for v in vals]
    denom = functools.reduce(jnp.add, exps)
    for k in range(TOP_K):
        p_ref[k:k + 1, :] = exps[k] / denom
    chosen = functools.reduce(jnp.logical_or, hots)
    sel = jnp.where(chosen, 1.0, 0.0)
    tri = (lax.broadcasted_iota(I32, (tr, tr), 0) < lax.broadcasted_iota(I32, (tr, tr), 1))
    before = jnp.dot(sel.astype(BF16), jnp.where(tri, 1.0, 0.0).astype(BF16),
                     preferred_element_type=F32)
    carry = carry_ref[...]
    nblk = jnp.floor((total_ref[...][:, 0:1] + (ROW_BLOCK - 1)) * (1.0 / ROW_BLOCK))
    e_row = lax.broadcasted_iota(I32, (ne, ne), 0)
    e_col = lax.broadcasted_iota(I32, (ne, ne), 1)
    nblk_lanes = jnp.sum(jnp.where(e_row == e_col, nblk, 0.0), axis=0, keepdims=True)
    first_row = jnp.sum(jnp.where(e_col < e_row, nblk_lanes, 0.0), axis=1,
                        keepdims=True) * ROW_BLOCK
    place = before + (carry[:, 0:1] + first_row)
    for k in range(TOP_K):
        dest_ref[k:k + 1, :] = jnp.sum(jnp.where(hots[k], place, 0.0), axis=0,
                                       keepdims=True).astype(I32)
    carry = carry + jnp.sum(sel, axis=1, keepdims=True)
    carry_ref[...] = carry
    cnt_ref[...] = carry.astype(I32)


def _route(logits_t, tr=512):
    ne, t = logits_t.shape
    slot_spec = pl.BlockSpec((None, TOP_K, tr), lambda s, i: (s, 0, i))
    probs, dest, cnt = pl.pallas_call(
        _route_kernel,
        grid=(2, t // tr),
        in_specs=[pl.BlockSpec((ne, tr), lambda s, i: (0, i))],
        out_specs=[slot_spec, slot_spec,
                   pl.BlockSpec((None, ne, LANES), lambda s, i: (s, 0, 0))],
        out_shape=[jax.ShapeDtypeStruct((2, TOP_K, t), F32),
                   jax.ShapeDtypeStruct((2, TOP_K, t), I32),
                   jax.ShapeDtypeStruct((2, ne, LANES), I32)],
        scratch_shapes=[pltpu.VMEM((ne, LANES), F32), pltpu.VMEM((ne, LANES), F32)],
        compiler_params=_params(("arbitrary", "arbitrary")),
        name="route",
    )(logits_t)
    return probs[1], dest[1], cnt[1]


def _dispatch_kernel(dest_ref, fill_ref, nfill_ref, src_ref, dst_ref, sem, fill_sem, *, tm, t_total):
    i = pl.program_id(0)

    @pl.when(i == 0)
    def _():
        def fill(b):
            rows = pl.ds(pl.multiple_of(fill_ref[b] * ROW_BLOCK, ROW_BLOCK), ROW_BLOCK)
            return pltpu.make_async_copy(src_ref.at[pl.ds(0, ROW_BLOCK)], dst_ref.at[rows], fill_sem)

        def fill_start(b, c):
            fill(b).start()
            return c

        def fill_wait(b, c):
            fill(b).wait()
            return c

        lax.fori_loop(0, nfill_ref[0], fill_start, 0)
        lax.fori_loop(0, nfill_ref[0], fill_wait, 0)

    for k in range(TOP_K):
        def group(g, c):
            j0 = pl.multiple_of(g * SUBLANES, SUBLANES)
            tile = src_ref.at[pl.ds(j0, SUBLANES)]
            for u in range(SUBLANES):
                row = dest_ref[k * t_total + i * tm + j0 + u]
                pltpu.make_async_copy(tile.at[pl.ds(u, 1)], dst_ref.at[pl.ds(row, 1)], sem).start()
            return c
        lax.fori_loop(0, tm // SUBLANES, group, 0)
    for k in range(TOP_K):
        pltpu.make_async_copy(src_ref, dst_ref.at[pl.ds(0, tm)], sem).wait()


def _dispatch(dest_flat, fill_blocks, nfill, u2p, total_rows, tm=512):
    t, w = u2p.shape
    kern = functools.partial(_dispatch_kernel, tm=tm, t_total=t)
    return pl.pallas_call(
        kern,
        grid_spec=pltpu.PrefetchScalarGridSpec(
            num_scalar_prefetch=3,
            grid=(t // tm,),
            in_specs=[pl.BlockSpec((tm, w), lambda i, d, fb, nf: (i, 0))],
            out_specs=pl.BlockSpec(memory_space=pl.ANY),
            scratch_shapes=[pltpu.SemaphoreType.DMA(()), pltpu.SemaphoreType.DMA(())]),
        out_shape=jax.ShapeDtypeStruct((total_rows, w), U32),
        compiler_params=_params(("arbitrary",)),
        name="dispatch",
    )(dest_flat, fill_blocks, nfill, u2p)


def _moe_kernel(che_ref, chblk_ref, chn_ref, nvalid_ref,
                xs_ref, wg_ref, wu_ref, wd_ref, bg_ref, bu_ref, bd_ref,
                ys_ref,
                xin_ref, xbf_ref, yacc_ref, act_ref, wgb_ref, wub_ref, wdb_ref, in_sem, out_sem,
                *, nff, nsplit, mm_rows):
    c = pl.program_id(0)
    f = pl.program_id(1)
    nvalid = nvalid_ref[0]
    half = xin_ref.shape[1]

    def in_copy(cc, j):
        row = pl.multiple_of((chblk_ref[cc] + j) * ROW_BLOCK, ROW_BLOCK)
        return pltpu.make_async_copy(xs_ref.at[pl.ds(row, ROW_BLOCK)],
                                     xin_ref.at[pl.ds(j * ROW_BLOCK, ROW_BLOCK)], in_sem)

    def out_copy(cc, j):
        row = pl.multiple_of((chblk_ref[cc] + j) * ROW_BLOCK, ROW_BLOCK)
        return pltpu.make_async_copy(yacc_ref.at[pl.ds(j * ROW_BLOCK, ROW_BLOCK)],
                                     ys_ref.at[pl.ds(row, ROW_BLOCK)], out_sem)

    def for_blocks(cc, fn):
        def one(j, carry):
            fn(cc, j)
            return carry
        lax.fori_loop(0, chn_ref[cc], one, 0)

    def chunk_step():
        @pl.when(f == 0)
        def _():
            @pl.when(c == 0)
            def _():
                xbf_ref[...] = jnp.zeros_like(xbf_ref)
                for_blocks(c, lambda cc, j: in_copy(cc, j).start())

            for_blocks(c, lambda cc, j: in_copy(cc, j).wait())

            def unpack(cc, j):
                rows = pl.ds(pl.multiple_of(j * ROW_BLOCK, ROW_BLOCK), ROW_BLOCK)
                w = xin_ref[rows, :]
                xbf_ref[rows, :half] = pltpu.bitcast(w << 16, F32).astype(BF16)
                xbf_ref[rows, half:] = pltpu.bitcast(w & jnp.uint32(0xFFFF0000), F32).astype(BF16)

            for_blocks(c, unpack)

            @pl.when(c + 1 < nvalid)
            def _():
                for_blocks(c + 1, lambda cc, j: in_copy(cc, j).start())

        mm_blocks = mm_rows // ROW_BLOCK
        rest = jnp.maximum(chn_ref[c] - mm_blocks, 0)
        n_full = rest // mm_blocks
        rem = rest - n_full * mm_blocks
        tail0 = pl.multiple_of((1 + n_full) * mm_rows, ROW_BLOCK)

        def for_tail(fn):
            for r in range(1, mm_blocks):
                pl.when(rem == r)(functools.partial(fn, pl.ds(tail0, r * ROW_BLOCK)))

        def gate_up(rows, wg, wu):
            x = xbf_ref[rows, :]
            g = jnp.minimum(jnp.dot(x, wg, preferred_element_type=F32) + bg_ref[...], SWIGLU_LIMIT)
            u = jnp.clip(jnp.dot(x, wu, preferred_element_type=F32) + bu_ref[...],
                         -SWIGLU_LIMIT, SWIGLU_LIMIT)
            act_ref[rows, :] = ((u + 1.0) * (g * jax.nn.sigmoid(SWIGLU_ALPHA * g))).astype(BF16)

        wg = wg_ref[...].astype(BF16)
        wu = wu_ref[...].astype(BF16)
        wgb_ref[...] = wg
        wub_ref[...] = wu
        gate_up(slice(0, mm_rows), wg, wu)
        wdb_ref[...] = wd_ref[...].astype(BF16)

        def gate_up_body(rb, carry):
            gate_up(pl.ds(pl.multiple_of(rb * mm_rows, mm_rows), mm_rows), wgb_ref[...], wub_ref[...])
            return carry

        lax.fori_loop(1, 1 + n_full, gate_up_body, 0)
        for_tail(lambda rows: gate_up(rows, wgb_ref[...], wub_ref[...]))

        @pl.when((f == 0) & (c > 0))
        def _():
            for_blocks(c - 1, lambda cc, j: out_copy(cc, j).wait())

        first = f == 0
        ncol = yacc_ref.shape[1] // nsplit

        def down(rows):
            a = act_ref[rows, :]
            for s in range(nsplit):
                cols = slice(s * ncol, (s + 1) * ncol)
                part = jnp.dot(a, wdb_ref[:, cols], preferred_element_type=F32)
                base = jnp.where(first, jnp.broadcast_to(bd_ref[:, cols], part.shape),
                                 yacc_ref[rows, cols])
                yacc_ref[rows, cols] = base + part

        def down_body(rb, carry):
            down(pl.ds(pl.multiple_of(rb * mm_rows, mm_rows), mm_rows))
            return carry

        lax.fori_loop(0, 1 + n_full, down_body, 0)
        for_tail(down)

        @pl.when(f == nff - 1)
        def _():
            for_blocks(c, lambda cc, j: out_copy(cc, j).start())

            @pl.when(c == nvalid - 1)
            def _():
                for_blocks(c, lambda cc, j: out_copy(cc, j).wait())

                def fill(b):
                    rows = pl.ds(pl.multiple_of(b * ROW_BLOCK, ROW_BLOCK), ROW_BLOCK)
                    return pltpu.make_async_copy(yacc_ref.at[pl.ds(0, ROW_BLOCK)], ys_ref.at[rows],
                                                 out_sem)

                def fill_start(b, carry):
                    fill(b).start()
                    return carry

                def fill_wait(b, carry):
                    fill(b).wait()
                    return carry

                nused = chblk_ref[c] + chn_ref[c]
                ntotal = ys_ref.shape[0] // ROW_BLOCK
                lax.fori_loop(nused, ntotal, fill_start, 0)
                lax.fori_loop(nused, ntotal, fill_wait, 0)

    chunk_step()


def _moe(che, chblk, chn, nvalid, xs, wg, wu, wd, bg, bu, bd):
    ne, d, dff = wg.shape
    rows = CHUNK_BLOCKS * ROW_BLOCK
    nff = dff // FF_TILE

    kern = functools.partial(_moe_kernel, nff=nff, nsplit=4, mm_rows=MM_ROW_BLOCKS * ROW_BLOCK)
    return pl.pallas_call(
        kern,
        grid_spec=pltpu.PrefetchScalarGridSpec(
            num_scalar_prefetch=4,
            grid=(nvalid[0], nff),
            in_specs=[pl.BlockSpec(memory_space=pl.ANY),
                      pl.BlockSpec((None, d, FF_TILE), lambda c, f, e, b, n, nv: (e[c], 0, f)),
                      pl.BlockSpec((None, d, FF_TILE), lambda c, f, e, b, n, nv: (e[c], 0, f)),
                      pl.BlockSpec((None, FF_TILE, d), lambda c, f, e, b, n, nv: (e[c], f, 0)),
                      pl.BlockSpec((None, 1, FF_TILE), lambda c, f, e, b, n, nv: (e[c], 0, f)),
                      pl.BlockSpec((None, 1, FF_TILE), lambda c, f, e, b, n, nv: (e[c], 0, f)),
                      pl.BlockSpec((None, 1, d), lambda c, f, e, b, n, nv: (e[c], 0, 0))],
            out_specs=pl.BlockSpec(memory_space=pl.ANY),
            scratch_shapes=[pltpu.VMEM((rows, d // 2), U32),
                            pltpu.VMEM((rows, d), BF16),
                            pltpu.VMEM((rows, d), F32),
                            pltpu.VMEM((rows, FF_TILE), BF16),
                            pltpu.VMEM((d, FF_TILE), BF16),
                            pltpu.VMEM((d, FF_TILE), BF16),
                            pltpu.VMEM((FF_TILE, d), BF16),
                            pltpu.SemaphoreType.DMA(()),
                            pltpu.SemaphoreType.DMA(())]),
        out_shape=jax.ShapeDtypeStruct((xs.shape[0], d), F32),
        compiler_params=_params(("arbitrary", "arbitrary")),
        name="moe",
    )(che, chblk, chn, nvalid, xs, wg, wu, wd, bg, bu, bd)


def _combine_kernel(dest_ref, ys_ref, h_ref, p_ref, g2_ref, fg_ref, o_ref, buf_ref, sem,
                    *, tc, nsteps, final_norm):
    i = pl.program_id(0)
    t_total = nsteps * tc

    def issue(step, slot):
        for k in range(TOP_K):
            def group(g, c):
                j0 = pl.multiple_of(g * SUBLANES, SUBLANES)
                tile = buf_ref.at[slot, k, pl.ds(j0, SUBLANES)]
                for u in range(SUBLANES):
                    src = dest_ref[k * t_total + step * tc + j0 + u]
                    pltpu.make_async_copy(ys_ref.at[pl.ds(src, 1)], tile.at[pl.ds(u, 1)],
                                          sem.at[slot]).start()
                return c
            lax.fori_loop(0, tc // SUBLANES, group, 0)

    def drain(slot):
        for k in range(TOP_K):
            pltpu.make_async_copy(ys_ref.at[pl.ds(0, tc)], buf_ref.at[slot, k], sem.at[slot]).wait()

    slot = i % 2

    @pl.when(i == 0)
    def _():
        issue(0, 0)

    @pl.when(i + 1 < nsteps)
    def _():
        issue(i + 1, 1 - slot)

    drain(slot)
    p = p_ref[...]
    moe = p[:, 0:1] * buf_ref[slot, 0]
    for k in range(1, TOP_K):
        moe += p[:, k:k + 1] * buf_ref[slot, k]
    h = h_ref[...] + g2_ref[...] * moe
    if final_norm:
        h = h * lax.rsqrt(jnp.mean(h * h, axis=-1, keepdims=True) + EPS) * fg_ref[...]
    o_ref[...] = h


def _combine(dest_flat, ys, h, probs_t, mod3, final_g, seq, final_norm, tc=256):
    t, d = h.shape
    per_b = seq // tc
    nsteps = t // tc
    kern = functools.partial(_combine_kernel, tc=tc, nsteps=nsteps, final_norm=final_norm)
    return pl.pallas_call(
        kern,
        grid_spec=pltpu.PrefetchScalarGridSpec(
            num_scalar_prefetch=1,
            grid=(nsteps,),
            in_specs=[pl.BlockSpec(memory_space=pl.ANY),
                      pl.BlockSpec((tc, d), lambda i, s: (i, 0)),
                      pl.BlockSpec((tc, TOP_K), lambda i, s: (i, 0)),
                      pl.BlockSpec((None, 1, d), lambda i, s: ((i // per_b) * N_MOD + 5, 0, 0)),
                      pl.BlockSpec((1, d), lambda i, s: (0, 0))],
            out_specs=pl.BlockSpec((tc, d), lambda i, s: (i, 0)),
            scratch_shapes=[pltpu.VMEM((2, TOP_K, tc, d), F32),
                            pltpu.SemaphoreType.DMA((2,))]),
        out_shape=jax.ShapeDtypeStruct((t, d), F32),
        compiler_params=_params(("arbitrary",)),
        name="combine",
    )(dest_flat, ys, h, probs_t, mod3, final_g)


def _routing_plan(counts, t):
    nblk_total = -(-t * TOP_K // ROW_BLOCK) + N_EXPERTS
    nblk = (counts + ROW_BLOCK - 1) // ROW_BLOCK
    blk_end = jnp.cumsum(nblk)
    blk_start = blk_end - nblk
    ntail_max = nblk_total - t * TOP_K // ROW_BLOCK
    fill_blocks = jnp.concatenate([jnp.maximum(blk_end - 1, 0),
                                   jnp.minimum(blk_end[-1] + jnp.arange(ntail_max), nblk_total - 1)])
    nfill = N_EXPERTS + nblk_total - blk_end[-1]
    max_chunks = -(-nblk_total // CHUNK_BLOCKS) + N_EXPERTS
    nch = (nblk + CHUNK_BLOCKS - 1) // CHUNK_BLOCKS
    ch_end = jnp.cumsum(nch)
    ch_start = ch_end - nch
    nvalid = ch_end[-1]
    cid = jnp.minimum(jnp.arange(max_chunks, dtype=I32), nvalid - 1)
    che =jnp.clip(jnp.searchsorted(ch_end, cid, side='right'), 0, N_EXPERTS - 1).astype(I32)
    local = cid - ch_start[che]
    nch_e = jnp.maximum(nch[che], 1)
    n_mm = (nblk[che] + MM_ROW_BLOCKS - 1) // MM_ROW_BLOCKS
    size = n_mm // nch_e
    extra = n_mm % nch_e
    first_mm = local * size + jnp.minimum(local, extra)
    chblk = (blk_start[che] + first_mm * MM_ROW_BLOCKS).astype(I32)
    chn = jnp.minimum((size + (local < extra)) * MM_ROW_BLOCKS,
                      nblk[che] - first_mm * MM_ROW_BLOCKS).astype(I32)
    return (fill_blocks.astype(I32), nfill.astype(I32).reshape(1), che, chblk, chn,
            nvalid.astype(I32).reshape(1), nblk_total * ROW_BLOCK)


def kernel(x, c, w_ada, b_ada, norm1_g, w_in, lam_q1, lam_k1, lam_q2, lam_k2, subln_g, w_out_a, w_out_b, w_o, norm2_g, w_router, b_router, w_gate, b_gate, w_up, b_up, w_down, b_down, final_g):
    batch, seq, d = x.shape
    t = batch * seq
    depth = w_ada.shape[0]
    wa = N_HEADS_A * HEAD_DIM
    wb = N_HEADS_B * 2 * HEAD_DIM
    slopes = _alibi_slopes(N_HEADS_A + N_HEADS_B)
    slopes_a = jnp.asarray(slopes[:N_HEADS_A])
    slopes_b = jnp.asarray(slopes[N_HEADS_A:])
    c8 = jnp.pad(c, ((0, 8 - batch), (0, 0)))

    h = x.reshape(t, d)
    for l in range(depth):
        lambda_init = 0.8 - 0.6 * math.exp(-0.3 * l)
        mod = _ada(c8, w_ada[l], b_ada[l].reshape(1, -1))[:batch]
        mod3 = mod.reshape(batch * N_MOD, 1, d)

        u = _norm_mod(h, norm1_g[l].reshape(1, d), mod3, 1, 0, seq)
        proj = _inproj(u, w_in[l], q_tiles=(0, 3 * wa // 1024))
        y_a = _dil_attention(proj, slopes_a, batch, seq, 0, wa // HEAD_DIM, 2 * wa // HEAD_DIM)
        off_b = 3 * wa // (2 * HEAD_DIM)
        nb = wb // (2 * HEAD_DIM)
        y_b = _diff_attention(proj, slopes_b,
                              [v[l].reshape(1, HEAD_DIM) for v in (lam_q1, lam_k1, lam_q2, lam_k2)],
                              subln_g[l].reshape(1, -1), lambda_init, batch, seq,
                              off_b, off_b + nb, off_b + 2 * nb)
        wr_hi, wr_lo = _split_bf16(w_router[l].T)
        gate_col = (3 * wa + 3 * wb) // d
        h, u2p, logits_t = _merge(y_a, y_b, proj, h, mod3, norm2_g[l].reshape(1, d),
                                  w_out_a[l].astype(BF16), w_out_b[l].astype(BF16),
                                  w_o[l].astype(BF16), wr_hi, wr_lo, b_router[l].reshape(-1, 1),
                                  seq, gate_col, gate_col + 1)
        probs, dest, cnt = _route(logits_t)
        (fill_blocks, nfill, che, chblk, chn, nvalid, total_rows) = _routing_plan(cnt[:, 0], t)
        xs = _dispatch(dest.reshape(-1), fill_blocks, nfill, u2p, total_rows)
        ys = _moe(che, chblk, chn, nvalid, xs, w_gate[l], w_up[l], w_down[l],
                  b_gate[l].reshape(N_EXPERTS, 1, -1), b_up[l].reshape(N_EXPERTS, 1, -1),
                  b_down[l].reshape(N_EXPERTS, 1, -1))
        h = _combine(dest.reshape(-1), ys, h, probs.T, mod3, final_g.reshape(1, d), seq,
                     final_norm=(l == depth - 1))
    return h.reshape(batch, seq, d)
```

```python
import functools
import math

import numpy as np
import jax
import jax.numpy as jnp
from jax import lax
from jax.experimental import pallas as pl
from jax.experimental.pallas import tpu as pltpu

F32 = jnp.float32
BF16 = jnp.bfloat16
U32 = jnp.uint32
I32 = jnp.int32

HEAD_DIM = 128
N_HEADS_A = 8
N_HEADS_B = 4
DILATED_PATTERNS = ((128, 1), (512, 4), (2048, 16))
N_EXPERTS = 32
TOP_K = 4
SWIGLU_LIMIT = 7.0
SWIGLU_ALPHA = 1.702
N_MOD = 6
EPS = 1e-5
NEG = -1e30
LOG2E = math.log2(math.e)

LANES = 128
SUBLANES = 8
V7X_VMEM_LIMIT = 56 * 1024 * 1024

ROW_BLOCK = 128
CHUNK_BLOCKS = 12
MM_ROW_BLOCKS = 4
FF_TILE = 512


def _alibi_slopes(n):
    return np.array([2.0 ** (-8.0 * (i + 1) / n) for i in range(n)], dtype=np.float32)


def _nt_dot(a, b):
    return lax.dot_general(a, b, (((1,), (1,)), ((), ())), preferred_element_type=F32)


def _split_bf16(x):
    hi = x.astype(BF16)
    lo = (x - hi.astype(F32)).astype(BF16)
    return hi, lo


def _params(sem, vmem=V7X_VMEM_LIMIT):
    return pltpu.CompilerParams(dimension_semantics=sem, vmem_limit_bytes=vmem)


def _ada_kernel(c_ref, w_ref, b_ref, o_ref):
    c = c_ref[...]
    a = c * jax.nn.sigmoid(c)
    a_hi, a_lo = _split_bf16(a)
    w_hi, w_lo = _split_bf16(w_ref[...])
    acc = jnp.dot(a_hi, w_hi, preferred_element_type=F32)
    acc += jnp.dot(a_lo, w_hi, preferred_element_type=F32)
    acc += jnp.dot(a_hi, w_lo, preferred_element_type=F32)
    o_ref[...] = acc + b_ref[...]


def _ada(c8, w, b, tn=1024):
    m, d = c8.shape
    n = w.shape[1]
    return pl.pallas_call(
        _ada_kernel,
        grid=(n // tn,),
        in_specs=[pl.BlockSpec((m, d), lambda j: (0, 0)),
                  pl.BlockSpec((d, tn), lambda j: (0, j)),
                  pl.BlockSpec((1, tn), lambda j: (0, j))],
        out_specs=pl.BlockSpec((m, tn), lambda j: (0, j)),
        out_shape=jax.ShapeDtypeStruct((m, n), F32),
        compiler_params=_params(("arbitrary",)),
        name="ada",
    )(c8, w, b)


def _norm_mod_kernel(x_ref, g_ref, sc_ref, sh_ref, o_ref):
    x = x_ref[...]
    y = x * lax.rsqrt(jnp.mean(x * x, axis=-1, keepdims=True) + EPS) * g_ref[...]
    o_ref[...] = (y * (1.0 + sc_ref[...]) + sh_ref[...]).astype(o_ref.dtype)


def _norm_mod(x2, g, mod3, i_scale, i_shift, seq, tm=512):
    t, d = x2.shape
    per_b = seq // tm
    return pl.pallas_call(
        _norm_mod_kernel,
        grid=(t // tm,),
        in_specs=[pl.BlockSpec((tm, d), lambda i: (i, 0)),
                  pl.BlockSpec((1, d), lambda i: (0, 0)),
                  pl.BlockSpec((None, 1, d), lambda i: ((i // per_b) * N_MOD + i_scale, 0, 0)),
                  pl.BlockSpec((None, 1, d), lambda i: ((i // per_b) * N_MOD + i_shift, 0, 0))],
        out_specs=pl.BlockSpec((tm, d), lambda i: (i, 0)),
        out_shape=jax.ShapeDtypeStruct((t, d), BF16),
        compiler_params=_params(("arbitrary",)),
        name="norm1",
    )(x2, g, mod3, mod3)


def _inproj_kernel(u_ref, w_ref, o_ref, *, q_tiles, scale):
    n = pl.program_id(0)
    acc = jnp.dot(u_ref[...], w_ref[...].astype(BF16), preferred_element_type=F32)
    is_q = functools.reduce(jnp.logical_or, [n == q for q in q_tiles])
    o_ref[...] = (acc * jnp.where(is_q, scale, 1.0)).astype(BF16)


def _inproj(u, w, q_tiles, tm=2048, tn=1024):
    t, d = u.shape
    n = w.shape[1]
    return pl.pallas_call(
        functools.partial(_inproj_kernel, q_tiles=q_tiles, scale=HEAD_DIM ** -0.5 * LOG2E),
        grid=(n // tn, t // tm),
        in_specs=[pl.BlockSpec((tm, d), lambda j, i: (i, 0)),
                  pl.BlockSpec((d, tn), lambda j, i: (0, j))],
        out_specs=pl.BlockSpec((tm, tn), lambda j, i: (i, j)),
        out_shape=jax.ShapeDtypeStruct((t, n), BF16),
        compiler_params=_params(("arbitrary", "arbitrary")),
        name="inproj",
    )(u, w)


def _dil_tables(tq, span):
    nside = span // tq
    o = lax.broadcasted_iota(I32, (2 * nside + 1, tq, tq), 0) - nside
    i = lax.broadcasted_iota(I32, (2 * nside + 1, tq, tq), 1)
    j = lax.broadcasted_iota(I32, (2 * nside + 1, tq, tq), 2)
    ad = jnp.abs(o * tq + j - i)
    mult = jnp.zeros_like(ad)
    for window, dil in DILATED_PATTERNS:
        mult += ((ad % dil == 0) & (ad // dil <= window // (2 * dil))).astype(I32)
    lm = jnp.where(mult > 0, jnp.log2(jnp.maximum(mult, 1).astype(F32)), NEG)
    return ad.astype(F32) * LOG2E, lm


def _dil_kernel(slopes_ref, q_ref, k_ref, v_ref, ad_ref, lm_ref, o_ref, bias_ref, vt_ref, st_ref,
                *, tq, nside, nq):
    h = pl.program_id(1)
    i = pl.program_id(2)

    @pl.when(i == 0)
    def _():
        bias_ref[:2 * nside + 1] = lm_ref[...] - slopes_ref[h] * ad_ref[...]
        bias_ref[2 * nside + 1] = jnp.full((tq, tq), NEG, F32)
        for kb in range(nq):
            vt_ref[kb] = v_ref[kb * tq:(kb + 1) * tq, :].astype(F32).T.astype(BF16)

    nwin = 2 * nside + 1
    b0 = jnp.clip(i - nside, 0, nq - nwin)
    q = q_ref[...]

    def score_block(j, m):
        o = b0 + j - i
        plane = jnp.where(jnp.abs(o) <= nside, nside - o, nwin)
        kb = k_ref[pl.ds(pl.multiple_of((b0 + j) * tq, tq), tq), :]
        s = _nt_dot(kb, q) + bias_ref[plane]
        st_ref[j] = s
        return jnp.maximum(m, jnp.max(s, axis=0, keepdims=True))

    m = lax.fori_loop(0, nwin, score_block, jnp.full((1, tq), NEG, F32), unroll=True)
    l = acc = None
    for j in range(nwin):
        p = jnp.exp2(st_ref[j] - m)
        cl = jnp.sum(p, axis=0, keepdims=True)
        ca = jnp.dot(vt_ref[b0 + j], p.astype(BF16), preferred_element_type=F32)
        l = cl if j == 0 else l + cl
        acc = ca if j == 0 else acc + ca
    o_ref[...] = (acc / l).T.astype(o_ref.dtype)


def _dil_attention(proj, slopes, batch, seq, q_col, k_col, v_col, tq=512):
    span = max(w // 2 for w, _ in DILATED_PATTERNS)
    nside = span // tq
    nq = seq // tq
    ad, lm = _dil_tables(tq, span)
    tab_spec = pl.BlockSpec((2 * nside + 1, tq, tq), lambda b, h, i, s: (0, 0, 0))
    kern = functools.partial(_dil_kernel, tq=tq, nside=nside, nq=nq)
    return pl.pallas_call(
        kern,
        grid_spec=pltpu.PrefetchScalarGridSpec(
            num_scalar_prefetch=1,
            grid=(batch, N_HEADS_A, nq),
            in_specs=[pl.BlockSpec((tq, HEAD_DIM), lambda b, h, i, s: (b * nq + i, q_col + h)),
                      pl.BlockSpec((seq, HEAD_DIM), lambda b, h, i, s: (b, k_col + h)),
                      pl.BlockSpec((seq, HEAD_DIM), lambda b, h, i, s: (b, v_col + h)),
                      tab_spec, tab_spec],
            out_specs=pl.BlockSpec((tq, HEAD_DIM), lambda b, h, i, s: (b * nq + i, h)),
            scratch_shapes=[pltpu.VMEM((2 * nside + 2, tq, tq), F32),
                            pltpu.VMEM((nq, HEAD_DIM, tq), BF16),
                            pltpu.VMEM((2 * nside + 1, tq, tq), F32)]),
        out_shape=jax.ShapeDtypeStruct((batch * seq, N_HEADS_A * HEAD_DIM), BF16),
        compiler_params=_params(("arbitrary", "arbitrary", "arbitrary")),
        name="dil_attn",
    )(slopes, proj, proj, proj, ad, lm)


def _diff_kernel(slopes_ref, lq1_ref, lk1_ref, lq2_ref, lk2_ref, g_ref, q_ref, k_ref, v_ref, o_ref,
                 vt_ref, bias_ref, st1_ref, st2_ref, *, tq, tk, nk, lambda_init):
    h = pl.program_id(1)
    i = pl.program_id(2)

    seq = k_ref.shape[0]
    nq = seq // tq

    @pl.when(i == 0)
    def _():
        for kb in range(nk):
            cols = slice(kb * tk, (kb + 1) * tk)
            vt_ref[:, cols] = v_ref[cols, :].astype(F32).T.astype(BF16)

        neg_c = -slopes_ref[h] * LOG2E
        base = (lax.broadcasted_iota(I32, (tq, tq), 0) - lax.broadcasted_iota(I32, (tq, tq), 1)
                - (seq - tq))

        def fill(r, carry):
            rows = pl.ds(pl.multiple_of(r * tq, tq), tq)
            bias_ref[rows, :] = jnp.abs(base + r * tq).astype(F32) * neg_c
            return carry

        lax.fori_loop(0, 2 * nq - 1, fill, 0)

    q = q_ref[...]
    qs = (q[:, :HEAD_DIM], q[:, HEAD_DIM:])
    st_refs = (st1_ref, st2_ref)
    w0 = (nq - 1 - i) * tq
    def score_chunk(c, mx):
        rows = pl.ds(pl.multiple_of(c * tk, tk), tk)
        kc = k_ref[rows, :]
        b = bias_ref[pl.ds(pl.multiple_of(w0 + c * tk, tq), tk), :]
        out = []
        for j in range(2):
            s = _nt_dot(kc[:, j * HEAD_DIM:(j + 1) * HEAD_DIM], qs[j]) + b
            st_refs[j][rows, :] = s
            out.append(jnp.maximum(mx[j], jnp.max(s, axis=0, keepdims=True)))
        return tuple(out)

    neg = jnp.full((1, tq), NEG, F32)
    mx = lax.fori_loop(0, nk, score_chunk, (neg, neg), unroll=4)
    ls = [None, None]
    accs = [None, None]
    for c in range(nk):
        rows = slice(c * tk, (c + 1) * tk)
        vt = vt_ref[:, rows]
        for j in range(2):
            p = jnp.exp2(st_refs[j][rows, :] - mx[j])
            cl = jnp.sum(p, axis=0, keepdims=True)
            ca = jnp.dot(vt, p.astype(BF16), preferred_element_type=F32)
            ls[j] = cl if c == 0 else ls[j] + cl
            accs[j] = ca if c == 0 else accs[j] + ca
    (a1, a2), (l1, l2) = accs, ls

    lam = (jnp.exp(jnp.sum(lq1_ref[...] * lk1_ref[...], axis=-1, keepdims=True))
           - jnp.exp(jnp.sum(lq2_ref[...] * lk2_ref[...], axis=-1, keepdims=True)) + lambda_init)
    ot = a1 / l1 - lam * (a2 / l2)
    yt = ot * lax.rsqrt(jnp.mean(ot * ot, axis=0, keepdims=True) + EPS)
    o_ref[...] = (yt.T * (g_ref[...] * (1.0 - lambda_init))).astype(o_ref.dtype)


def _diff_attention(proj, slopes, lam_vecs, subln_g, lambda_init, batch, seq, q_col, k_col, v_col,
                    tq=512, tk=512):
    nq = seq // tq
    w = 2 * HEAD_DIM
    vec_spec = pl.BlockSpec((1, HEAD_DIM), lambda b, h, i, s: (0, 0))
    kern = functools.partial(_diff_kernel, tq=tq, tk=tk, nk=seq // tk, lambda_init=lambda_init)
    return pl.pallas_call(
        kern,
        grid_spec=pltpu.PrefetchScalarGridSpec(
            num_scalar_prefetch=1,
            grid=(batch, N_HEADS_B, nq),
            in_specs=[vec_spec, vec_spec, vec_spec, vec_spec,
                      pl.BlockSpec((1, w), lambda b, h, i, s: (0, 0)),
                      pl.BlockSpec((tq, w), lambda b, h, i, s: (b * nq + i, q_col + h)),
                      pl.BlockSpec((seq, w), lambda b, h, i, s: (b, k_col + h)),
                      pl.BlockSpec((seq, w), lambda b, h, i, s: (b, v_col + h))],
            out_specs=pl.BlockSpec((tq, w), lambda b, h, i, s: (b * nq + i, h)),
            scratch_shapes=[pltpu.VMEM((w, seq), BF16),
                            pltpu.VMEM((2 * seq - tq, tq), F32),
                            pltpu.VMEM((seq, tq), F32),
                            pltpu.VMEM((seq, tq), F32)]),
        out_shape=jax.ShapeDtypeStruct((batch * seq, N_HEADS_B * w), BF16),
        compiler_params=_params(("arbitrary", "arbitrary", "arbitrary")),
        name="diff_attn",
    )(slopes, *lam_vecs, subln_g, proj, proj, proj)


def _merge_kernel(ya_ref, yb_ref, ga_ref, gb_ref, x_ref, g1_ref, sc2_ref, sh2_ref, n2g_ref,
                  woa_ref, wob_ref, wo_ref, wrh_ref, wrl_ref, br_ref,
                  h_ref, u2p_ref, lg_ref):
    a = jnp.dot(ya_ref[...], woa_ref[...], preferred_element_type=F32)
    b = jnp.dot(yb_ref[...], wob_ref[...], preferred_element_type=F32)
    merged = (jax.nn.sigmoid(ga_ref[...].astype(F32)) * a
              + jax.nn.sigmoid(gb_ref[...].astype(F32)) * b)
    h = x_ref[...] + g1_ref[...] * jnp.dot(merged.astype(BF16), wo_ref[...],
                                           preferred_element_type=F32)
    h_ref[...] = h
    y = h * lax.rsqrt(jnp.mean(h * h, axis=-1, keepdims=True) + EPS) * n2g_ref[...]
    u2 = y * (1.0 + sc2_ref[...]) + sh2_ref[...]
    hi, lo = _split_bf16(u2)
    lg = _nt_dot(wrh_ref[...], hi) + _nt_dot(wrl_ref[...], hi) + _nt_dot(wrh_ref[...], lo)
    lg_ref[...] = lg + br_ref[...]
    bits = pltpu.bitcast(hi.astype(F32), U32)
    half = bits.shape[1] // 2
    u2p_ref[...] = (bits[:, :half] >> 16) | (bits[:, half:] & jnp.uint32(0xFFFF0000))


def _merge(ya, yb, proj, x2, mod3, n2g, woa, wob, wo, wr_hi, wr_lo, br, seq, ga_col, gb_col, tm=512):
    t, d = x2.shape
    per_b = seq // tm
    wa = ya.shape[1]
    wb = yb.shape[1]
    ne = wr_hi.shape[0]

    def mod_spec(idx):
        return pl.BlockSpec((None, 1, d), lambda i: ((i // per_b) * N_MOD + idx, 0, 0))

    def const_spec(shape):
        return pl.BlockSpec(shape, lambda i: (0,) * len(shape), pipeline_mode=pl.Buffered(1))

    return pl.pallas_call(
        _merge_kernel,
        grid=(t // tm,),
        in_specs=[pl.BlockSpec((tm, wa), lambda i: (i, 0)),
                  pl.BlockSpec((tm, wb), lambda i: (i, 0)),
                  pl.BlockSpec((tm, d), lambda i: (i, ga_col)),
                  pl.BlockSpec((tm, d), lambda i: (i, gb_col)),
                  pl.BlockSpec((tm, d), lambda i: (i, 0)),
                  mod_spec(2), mod_spec(4), mod_spec(3),
                  const_spec((1, d)),
                  const_spec((wa, d)), const_spec((wb, d)), const_spec((d, d)),
                  const_spec((ne, d)), const_spec((ne, d)), const_spec((ne, 1))],
        out_specs=[pl.BlockSpec((tm, d), lambda i: (i, 0)),
                   pl.BlockSpec((tm, d // 2), lambda i: (i, 0)),
                   pl.BlockSpec((ne, tm), lambda i: (0, i))],
        out_shape=[jax.ShapeDtypeStruct((t, d), F32),
                   jax.ShapeDtypeStruct((t, d // 2), U32),
                   jax.ShapeDtypeStruct((ne, t), F32)],
        compiler_params=_params(("arbitrary",)),
        name="merge",
    )(ya, yb, proj, proj, x2, mod3, mod3, mod3, n2g, woa, wob, wo, wr_hi, wr_lo, br)


def _route_kernel(lg_ref, p_ref, dest_ref, cnt_ref, carry_ref, total_ref):
    sweep = pl.program_id(0)

    @pl.when(pl.program_id(1) == 0)
    def _():
        @pl.when(sweep == 0)
        def _():
            total_ref[...] = jnp.zeros_like(total_ref)

        @pl.when(sweep == 1)
        def _():
            total_ref[...] = carry_ref[...]

        carry_ref[...] = jnp.zeros_like(carry_ref)

    lg = lg_ref[...]
    ne, tr = lg.shape
    eio = lax.broadcasted_iota(I32, (ne, tr), 0)
    work = lg
    vals, hots = [], []
    for k in range(TOP_K):
        mx = jnp.max(work, axis=0, keepdims=True)
        am = jnp.min(jnp.where(work == mx, eio, ne), axis=0, keepdims=True)
        hot = eio == am
        vals.append(mx)
        hots.append(hot)
        work = jnp.where(hot, -jnp.inf, work)
    exps = [jnp.exp(v - vals[0]) for v in vals]
    denom = functools.reduce(jnp.add, exps)
    for k in range(TOP_K):
        p_ref[k:k + 1, :] = exps[k] / denom
    chosen = functools.reduce(jnp.logical_or, hots)
    sel = jnp.where(chosen, 1.0, 0.0)
    tri = (lax.broadcasted_iota(I32, (tr, tr), 0) < lax.broadcasted_iota(I32, (tr, tr), 1))
    before = jnp.dot(sel.astype(BF16), jnp.where(tri, 1.0, 0.0).astype(BF16),
                     preferred_element_type=F32)
    carry = carry_ref[...]
    nblk = jnp.floor((total_ref[...][:, 0:1] + (ROW_BLOCK - 1)) * (1.0 / ROW_BLOCK))
    e_row = lax.broadcasted_iota(I32, (ne, ne), 0)
    e_col = lax.broadcasted_iota(I32, (ne, ne), 1)
    nblk_lanes = jnp.sum(jnp.where(e_row == e_col, nblk, 0.0), axis=0, keepdims=True)
    first_row = jnp.sum(jnp.where(e_col < e_row, nblk_lanes, 0.0), axis=1,
                        keepdims=True) * ROW_BLOCK
    place = before + (carry[:, 0:1] + first_row)
    for k in range(TOP_K):
        dest_ref[k:k + 1, :] = jnp.sum(jnp.where(hots[k], place, 0.0), axis=0,
                                       keepdims=True).astype(I32)
    carry = carry + jnp.sum(sel, axis=1, keepdims=True)
    carry_ref[...] = carry
    cnt_ref[...] = carry.astype(I32)


def _route(logits_t, tr=512):
    ne, t = logits_t.shape
    slot_spec = pl.BlockSpec((None, TOP_K, tr), lambda s, i: (s, 0, i))
    probs, dest, cnt = pl.pallas_call(
        _route_kernel,
        grid=(2, t // tr),
        in_specs=[pl.BlockSpec((ne, tr), lambda s, i: (0, i))],
        out_specs=[slot_spec, slot_spec,
                   pl.BlockSpec((None, ne, LANES), lambda s, i: (s, 0, 0))],
        out_shape=[jax.ShapeDtypeStruct((2, TOP_K, t), F32),
                   jax.ShapeDtypeStruct((2, TOP_K, t), I32),
                   jax.ShapeDtypeStruct((2, ne, LANES), I32)],
        scratch_shapes=[pltpu.VMEM((ne, LANES), F32), pltpu.VMEM((ne, LANES), F32)],
        compiler_params=_params(("arbitrary", "arbitrary")),
        name="route",
    )(logits_t)
    return probs[1], dest[1], cnt[1]


def _dispatch_kernel(dest_ref, fill_ref, nfill_ref, src_ref, dst_ref, sem, fill_sem, *, tm, t_total):
    i = pl.program_id(0)

    @pl.when(i == 0)
    def _():
        def fill(b):
            rows = pl.ds(pl.multiple_of(fill_ref[b] * ROW_BLOCK, ROW_BLOCK), ROW_BLOCK)
            return pltpu.make_async_copy(src_ref.at[pl.ds(0, ROW_BLOCK)], dst_ref.at[rows], fill_sem)

        def fill_start(b, c):
            fill(b).start()
            return c

        def fill_wait(b, c):
            fill(b).wait()
            return c

        lax.fori_loop(0, nfill_ref[0], fill_start, 0)
        lax.fori_loop(0, nfill_ref[0], fill_wait, 0)

    for k in range(TOP_K):
        def group(g, c):
            j0 = pl.multiple_of(g * SUBLANES, SUBLANES)
            tile = src_ref.at[pl.ds(j0, SUBLANES)]
            for u in range(SUBLANES):
                row = dest_ref[k * t_total + i * tm + j0 + u]
                pltpu.make_async_copy(tile.at[pl.ds(u, 1)], dst_ref.at[pl.ds(row, 1)], sem).start()
            return c
        lax.fori_loop(0, tm // SUBLANES, group, 0)
    for k in range(TOP_K):
        pltpu.make_async_copy(src_ref, dst_ref.at[pl.ds(0, tm)], sem).wait()


def _dispatch(dest_flat, fill_blocks, nfill, u2p, total_rows, tm=512):
    t, w = u2p.shape
    kern = functools.partial(_dispatch_kernel, tm=tm, t_total=t)
    return pl.pallas_call(
        kern,
        grid_spec=pltpu.PrefetchScalarGridSpec(
            num_scalar_prefetch=3,
            grid=(t // tm,),
            in_specs=[pl.BlockSpec((tm, w), lambda i, d, fb, nf: (i, 0))],
            out_specs=pl.BlockSpec(memory_space=pl.ANY),
            scratch_shapes=[pltpu.SemaphoreType.DMA(()), pltpu.SemaphoreType.DMA(())]),
        out_shape=jax.ShapeDtypeStruct((total_rows, w), U32),
        compiler_params=_params(("arbitrary",)),
        name="dispatch",
    )(dest_flat, fill_blocks, nfill, u2p)


def _moe_kernel(che_ref, chblk_ref, chn_ref, nvalid_ref,
                xs_ref, wg_ref, wu_ref, wd_ref, bg_ref, bu_ref, bd_ref,
                ys_ref,
                xin_ref, xbf_ref, yacc_ref, act_ref, in_sem, out_sem,
                *, nff, nsplit, mm_rows):
    c = pl.program_id(0)
    f = pl.program_id(1)
    nvalid = nvalid_ref[0]
    half = xin_ref.shape[1]

    def in_copy(cc, j):
        row = pl.multiple_of((chblk_ref[cc] + j) * ROW_BLOCK, ROW_BLOCK)
        return pltpu.make_async_copy(xs_ref.at[pl.ds(row, ROW_BLOCK)],
                                     xin_ref.at[pl.ds(j * ROW_BLOCK, ROW_BLOCK)], in_sem)

    def out_copy(cc, j):
        row = pl.multiple_of((chblk_ref[cc] + j) * ROW_BLOCK, ROW_BLOCK)
        return pltpu.make_async_copy(yacc_ref.at[pl.ds(j * ROW_BLOCK, ROW_BLOCK)],
                                     ys_ref.at[pl.ds(row, ROW_BLOCK)], out_sem)

    def for_blocks(cc, fn):
        def one(j, carry):
            fn(cc, j)
            return carry
        lax.fori_loop(0, chn_ref[cc], one, 0)

    def chunk_step():
        @pl.when(f == 0)
        def _():
            @pl.when(c == 0)
            def _():
                xbf_ref[...] = jnp.zeros_like(xbf_ref)
                for_blocks(c, lambda cc, j: in_copy(cc, j).start())

            for_blocks(c, lambda cc, j: in_copy(cc, j).wait())

            def unpack(cc, j):
                rows = pl.ds(pl.multiple_of(j * ROW_BLOCK, ROW_BLOCK), ROW_BLOCK)
                w = xin_ref[rows, :]
                xbf_ref[rows, :half] = pltpu.bitcast(w << 16, F32).astype(BF16)
                xbf_ref[rows, half:] = pltpu.bitcast(w & jnp.uint32(0xFFFF0000), F32).astype(BF16)

            for_blocks(c, unpack)

            @pl.when(c + 1 < nvalid)
            def _():
                for_blocks(c + 1, lambda cc, j: in_copy(cc, j).start())

        mm_blocks = mm_rows // ROW_BLOCK
        rest = jnp.maximum(chn_ref[c] - mm_blocks, 0)
        n_full = rest // mm_blocks
        rem = rest - n_full * mm_blocks
        tail0 = pl.multiple_of((1 + n_full) * mm_rows, ROW_BLOCK)

        def for_tail(fn):
            for r in range(1, mm_blocks):
                pl.when(rem == r)(functools.partial(fn, pl.ds(tail0, r * ROW_BLOCK)))

        def gate_up(rows):
            x = xbf_ref[rows, :]
            g = jnp.minimum(jnp.dot(x, wg_ref[...].astype(BF16), preferred_element_type=F32)
                            + bg_ref[...], SWIGLU_LIMIT)
            u = jnp.clip(jnp.dot(x, wu_ref[...].astype(BF16), preferred_element_type=F32)
                         + bu_ref[...], -SWIGLU_LIMIT, SWIGLU_LIMIT)
            act_ref[rows, :] = ((u + 1.0) * (g * jax.nn.sigmoid(SWIGLU_ALPHA * g))).astype(BF16)

        def gate_up_body(rb, carry):
            gate_up(pl.ds(pl.multiple_of(rb * mm_rows, mm_rows), mm_rows))
            return carry

        lax.fori_loop(0, 1 + n_full, gate_up_body, 0)
        for_tail(gate_up)

        @pl.when((f == 0) & (c > 0))
        def _():
            for_blocks(c - 1, lambda cc, j: out_copy(cc, j).wait())

        first = f == 0
        ncol = yacc_ref.shape[1] // nsplit

        def down(rows):
            a = act_ref[rows, :]
            for s in range(nsplit):
                cols = slice(s * ncol, (s + 1) * ncol)
                part = jnp.dot(a, wd_ref[:, cols].astype(BF16), preferred_element_type=F32)
                base = jnp.where(first, jnp.broadcast_to(bd_ref[:, cols], part.shape),
                                 yacc_ref[rows, cols])
                yacc_ref[rows, cols] = base + part

        def down_body(rb, carry):
            down(pl.ds(pl.multiple_of(rb * mm_rows, mm_rows), mm_rows))
            return carry

        lax.fori_loop(0, 1 + n_full, down_body, 0)
        for_tail(down)

        @pl.when(f == nff - 1)
        def _():
            for_blocks(c, lambda cc, j: out_copy(cc, j).start())

            @pl.when(c == nvalid - 1)
            def _():
                for_blocks(c, lambda cc, j: out_copy(cc, j).wait())

                def fill(b):
                    rows = pl.ds(pl.multiple_of(b * ROW_BLOCK, ROW_BLOCK), ROW_BLOCK)
                    return pltpu.make_async_copy(yacc_ref.at[pl.ds(0, ROW_BLOCK)], ys_ref.at[rows],
                                                 out_sem)

                def fill_start(b, carry):
                    fill(b).start()
                    return carry

                def fill_wait(b, carry):
                    fill(b).wait()
                    return carry

                nused = chblk_ref[c] + chn_ref[c]
                ntotal = ys_ref.shape[0] // ROW_BLOCK
                lax.fori_loop(nused, ntotal, fill_start, 0)
                lax.fori_loop(nused, ntotal, fill_wait, 0)

    chunk_step()


def _moe(che, chblk, chn, nvalid, xs, wg, wu, wd, bg, bu, bd):
    ne, d, dff = wg.shape
    rows = CHUNK_BLOCKS * ROW_BLOCK
    nff = dff // FF_TILE

    kern = functools.partial(_moe_kernel, nff=nff, nsplit=4, mm_rows=MM_ROW_BLOCKS * ROW_BLOCK)
    return pl.pallas_call(
        kern,
        grid_spec=pltpu.PrefetchScalarGridSpec(
            num_scalar_prefetch=4,
            grid=(nvalid[0], nff),
            in_specs=[pl.BlockSpec(memory_space=pl.ANY),
                      pl.BlockSpec((None, d, FF_TILE), lambda c, f, e, b, n, nv: (e[c], 0, f)),
                      pl.BlockSpec((None, d, FF_TILE), lambda c, f, e, b, n, nv: (e[c], 0, f)),
                      pl.BlockSpec((None, FF_TILE, d), lambda c, f, e, b, n, nv: (e[c], f, 0)),
                      pl.BlockSpec((None, 1, FF_TILE), lambda c, f, e, b, n, nv: (e[c], 0, f)),
                      pl.BlockSpec((None, 1, FF_TILE), lambda c, f, e, b, n, nv: (e[c], 0, f)),
                      pl.BlockSpec((None, 1, d), lambda c, f, e, b, n, nv: (e[c], 0, 0))],
            out_specs=pl.BlockSpec(memory_space=pl.ANY),
            scratch_shapes=[pltpu.VMEM((rows, d // 2), U32),
                            pltpu.VMEM((rows, d), BF16),
                            pltpu.VMEM((rows, d), F32),
                            pltpu.VMEM((rows, FF_TILE), BF16),
                            pltpu.SemaphoreType.DMA(()),
                            pltpu.SemaphoreType.DMA(())]),
        out_shape=jax.ShapeDtypeStruct((xs.shape[0], d), F32),
        compiler_params=_params(("arbitrary", "arbitrary")),
        name="moe",
    )(che, chblk, chn, nvalid, xs, wg, wu, wd, bg, bu, bd)


def _combine_kernel(dest_ref, ys_ref, h_ref, p_ref, g2_ref, fg_ref, o_ref, buf_ref, sem,
                    *, tc, nsteps, final_norm):
    i = pl.program_id(0)
    t_total = nsteps * tc

    def issue(step, slot):
        for k in range(TOP_K):
            def group(g, c):
                j0 = pl.multiple_of(g * SUBLANES, SUBLANES)
                tile = buf_ref.at[slot, k, pl.ds(j0, SUBLANES)]
                for u in range(SUBLANES):
                    src = dest_ref[k * t_total + step * tc + j0 + u]
                    pltpu.make_async_copy(ys_ref.at[pl.ds(src, 1)], tile.at[pl.ds(u, 1)],
                                          sem.at[slot]).start()
                return c
            lax.fori_loop(0, tc // SUBLANES, group, 0)

    def drain(slot):
        for k in range(TOP_K):
            pltpu.make_async_copy(ys_ref.at[pl.ds(0, tc)], buf_ref.at[slot, k], sem.at[slot]).wait()

    slot = i % 2

    @pl.when(i == 0)
    def _():
        issue(0, 0)

    @pl.when(i + 1 < nsteps)
    def _():
        issue(i + 1, 1 - slot)

    drain(slot)
    p = p_ref[...]
    moe = p[:, 0:1] * buf_ref[slot, 0]
    for k in range(1, TOP_K):
        moe += p[:, k:k + 1] * buf_ref[slot, k]
    h = h_ref[...] + g2_ref[...] * moe
    if final_norm:
        h = h * lax.rsqrt(jnp.mean(h * h, axis=-1, keepdims=True) + EPS) * fg_ref[...]
    o_ref[...] = h


def _combine(dest_flat, ys, h, probs_t, mod3, final_g, seq, final_norm, tc=256):
    t, d = h.shape
    per_b = seq // tc
    nsteps = t // tc
    kern = functools.partial(_combine_kernel, tc=tc, nsteps=nsteps, final_norm=final_norm)
    return pl.pallas_call(
        kern,
        grid_spec=pltpu.PrefetchScalarGridSpec(
            num_scalar_prefetch=1,
            grid=(nsteps,),
            in_specs=[pl.BlockSpec(memory_space=pl.ANY),
                      pl.BlockSpec((tc, d), lambda i, s: (i, 0)),
                      pl.BlockSpec((tc, TOP_K), lambda i, s: (i, 0)),
                      pl.BlockSpec((None, 1, d), lambda i, s: ((i // per_b) * N_MOD + 5, 0, 0)),
                      pl.BlockSpec((1, d), lambda i, s: (0, 0))],
            out_specs=pl.BlockSpec((tc, d), lambda i, s: (i, 0)),
            scratch_shapes=[pltpu.VMEM((2, TOP_K, tc, d), F32),
                            pltpu.SemaphoreType.DMA((2,))]),
        out_shape=jax.ShapeDtypeStruct((t, d), F32),
        compiler_params=_params(("arbitrary",)),
        name="combine",
    )(dest_flat, ys, h, probs_t, mod3, final_g)


def _routing_plan(counts, t):
    nblk_total = -(-t * TOP_K // ROW_BLOCK) + N_EXPERTS
    nblk = (counts + ROW_BLOCK - 1) // ROW_BLOCK
    blk_end = jnp.cumsum(nblk)
    blk_start = blk_end - nblk
    ntail_max = nblk_total - t * TOP_K // ROW_BLOCK
    fill_blocks = jnp.concatenate([jnp.maximum(blk_end - 1, 0),
                                   jnp.minimum(blk_end[-1] + jnp.arange(ntail_max), nblk_total - 1)])
    nfill = N_EXPERTS + nblk_total - blk_end[-1]
    max_chunks = -(-nblk_total // CHUNK_BLOCKS) + N_EXPERTS
    nch = (nblk + CHUNK_BLOCKS - 1) // CHUNK_BLOCKS
    ch_end = jnp.cumsum(nch)
    ch_start = ch_end - nch
    nvalid = ch_end[-1]
    cid = jnp.minimum(jnp.arange(max_chunks, dtype=I32), nvalid - 1)
    che =jnp.clip(jnp.searchsorted(ch_end, cid, side='right'), 0, N_EXPERTS - 1).astype(I32)
    local = cid - ch_start[che]
    nch_e = jnp.maximum(nch[che], 1)
    n_mm = (nblk[che] + MM_ROW_BLOCKS - 1) // MM_ROW_BLOCKS
    size = n_mm // nch_e
    extra = n_mm % nch_e
    first_mm = local * size + jnp.minimum(local, extra)
    chblk = (blk_start[che] + first_mm * MM_ROW_BLOCKS).astype(I32)
    chn = jnp.minimum((size + (local < extra)) * MM_ROW_BLOCKS,
                      nblk[che] - first_mm * MM_ROW_BLOCKS).astype(I32)
    return (fill_blocks.astype(I32), nfill.astype(I32).reshape(1), che, chblk, chn,
            nvalid.astype(I32).reshape(1), nblk_total * ROW_BLOCK)


def kernel(x, c, w_ada, b_ada, norm1_g, w_in, lam_q1, lam_k1, lam_q2, lam_k2, subln_g, w_out_a, w_out_b, w_o, norm2_g, w_router, b_router, w_gate, b_gate, w_up, b_up, w_down, b_down, final_g):
    batch, seq, d = x.shape
    t = batch * seq
    depth = w_ada.shape[0]
    wa = N_HEADS_A * HEAD_DIM
    wb = N_HEADS_B * 2 * HEAD_DIM
    slopes = _alibi_slopes(N_HEADS_A + N_HEADS_B)
    slopes_a = jnp.asarray(slopes[:N_HEADS_A])
    slopes_b = jnp.asarray(slopes[N_HEADS_A:])
    c8 = jnp.pad(c, ((0, 8 - batch), (0, 0)))

    h = x.reshape(t, d)
    for l in range(depth):
        lambda_init = 0.8 - 0.6 * math.exp(-0.3 * l)
        mod = _ada(c8, w_ada[l], b_ada[l].reshape(1, -1))[:batch]
        mod3 = mod.reshape(batch * N_MOD, 1, d)

        u = _norm_mod(h, norm1_g[l].reshape(1, d), mod3, 1, 0, seq)
        proj = _inproj(u, w_in[l], q_tiles=(0, 3 * wa // 1024))
        y_a = _dil_attention(proj, slopes_a, batch, seq, 0, wa // HEAD_DIM, 2 * wa // HEAD_DIM)
        off_b = 3 * wa // (2 * HEAD_DIM)
        nb = wb // (2 * HEAD_DIM)
        y_b = _diff_attention(proj, slopes_b,
                              [v[l].reshape(1, HEAD_DIM) for v in (lam_q1, lam_k1, lam_q2, lam_k2)],
                              subln_g[l].reshape(1, -1), lambda_init, batch, seq,
                              off_b, off_b + nb, off_b + 2 * nb)
        wr_hi, wr_lo = _split_bf16(w_router[l].T)
        gate_col = (3 * wa + 3 * wb) // d
        h, u2p, logits_t = _merge(y_a, y_b, proj, h, mod3, norm2_g[l].reshape(1, d),
                                  w_out_a[l].astype(BF16), w_out_b[l].astype(BF16),
                                  w_o[l].astype(BF16), wr_hi, wr_lo, b_router[l].reshape(-1, 1),
                                  seq, gate_col, gate_col + 1)
        probs, dest, cnt = _route(logits_t)
        (fill_blocks, nfill, che, chblk, chn, nvalid, total_rows) = _routing_plan(cnt[:, 0], t)
        xs = _dispatch(dest.reshape(-1), fill_blocks, nfill, u2p, total_rows)
        ys = _moe(che, chblk, chn, nvalid, xs, w_gate[l], w_up[l], w_down[l],
                  b_gate[l].reshape(N_EXPERTS, 1, -1), b_up[l].reshape(N_EXPERTS, 1, -1),
                  b_down[l].reshape(N_EXPERTS, 1, -1))
        h = _combine(dest.reshape(-1), ys, h, probs.T, mod3, final_g.reshape(1, d), seq,
                     final_norm=(l == depth - 1))
    return h.reshape(batch, seq, d)
```

```python
import functools
import math

import numpy as np
import jax
import jax.numpy as jnp
from jax import lax
from jax.experimental import pallas as pl
from jax.experimental.pallas import tpu as pltpu

F32 = jnp.float32
BF16 = jnp.bfloat16
U32 = jnp.uint32
I32 = jnp.int32

HEAD_DIM = 128
N_HEADS_A = 8
N_HEADS_B = 4
DILATED_PATTERNS = ((128, 1), (512, 4), (2048, 16))
N_EXPERTS = 32
TOP_K = 4
SWIGLU_LIMIT = 7.0
SWIGLU_ALPHA = 1.702
N_MOD = 6
EPS = 1e-5
NEG = -1e30
LOG2E = math.log2(math.e)

LANES = 128
SUBLANES = 8
V7X_VMEM_LIMIT = 56 * 1024 * 1024

ROW_BLOCK = 128
CHUNK_BLOCKS = 12
MM_ROW_BLOCKS = 4
FF_TILE = 512


def _alibi_slopes(n):
    return np.array([2.0 ** (-8.0 * (i + 1) / n) for i in range(n)], dtype=np.float32)


def _nt_dot(a, b):
    return lax.dot_general(a, b, (((1,), (1,)), ((), ())), preferred_element_type=F32)


def _split_bf16(x):
    hi = x.astype(BF16)
    lo = (x - hi.astype(F32)).astype(BF16)
    return hi, lo


def _params(sem, vmem=V7X_VMEM_LIMIT):
    return pltpu.CompilerParams(dimension_semantics=sem, vmem_limit_bytes=vmem)


def _ada_kernel(c_ref, w_ref, b_ref, o_ref):
    c = c_ref[...]
    a = c * jax.nn.sigmoid(c)
    a_hi, a_lo = _split_bf16(a)
    w_hi, w_lo = _split_bf16(w_ref[...])
    m = a.shape[0]
    both = jnp.dot(jnp.concatenate([a_hi, a_lo], axis=0), w_hi, preferred_element_type=F32)
    acc = both[:m] + both[m:] + jnp.dot(a_hi, w_lo, preferred_element_type=F32)
    o_ref[...] = acc + b_ref[...]


def _ada(c8, w, b, tn=1024):
    m, d = c8.shape
    n = w.shape[1]
    return pl.pallas_call(
        _ada_kernel,
        grid=(n // tn,),
        in_specs=[pl.BlockSpec((m, d), lambda j: (0, 0)),
                  pl.BlockSpec((d, tn), lambda j: (0, j)),
                  pl.BlockSpec((1, tn), lambda j: (0, j))],
        out_specs=pl.BlockSpec((m, tn), lambda j: (0, j)),
        out_shape=jax.ShapeDtypeStruct((m, n), F32),
        compiler_params=_params(("arbitrary",)),
        name="ada",
    )(c8, w, b)


def _norm_mod_kernel(x_ref, g_ref, sc_ref, sh_ref, o_ref):
    x = x_ref[...]
    y = x * lax.rsqrt(jnp.mean(x * x, axis=-1, keepdims=True) + EPS) * g_ref[...]
    o_ref[...] = (y * (1.0 + sc_ref[...]) + sh_ref[...]).astype(o_ref.dtype)


def _norm_mod(x2, g, mod3, i_scale, i_shift, seq, tm=1024):
    t, d = x2.shape
    per_b = seq // tm
    return pl.pallas_call(
        _norm_mod_kernel,
        grid=(t // tm,),
        in_specs=[pl.BlockSpec((tm, d), lambda i: (i, 0)),
                  pl.BlockSpec((1, d), lambda i: (0, 0)),
                  pl.BlockSpec((None, 1, d), lambda i: ((i // per_b) * N_MOD + i_scale, 0, 0)),
                  pl.BlockSpec((None, 1, d), lambda i: ((i // per_b) * N_MOD + i_shift, 0, 0))],
        out_specs=pl.BlockSpec((tm, d), lambda i: (i, 0)),
        out_shape=jax.ShapeDtypeStruct((t, d), BF16),
        compiler_params=_params(("arbitrary",)),
        name="norm1",
    )(x2, g, mod3, mod3)


def _inproj_kernel(u_ref, w_ref, o_ref, *, q_tiles, scale):
    n = pl.program_id(0)
    acc = jnp.dot(u_ref[...], w_ref[...].astype(BF16), preferred_element_type=F32)
    is_q = functools.reduce(jnp.logical_or, [n == q for q in q_tiles])
    o_ref[...] = (acc * jnp.where(is_q, scale, 1.0)).astype(BF16)


def _inproj(u, w, q_tiles, tm=2048, tn=1024):
    t, d = u.shape
    n = w.shape[1]
    return pl.pallas_call(
        functools.partial(_inproj_kernel, q_tiles=q_tiles, scale=HEAD_DIM ** -0.5 * LOG2E),
        grid=(n // tn, t // tm),
        in_specs=[pl.BlockSpec((tm, d), lambda j, i: (i, 0)),
                  pl.BlockSpec((d, tn), lambda j, i: (0, j))],
        out_specs=pl.BlockSpec((tm, tn), lambda j, i: (i, j)),
        out_shape=jax.ShapeDtypeStruct((t, n), BF16),
        compiler_params=_params(("arbitrary", "arbitrary")),
        name="inproj",
    )(u, w)


def _dil_tables(tq, span):
    nside = span // tq
    o = lax.broadcasted_iota(I32, (2 * nside + 1, tq, tq), 0) - nside
    i = lax.broadcasted_iota(I32, (2 * nside + 1, tq, tq), 1)
    j = lax.broadcasted_iota(I32, (2 * nside + 1, tq, tq), 2)
    ad = jnp.abs(o * tq + j - i)
    mult = jnp.zeros_like(ad)
    for window, dil in DILATED_PATTERNS:
        mult += ((ad % dil == 0) & (ad // dil <= window // (2 * dil))).astype(I32)
    lm = jnp.where(mult > 0, jnp.log2(jnp.maximum(mult, 1).astype(F32)), NEG)
    return ad.astype(F32) * LOG2E, lm


def _dil_kernel(slopes_ref, q_ref, k_ref, v_ref, ad_ref, lm_ref, o_ref, bias_ref, vt_ref, st_ref,
                *, tq, nside, nq):
    h = pl.program_id(1)
    i = pl.program_id(2)

    @pl.when(i == 0)
    def _():
        bias_ref[:2 * nside + 1] = lm_ref[...] - slopes_ref[h] * ad_ref[...]
        bias_ref[2 * nside + 1] = jnp.full((tq, tq), NEG, F32)
        for kb in range(nq):
            vt_ref[kb] = v_ref[kb * tq:(kb + 1) * tq, :].astype(F32).T.astype(BF16)

    nwin = 2 * nside + 1
    b0 = jnp.clip(i - nside, 0, nq - nwin)
    q = q_ref[...]

    def score_block(j, m):
        o = b0 + j - i
        plane = jnp.where(jnp.abs(o) <= nside, nside - o, nwin)
        kb = k_ref[pl.ds(pl.multiple_of((b0 + j) * tq, tq), tq), :]
        s = _nt_dot(kb, q) + bias_ref[plane]
        st_ref[j] = s
        return jnp.maximum(m, jnp.max(s, axis=0, keepdims=True))

    m = lax.fori_loop(0, nwin, score_block, jnp.full((1, tq), NEG, F32), unroll=True)
    l = acc = None
    for j in range(nwin):
        p = jnp.exp2(st_ref[j] - m)
        cl = jnp.sum(p, axis=0, keepdims=True)
        ca = jnp.dot(vt_ref[b0 + j], p.astype(BF16), preferred_element_type=F32)
        l = cl if j == 0 else l + cl
        acc = ca if j == 0 else acc + ca
    o_ref[...] = (acc / l).T.astype(o_ref.dtype)


def _dil_attention(proj, slopes, batch, seq, q_col, k_col, v_col, tq=512):
    span = max(w // 2 for w, _ in DILATED_PATTERNS)
    nside = span // tq
    nq = seq // tq
    ad, lm = _dil_tables(tq, span)
    tab_spec = pl.BlockSpec((2 * nside + 1, tq, tq), lambda b, h, i, s: (0, 0, 0))
    kern = functools.partial(_dil_kernel, tq=tq, nside=nside, nq=nq)
    return pl.pallas_call(
        kern,
        grid_spec=pltpu.PrefetchScalarGridSpec(
            num_scalar_prefetch=1,
            grid=(batch, N_HEADS_A, nq),
            in_specs=[pl.BlockSpec((tq, HEAD_DIM), lambda b, h, i, s: (b * nq + i, q_col + h)),
                      pl.BlockSpec((seq, HEAD_DIM), lambda b, h, i, s: (b, k_col + h)),
                      pl.BlockSpec((seq, HEAD_DIM), lambda b, h, i, s: (b, v_col + h)),
                      tab_spec, tab_spec],
            out_specs=pl.BlockSpec((tq, HEAD_DIM), lambda b, h, i, s: (b * nq + i, h)),
            scratch_shapes=[pltpu.VMEM((2 * nside + 2, tq, tq), F32),
                            pltpu.VMEM((nq, HEAD_DIM, tq), BF16),
                            pltpu.VMEM((2 * nside + 1, tq, tq), F32)]),
        out_shape=jax.ShapeDtypeStruct((batch * seq, N_HEADS_A * HEAD_DIM), BF16),
        compiler_params=_params(("arbitrary", "arbitrary", "arbitrary")),
        name="dil_attn",
    )(slopes, proj, proj, proj, ad, lm)


def _diff_kernel(slopes_ref, lq1_ref, lk1_ref, lq2_ref, lk2_ref, g_ref, q_ref, k_ref, v_ref, o_ref,
                 vt_ref, bias_ref, st1_ref, st2_ref, *, tq, tk, nk, lambda_init):
    h = pl.program_id(1)
    i = pl.program_id(2)

    seq = k_ref.shape[0]
    nq = seq // tq

    @pl.when(i == 0)
    def _():
        for kb in range(nk):
            cols = slice(kb * tk, (kb + 1) * tk)
            vt_ref[:, cols] = v_ref[cols, :].astype(F32).T.astype(BF16)

        neg_c = -slopes_ref[h] * LOG2E
        base = (lax.broadcasted_iota(I32, (tq, tq), 0) - lax.broadcasted_iota(I32, (tq, tq), 1)
                - (seq - tq))

        def fill(r, carry):
            rows = pl.ds(pl.multiple_of(r * tq, tq), tq)
            bias_ref[rows, :] = jnp.abs(base + r * tq).astype(F32) * neg_c
            return carry

        lax.fori_loop(0, 2 * nq - 1, fill, 0)

    q = q_ref[...]
    qs = (q[:, :HEAD_DIM], q[:, HEAD_DIM:])
    st_refs = (st1_ref, st2_ref)
    w0 = (nq - 1 - i) * tq
    def score_chunk(c, mx):
        rows = pl.ds(pl.multiple_of(c * tk, tk), tk)
        kc = k_ref[rows, :]
        b = bias_ref[pl.ds(pl.multiple_of(w0 + c * tk, tq), tk), :]
        out = []
        for j in range(2):
            s = _nt_dot(kc[:, j * HEAD_DIM:(j + 1) * HEAD_DIM], qs[j]) + b
            st_refs[j][rows, :] = s
            out.append(jnp.maximum(mx[j], jnp.max(s, axis=0, keepdims=True)))
        return tuple(out)

    neg = jnp.full((1, tq), NEG, F32)
    mx = lax.fori_loop(0, nk, score_chunk, (neg, neg), unroll=4)
    ls = [None, None]
    accs = [None, None]
    for c in range(nk):
        rows = slice(c * tk, (c + 1) * tk)
        vt = vt_ref[:, rows]
        for j in range(2):
            p = jnp.exp2(st_refs[j][rows, :] - mx[j])
            cl = jnp.sum(p, axis=0, keepdims=True)
            ca = jnp.dot(vt, p.astype(BF16), preferred_element_type=F32)
            ls[j] = cl if c == 0 else ls[j] + cl
            accs[j] = ca if c == 0 else accs[j] + ca
    (a1, a2), (l1, l2) = accs, ls

    lam = (jnp.exp(jnp.sum(lq1_ref[...] * lk1_ref[...], axis=-1, keepdims=True))
           - jnp.exp(jnp.sum(lq2_ref[...] * lk2_ref[...], axis=-1, keepdims=True)) + lambda_init)
    ot = a1 / l1 - lam * (a2 / l2)
    yt = ot * lax.rsqrt(jnp.mean(ot * ot, axis=0, keepdims=True) + EPS)
    o_ref[...] = (yt.T * (g_ref[...] * (1.0 - lambda_init))).astype(o_ref.dtype)


def _diff_attention(proj, slopes, lam_vecs, subln_g, lambda_init, batch, seq, q_col, k_col, v_col,
                    tq=512, tk=512):
    nq = seq // tq
    w = 2 * HEAD_DIM
    vec_spec = pl.BlockSpec((1, HEAD_DIM), lambda b, h, i, s: (0, 0))
    kern = functools.partial(_diff_kernel, tq=tq, tk=tk, nk=seq // tk, lambda_init=lambda_init)
    return pl.pallas_call(
        kern,
        grid_spec=pltpu.PrefetchScalarGridSpec(
            num_scalar_prefetch=1,
            grid=(batch, N_HEADS_B, nq),
            in_specs=[vec_spec, vec_spec, vec_spec, vec_spec,
                      pl.BlockSpec((1, w), lambda b, h, i, s: (0, 0)),
                      pl.BlockSpec((tq, w), lambda b, h, i, s: (b * nq + i, q_col + h)),
                      pl.BlockSpec((seq, w), lambda b, h, i, s: (b, k_col + h)),
                      pl.BlockSpec((seq, w), lambda b, h, i, s: (b, v_col + h))],
            out_specs=pl.BlockSpec((tq, w), lambda b, h, i, s: (b * nq + i, h)),
            scratch_shapes=[pltpu.VMEM((w, seq), BF16),
                            pltpu.VMEM((2 * seq - tq, tq), F32),
                            pltpu.VMEM((seq, tq), F32),
                            pltpu.VMEM((seq, tq), F32)]),
        out_shape=jax.ShapeDtypeStruct((batch * seq, N_HEADS_B * w), BF16),
        compiler_params=_params(("arbitrary", "arbitrary", "arbitrary")),
        name="diff_attn",
    )(slopes, *lam_vecs, subln_g, proj, proj, proj)


def _merge_kernel(ya_ref, yb_ref, ga_ref, gb_ref, x_ref, g1_ref, sc2_ref, sh2_ref, n2g_ref,
                  woa_ref, wob_ref, wo_ref, wrh_ref, wrl_ref, br_ref,
                  h_ref, u2p_ref, lg_ref):
    a = jnp.dot(ya_ref[...], woa_ref[...], preferred_element_type=F32)
    b = jnp.dot(yb_ref[...], wob_ref[...], preferred_element_type=F32)
    merged = (jax.nn.sigmoid(ga_ref[...].astype(F32)) * a
              + jax.nn.sigmoid(gb_ref[...].astype(F32)) * b)
    h = x_ref[...] + g1_ref[...] * jnp.dot(merged.astype(BF16), wo_ref[...],
                                           preferred_element_type=F32)
    h_ref[...] = h
    y = h * lax.rsqrt(jnp.mean(h * h, axis=-1, keepdims=True) + EPS) * n2g_ref[...]
    u2 = y * (1.0 + sc2_ref[...]) + sh2_ref[...]
    hi, lo = _split_bf16(u2)
    lg = _nt_dot(wrh_ref[...], hi) + _nt_dot(wrl_ref[...], hi) + _nt_dot(wrh_ref[...], lo)
    lg_ref[...] = lg + br_ref[...]
    bits = pltpu.bitcast(hi.astype(F32), U32)
    half = bits.shape[1] // 2
    u2p_ref[...] = (bits[:, :half] >> 16) | (bits[:, half:] & jnp.uint32(0xFFFF0000))


def _merge(ya, yb, proj, x2, mod3, n2g, woa, wob, wo, wr_hi, wr_lo, br, seq, ga_col, gb_col, tm=512):
    t, d = x2.shape
    per_b = seq // tm
    wa = ya.shape[1]
    wb = yb.shape[1]
    ne = wr_hi.shape[0]

    def mod_spec(idx):
        return pl.BlockSpec((None, 1, d), lambda i: ((i // per_b) * N_MOD + idx, 0, 0))

    def const_spec(shape):
        return pl.BlockSpec(shape, lambda i: (0,) * len(shape), pipeline_mode=pl.Buffered(1))

    return pl.pallas_call(
        _merge_kernel,
        grid=(t // tm,),
        in_specs=[pl.BlockSpec((tm, wa), lambda i: (i, 0)),
                  pl.BlockSpec((tm, wb), lambda i: (i, 0)),
                  pl.BlockSpec((tm, d), lambda i: (i, ga_col)),
                  pl.BlockSpec((tm, d), lambda i: (i, gb_col)),
                  pl.BlockSpec((tm, d), lambda i: (i, 0)),
                  mod_spec(2), mod_spec(4), mod_spec(3),
                  const_spec((1, d)),
                  const_spec((wa, d)), const_spec((wb, d)), const_spec((d, d)),
                  const_spec((ne, d)), const_spec((ne, d)), const_spec((ne, 1))],
        out_specs=[pl.BlockSpec((tm, d), lambda i: (i, 0)),
                   pl.BlockSpec((tm, d // 2), lambda i: (i, 0)),
                   pl.BlockSpec((ne, tm), lambda i: (0, i))],
        out_shape=[jax.ShapeDtypeStruct((t, d), F32),
                   jax.ShapeDtypeStruct((t, d // 2), U32),
                   jax.ShapeDtypeStruct((ne, t), F32)],
        compiler_params=_params(("arbitrary",)),
        name="merge",
    )(ya, yb, proj, proj, x2, mod3, mod3, mod3, n2g, woa, wob, wo, wr_hi, wr_lo, br)


def _route_kernel(lg_ref, p_ref, dest_ref, cnt_ref, carry_ref, total_ref):
    sweep = pl.program_id(0)

    @pl.when(pl.program_id(1) == 0)
    def _():
        @pl.when(sweep == 0)
        def _():
            total_ref[...] = jnp.zeros_like(total_ref)

        @pl.when(sweep == 1)
        def _():
            total_ref[...] = carry_ref[...]

        carry_ref[...] = jnp.zeros_like(carry_ref)

    lg = lg_ref[...]
    ne, tr = lg.shape
    eio = lax.broadcasted_iota(I32, (ne, tr), 0)
    work = lg
    vals, hots = [], []
    for k in range(TOP_K):
        mx = jnp.max(work, axis=0, keepdims=True)
        am = jnp.min(jnp.where(work == mx, eio, ne), axis=0, keepdims=True)
        hot = eio == am
        vals.append(mx)
        hots.append(hot)
        work = jnp.where(hot, -jnp.inf, work)
    exps = [jnp.exp(v - vals[0]) for v in vals]
    denom = functools.reduce(jnp.add, exps)
    for k in range(TOP_K):
        p_ref[k:k + 1, :] = exps[k] / denom
    chosen = functools.reduce(jnp.logical_or, hots)
    sel = jnp.where(chosen, 1.0, 0.0)
    tri = (lax.broadcasted_iota(I32, (tr, tr), 0) < lax.broadcasted_iota(I32, (tr, tr), 1))
    before = jnp.dot(sel.astype(BF16), jnp.where(tri, 1.0, 0.0).astype(BF16),
                     preferred_element_type=F32)
    carry = carry_ref[...]
    nblk = jnp.floor((total_ref[...][:, 0:1] + (ROW_BLOCK - 1)) * (1.0 / ROW_BLOCK))
    e_row = lax.broadcasted_iota(I32, (ne, ne), 0)
    e_col = lax.broadcasted_iota(I32, (ne, ne), 1)
    nblk_lanes = jnp.sum(jnp.where(e_row == e_col, nblk, 0.0), axis=0, keepdims=True)
    first_row = jnp.sum(jnp.where(e_col < e_row, nblk_lanes, 0.0), axis=1,
                        keepdims=True) * ROW_BLOCK
    place = before + (carry[:, 0:1] + first_row)
    for k in range(TOP_K):
        dest_ref[k:k + 1, :] = jnp.sum(jnp.where(hots[k], place, 0.0), axis=0,
                                       keepdims=True).astype(I32)
    carry = carry + jnp.sum(sel, axis=1, keepdims=True)
    carry_ref[...] = carry
    cnt_ref[...] = carry.astype(I32)


def _route(logits_t, tr=1024):
    ne, t = logits_t.shape
    slot_spec = pl.BlockSpec((None, TOP_K, tr), lambda s, i: (s, 0, i))
    probs, dest, cnt = pl.pallas_call(
        _route_kernel,
        grid=(2, t // tr),
        in_specs=[pl.BlockSpec((ne, tr), lambda s, i: (0, i))],
        out_specs=[slot_spec, slot_spec,
                   pl.BlockSpec((None, ne, LANES), lambda s, i: (s, 0, 0))],
        out_shape=[jax.ShapeDtypeStruct((2, TOP_K, t), F32),
                   jax.ShapeDtypeStruct((2, TOP_K, t), I32),
                   jax.ShapeDtypeStruct((2, ne, LANES), I32)],
        scratch_shapes=[pltpu.VMEM((ne, LANES), F32), pltpu.VMEM((ne, LANES), F32)],
        compiler_params=_params(("arbitrary", "arbitrary")),
        name="route",
    )(logits_t)
    return probs[1], dest[1], cnt[1]


def _dispatch_kernel(dest_ref, fill_ref, nfill_ref, src_ref, dst_ref, sem, fill_sem, *, tm, t_total):
    i = pl.program_id(0)

    @pl.when(i == 0)
    def _():
        def fill(b):
            rows = pl.ds(pl.multiple_of(fill_ref[b] * ROW_BLOCK, ROW_BLOCK), ROW_BLOCK)
            return pltpu.make_async_copy(src_ref.at[pl.ds(0, ROW_BLOCK)], dst_ref.at[rows], fill_sem)

        def fill_start(b, c):
            fill(b).start()
            return c

        def fill_wait(b, c):
            fill(b).wait()
            return c

        lax.fori_loop(0, nfill_ref[0], fill_start, 0)
        lax.fori_loop(0, nfill_ref[0], fill_wait, 0)

    for k in range(TOP_K):
        def group(g, c):
            j0 = pl.multiple_of(g * SUBLANES, SUBLANES)
            tile = src_ref.at[pl.ds(j0, SUBLANES)]
            for u in range(SUBLANES):
                row = dest_ref[k * t_total + i * tm + j0 + u]
                pltpu.make_async_copy(tile.at[pl.ds(u, 1)], dst_ref.at[pl.ds(row, 1)], sem).start()
            return c
        lax.fori_loop(0, tm // SUBLANES, group, 0)
    for k in range(TOP_K):
        pltpu.make_async_copy(src_ref, dst_ref.at[pl.ds(0, tm)], sem).wait()


def _dispatch(dest_flat, fill_blocks, nfill, u2p, total_rows, tm=1024):
    t, w = u2p.shape
    kern = functools.partial(_dispatch_kernel, tm=tm, t_total=t)
    return pl.pallas_call(
        kern,
        grid_spec=pltpu.PrefetchScalarGridSpec(
            num_scalar_prefetch=3,
            grid=(t // tm,),
            in_specs=[pl.BlockSpec((tm, w), lambda i, d, fb, nf: (i, 0))],
            out_specs=pl.BlockSpec(memory_space=pl.ANY),
            scratch_shapes=[pltpu.SemaphoreType.DMA(()), pltpu.SemaphoreType.DMA(())]),
        out_shape=jax.ShapeDtypeStruct((total_rows, w), U32),
        compiler_params=_params(("arbitrary",)),
        name="dispatch",
    )(dest_flat, fill_blocks, nfill, u2p)


def _moe_kernel(che_ref, chblk_ref, chn_ref, nvalid_ref,
                xs_ref, wg_ref, wu_ref, wd_ref, bg_ref, bu_ref, bd_ref,
                ys_ref,
                xin_ref, xbf_ref, yacc_ref, act_ref, in_sem, out_sem,
                *, nff, nsplit, mm_rows):
    c = pl.program_id(0)
    f = pl.program_id(1)
    nvalid = nvalid_ref[0]
    half = xin_ref.shape[1]

    def in_copy(cc, j):
        row = pl.multiple_of((chblk_ref[cc] + j) * ROW_BLOCK, ROW_BLOCK)
        return pltpu.make_async_copy(xs_ref.at[pl.ds(row, ROW_BLOCK)],
                                     xin_ref.at[pl.ds(j * ROW_BLOCK, ROW_BLOCK)], in_sem)

    def out_copy(cc, j):
        row = pl.multiple_of((chblk_ref[cc] + j) * ROW_BLOCK, ROW_BLOCK)
        return pltpu.make_async_copy(yacc_ref.at[pl.ds(j * ROW_BLOCK, ROW_BLOCK)],
                                     ys_ref.at[pl.ds(row, ROW_BLOCK)], out_sem)

    def for_blocks(cc, fn):
        def one(j, carry):
            fn(cc, j)
            return carry
        lax.fori_loop(0, chn_ref[cc], one, 0)

    def chunk_step():
        @pl.when(f == 0)
        def _():
            @pl.when(c == 0)
            def _():
                xbf_ref[...] = jnp.zeros_like(xbf_ref)
                for_blocks(c, lambda cc, j: in_copy(cc, j).start())

            for_blocks(c, lambda cc, j: in_copy(cc, j).wait())

            def unpack(cc, j):
                rows = pl.ds(pl.multiple_of(j * ROW_BLOCK, ROW_BLOCK), ROW_BLOCK)
                w = xin_ref[rows, :]
                xbf_ref[rows, :half] = pltpu.bitcast(w << 16, F32).astype(BF16)
                xbf_ref[rows, half:] = pltpu.bitcast(w & jnp.uint32(0xFFFF0000), F32).astype(BF16)

            for_blocks(c, unpack)

            @pl.when(c + 1 < nvalid)
            def _():
                for_blocks(c + 1, lambda cc, j: in_copy(cc, j).start())

        mm_blocks = mm_rows // ROW_BLOCK
        rest = jnp.maximum(chn_ref[c] - mm_blocks, 0)
        n_full = rest // mm_blocks
        rem = rest - n_full * mm_blocks
        tail0 = pl.multiple_of((1 + n_full) * mm_rows, ROW_BLOCK)

        def for_tail(fn):
            for r in range(1, mm_blocks):
                pl.when(rem == r)(functools.partial(fn, pl.ds(tail0, r * ROW_BLOCK)))

        def gate_up(rows):
            x = xbf_ref[rows, :]
            g = jnp.minimum(jnp.dot(x, wg_ref[...].astype(BF16), preferred_element_type=F32)
                            + bg_ref[...], SWIGLU_LIMIT)
            u = jnp.clip(jnp.dot(x, wu_ref[...].astype(BF16), preferred_element_type=F32)
                         + bu_ref[...], -SWIGLU_LIMIT, SWIGLU_LIMIT)
            act_ref[rows, :] = ((u + 1.0) * (g * jax.nn.sigmoid(SWIGLU_ALPHA * g))).astype(BF16)

        def gate_up_body(rb, carry):
            gate_up(pl.ds(pl.multiple_of(rb * mm_rows, mm_rows), mm_rows))
            return carry

        lax.fori_loop(0, 1 + n_full, gate_up_body, 0)
        for_tail(gate_up)

        @pl.when((f == 0) & (c > 0))
        def _():
            for_blocks(c - 1, lambda cc, j: out_copy(cc, j).wait())

        first = f == 0
        ncol = yacc_ref.shape[1] // nsplit

        def down(rows):
            a = act_ref[rows, :]
            for s in range(nsplit):
                cols = slice(s * ncol, (s + 1) * ncol)
                part = jnp.dot(a, wd_ref[:, cols].astype(BF16), preferred_element_type=F32)
                base = jnp.where(first, jnp.broadcast_to(bd_ref[:, cols], part.shape),
                                 yacc_ref[rows, cols])
                yacc_ref[rows, cols] = base + part

        def down_body(rb, carry):
            down(pl.ds(pl.multiple_of(rb * mm_rows, mm_rows), mm_rows))
            return carry

        lax.fori_loop(0, 1 + n_full, down_body, 0)
        for_tail(down)

        @pl.when(f == nff - 1)
        def _():
            for_blocks(c, lambda cc, j: out_copy(cc, j).start())

            @pl.when(c == nvalid - 1)
            def _():
                for_blocks(c, lambda cc, j: out_copy(cc, j).wait())

                def fill(b):
                    rows = pl.ds(pl.multiple_of(b * ROW_BLOCK, ROW_BLOCK), ROW_BLOCK)
                    return pltpu.make_async_copy(yacc_ref.at[pl.ds(0, ROW_BLOCK)], ys_ref.at[rows],
                                                 out_sem)

                def fill_start(b, carry):
                    fill(b).start()
                    return carry

                def fill_wait(b, carry):
                    fill(b).wait()
                    return carry

                nused = chblk_ref[c] + chn_ref[c]
                ntotal = ys_ref.shape[0] // ROW_BLOCK
                lax.fori_loop(nused, ntotal, fill_start, 0)
                lax.fori_loop(nused, ntotal, fill_wait, 0)

    chunk_step()


def _moe(che, chblk, chn, nvalid, xs, wg, wu, wd, bg, bu, bd):
    ne, d, dff = wg.shape
    rows = CHUNK_BLOCKS * ROW_BLOCK
    nff = dff // FF_TILE

    kern = functools.partial(_moe_kernel, nff=nff, nsplit=4, mm_rows=MM_ROW_BLOCKS * ROW_BLOCK)
    return pl.pallas_call(
        kern,
        grid_spec=pltpu.PrefetchScalarGridSpec(
            num_scalar_prefetch=4,
            grid=(nvalid[0], nff),
            in_specs=[pl.BlockSpec(memory_space=pl.ANY),
                      pl.BlockSpec((None, d, FF_TILE), lambda c, f, e, b, n, nv: (e[c], 0, f)),
                      pl.BlockSpec((None, d, FF_TILE), lambda c, f, e, b, n, nv: (e[c], 0, f)),
                      pl.BlockSpec((None, FF_TILE, d), lambda c, f, e, b, n, nv: (e[c], f, 0)),
                      pl.BlockSpec((None, 1, FF_TILE), lambda c, f, e, b, n, nv: (e[c], 0, f)),
                      pl.BlockSpec((None, 1, FF_TILE), lambda c, f, e, b, n, nv: (e[c], 0, f)),
                      pl.BlockSpec((None, 1, d), lambda c, f, e, b, n, nv: (e[c], 0, 0))],
            out_specs=pl.BlockSpec(memory_space=pl.ANY),
            scratch_shapes=[pltpu.VMEM((rows, d // 2), U32),
                            pltpu.VMEM((rows, d), BF16),
                            pltpu.VMEM((rows, d), F32),
                            pltpu.VMEM((rows, FF_TILE), BF16),
                            pltpu.SemaphoreType.DMA(()),
                            pltpu.SemaphoreType.DMA(())]),
        out_shape=jax.ShapeDtypeStruct((xs.shape[0], d), F32),
        compiler_params=_params(("arbitrary", "arbitrary")),
        name="moe",
    )(che, chblk, chn, nvalid, xs, wg, wu, wd, bg, bu, bd)


def _combine_kernel(dest_ref, ys_ref, h_ref, p_ref, g2_ref, fg_ref, o_ref, buf_ref, sem,
                    *, tc, nsteps, final_norm):
    i = pl.program_id(0)
    t_total = nsteps * tc

    def issue(step, slot):
        for k in range(TOP_K):
            def group(g, c):
                j0 = pl.multiple_of(g * SUBLANES, SUBLANES)
                tile = buf_ref.at[slot, k, pl.ds(j0, SUBLANES)]
                for u in range(SUBLANES):
                    src = dest_ref[k * t_total + step * tc + j0 + u]
                    pltpu.make_async_copy(ys_ref.at[pl.ds(src, 1)], tile.at[pl.ds(u, 1)],
                                          sem.at[slot]).start()
                return c
            lax.fori_loop(0, tc // SUBLANES, group, 0)

    def drain(slot):
        for k in range(TOP_K):
            pltpu.make_async_copy(ys_ref.at[pl.ds(0, tc)], buf_ref.at[slot, k], sem.at[slot]).wait()

    slot = i % 2

    @pl.when(i == 0)
    def _():
        issue(0, 0)

    @pl.when(i + 1 < nsteps)
    def _():
        issue(i + 1, 1 - slot)

    drain(slot)
    p = p_ref[...]
    moe = p[:, 0:1] * buf_ref[slot, 0]
    for k in range(1, TOP_K):
        moe += p[:, k:k + 1] * buf_ref[slot, k]
    h = h_ref[...] + g2_ref[...] * moe
    if final_norm:
        h = h * lax.rsqrt(jnp.mean(h * h, axis=-1, keepdims=True) + EPS) * fg_ref[...]
    o_ref[...] = h


def _combine(dest_flat, ys, h, probs_t, mod3, final_g, seq, final_norm, tc=256):
    t, d = h.shape
    per_b = seq // tc
    nsteps = t // tc
    kern = functools.partial(_combine_kernel, tc=tc, nsteps=nsteps, final_norm=final_norm)
    return pl.pallas_call(
        kern,
        grid_spec=pltpu.PrefetchScalarGridSpec(
            num_scalar_prefetch=1,
            grid=(nsteps,),
            in_specs=[pl.BlockSpec(memory_space=pl.ANY),
                      pl.BlockSpec((tc, d), lambda i, s: (i, 0)),
                      pl.BlockSpec((tc, TOP_K), lambda i, s: (i, 0)),
                      pl.BlockSpec((None, 1, d), lambda i, s: ((i // per_b) * N_MOD + 5, 0, 0)),
                      pl.BlockSpec((1, d), lambda i, s: (0, 0))],
            out_specs=pl.BlockSpec((tc, d), lambda i, s: (i, 0)),
            scratch_shapes=[pltpu.VMEM((2, TOP_K, tc, d), F32),
                            pltpu.SemaphoreType.DMA((2,))]),
        out_shape=jax.ShapeDtypeStruct((t, d), F32),
        compiler_params=_params(("arbitrary",)),
        name="combine",
    )(dest_flat, ys, h, probs_t, mod3, final_g)


def _routing_plan(counts, t):
    nblk_total = -(-t * TOP_K // ROW_BLOCK) + N_EXPERTS
    nblk = (counts + ROW_BLOCK - 1) // ROW_BLOCK
    blk_end = jnp.cumsum(nblk)
    blk_start = blk_end - nblk
    ntail_max = nblk_total - t * TOP_K // ROW_BLOCK
    fill_blocks = jnp.concatenate([jnp.maximum(blk_end - 1, 0),
                                   jnp.minimum(blk_end[-1] + jnp.arange(ntail_max), nblk_total - 1)])
    nfill = N_EXPERTS + nblk_total - blk_end[-1]
    max_chunks = -(-nblk_total // CHUNK_BLOCKS) + N_EXPERTS
    nch = (nblk + CHUNK_BLOCKS - 1) // CHUNK_BLOCKS
    ch_end = jnp.cumsum(nch)
    ch_start = ch_end - nch
    nvalid = ch_end[-1]
    cid = jnp.minimum(jnp.arange(max_chunks, dtype=I32), nvalid - 1)
    che =jnp.clip(jnp.searchsorted(ch_end, cid, side='right'), 0, N_EXPERTS - 1).astype(I32)
    local = cid - ch_start[che]
    nch_e = jnp.maximum(nch[che], 1)
    n_mm = (nblk[che] + MM_ROW_BLOCKS - 1) // MM_ROW_BLOCKS
    size = n_mm // nch_e
    extra = n_mm % nch_e
    first_mm = local * size + jnp.minimum(local, extra)
    chblk = (blk_start[che] + first_mm * MM_ROW_BLOCKS).astype(I32)
    chn = jnp.minimum((size + (local < extra)) * MM_ROW_BLOCKS,
                      nblk[che] - first_mm * MM_ROW_BLOCKS).astype(I32)
    return (fill_blocks.astype(I32), nfill.astype(I32).reshape(1), che, chblk, chn,
            nvalid.astype(I32).reshape(1), nblk_total * ROW_BLOCK)


def kernel(x, c, w_ada, b_ada, norm1_g, w_in, lam_q1, lam_k1, lam_q2, lam_k2, subln_g, w_out_a, w_out_b, w_o, norm2_g, w_router, b_router, w_gate, b_gate, w_up, b_up, w_down, b_down, final_g):
    batch, seq, d = x.shape
    t = batch * seq
    depth = w_ada.shape[0]
    wa = N_HEADS_A * HEAD_DIM
    wb = N_HEADS_B * 2 * HEAD_DIM
    slopes = _alibi_slopes(N_HEADS_A + N_HEADS_B)
    slopes_a = jnp.asarray(slopes[:N_HEADS_A])
    slopes_b = jnp.asarray(slopes[N_HEADS_A:])
    c8 = jnp.pad(c, ((0, 8 - batch), (0, 0)))

    h = x.reshape(t, d)
    for l in range(depth):
        lambda_init = 0.8 - 0.6 * math.exp(-0.3 * l)
        mod = _ada(c8, w_ada[l], b_ada[l].reshape(1, -1))[:batch]
        mod3 = mod.reshape(batch * N_MOD, 1, d)

        u = _norm_mod(h, norm1_g[l].reshape(1, d), mod3, 1, 0, seq)
        proj = _inproj(u, w_in[l], q_tiles=(0, 3 * wa // 1024))
        y_a = _dil_attention(proj, slopes_a, batch, seq, 0, wa // HEAD_DIM, 2 * wa // HEAD_DIM)
        off_b = 3 * wa // (2 * HEAD_DIM)
        nb = wb // (2 * HEAD_DIM)
        y_b = _diff_attention(proj, slopes_b,
                              [v[l].reshape(1, HEAD_DIM) for v in (lam_q1, lam_k1, lam_q2, lam_k2)],
                              subln_g[l].reshape(1, -1), lambda_init, batch, seq,
                              off_b, off_b + nb, off_b + 2 * nb)
        wr_hi, wr_lo = _split_bf16(w_router[l].T)
        gate_col = (3 * wa + 3 * wb) // d
        h, u2p, logits_t = _merge(y_a, y_b, proj, h, mod3, norm2_g[l].reshape(1, d),
                                  w_out_a[l].astype(BF16), w_out_b[l].astype(BF16),
                                  w_o[l].astype(BF16), wr_hi, wr_lo, b_router[l].reshape(-1, 1),
                                  seq, gate_col, gate_col + 1)
        probs, dest, cnt = _route(logits_t)
        (fill_blocks, nfill, che, chblk, chn, nvalid, total_rows) = _routing_plan(cnt[:, 0], t)
        xs = _dispatch(dest.reshape(-1), fill_blocks, nfill, u2p, total_rows)
        ys = _moe(che, chblk, chn, nvalid, xs, w_gate[l], w_up[l], w_down[l],
                  b_gate[l].reshape(N_EXPERTS, 1, -1), b_up[l].reshape(N_EXPERTS, 1, -1),
                  b_down[l].reshape(N_EXPERTS, 1, -1))
        h = _combine(dest.reshape(-1), ys, h, probs.T, mod3, final_g.reshape(1, d), seq,
                     final_norm=(l == depth - 1))
    return h.reshape(batch, seq, d)
```

```python
import functools
import math

import numpy as np
import jax
import jax.numpy as jnp
from jax import lax
from jax.experimental import pallas as pl
from jax.experimental.pallas import tpu as pltpu

F32 = jnp.float32
BF16 = jnp.bfloat16
U32 = jnp.uint32
I32 = jnp.int32

HEAD_DIM = 128
N_HEADS_A = 8
N_HEADS_B = 4
DILATED_PATTERNS = ((128, 1), (512, 4), (2048, 16))
N_EXPERTS = 32
TOP_K = 4
SWIGLU_LIMIT = 7.0
SWIGLU_ALPHA = 1.702
N_MOD = 6
EPS = 1e-5
NEG = -1e30
LOG2E = math.log2(math.e)

LANES = 128
SUBLANES = 8
V7X_VMEM_LIMIT = 56 * 1024 * 1024

ROW_BLOCK = 128
CHUNK_BLOCKS = 12
MM_ROW_BLOCKS = 4
FF_TILE = 512


def _alibi_slopes(n):
    return np.array([2.0 ** (-8.0 * (i + 1) / n) for i in range(n)], dtype=np.float32)


def _nt_dot(a, b):
    return lax.dot_general(a, b, (((1,), (1,)), ((), ())), preferred_element_type=F32)


def _split_bf16(x):
    hi = x.astype(BF16)
    lo = (x - hi.astype(F32)).astype(BF16)
    return hi, lo


def _params(sem, vmem=V7X_VMEM_LIMIT):
    return pltpu.CompilerParams(dimension_semantics=sem, vmem_limit_bytes=vmem)


def _ada_kernel(c_ref, w_ref, b_ref, o_ref):
    c = c_ref[...]
    a = c * jax.nn.sigmoid(c)
    a_hi, a_lo = _split_bf16(a)
    w_hi, w_lo = _split_bf16(w_ref[...])
    m = a.shape[0]
    both = jnp.dot(jnp.concatenate([a_hi, a_lo], axis=0), w_hi, preferred_element_type=F32)
    acc = both[:m] + both[m:] + jnp.dot(a_hi, w_lo, preferred_element_type=F32)
    o_ref[...] = acc + b_ref[...]


def _ada(c8, w, b, tn=1024):
    m, d = c8.shape
    n = w.shape[1]
    return pl.pallas_call(
        _ada_kernel,
        grid=(n // tn,),
        in_specs=[pl.BlockSpec((m, d), lambda j: (0, 0)),
                  pl.BlockSpec((d, tn), lambda j: (0, j)),
                  pl.BlockSpec((1, tn), lambda j: (0, j))],
        out_specs=pl.BlockSpec((m, tn), lambda j: (0, j)),
        out_shape=jax.ShapeDtypeStruct((m, n), F32),
        compiler_params=_params(("arbitrary",)),
        name="ada",
    )(c8, w, b)


def _norm_mod_kernel(x_ref, g_ref, sc_ref, sh_ref, o_ref):
    x = x_ref[...]
    y = x * lax.rsqrt(jnp.mean(x * x, axis=-1, keepdims=True) + EPS) * g_ref[...]
    o_ref[...] = (y * (1.0 + sc_ref[...]) + sh_ref[...]).astype(o_ref.dtype)


def _norm_mod(x2, g, mod3, i_scale, i_shift, seq, tm=1024):
    t, d = x2.shape
    per_b = seq // tm
    return pl.pallas_call(
        _norm_mod_kernel,
        grid=(t // tm,),
        in_specs=[pl.BlockSpec((tm, d), lambda i: (i, 0)),
                  pl.BlockSpec((1, d), lambda i: (0, 0)),
                  pl.BlockSpec((None, 1, d), lambda i: ((i // per_b) * N_MOD + i_scale, 0, 0)),
                  pl.BlockSpec((None, 1, d), lambda i: ((i // per_b) * N_MOD + i_shift, 0, 0))],
        out_specs=pl.BlockSpec((tm, d), lambda i: (i, 0)),
        out_shape=jax.ShapeDtypeStruct((t, d), BF16),
        compiler_params=_params(("arbitrary",)),
        name="norm1",
    )(x2, g, mod3, mod3)


def _inproj_kernel(u_ref, w_ref, o_ref, *, q_tiles, scale):
    n = pl.program_id(0)
    acc = jnp.dot(u_ref[...], w_ref[...].astype(BF16), preferred_element_type=F32)
    is_q = functools.reduce(jnp.logical_or, [n == q for q in q_tiles])
    o_ref[...] = (acc * jnp.where(is_q, scale, 1.0)).astype(BF16)


def _inproj(u, w, q_tiles, tm=2048, tn=1024):
    t, d = u.shape
    n = w.shape[1]
    return pl.pallas_call(
        functools.partial(_inproj_kernel, q_tiles=q_tiles, scale=HEAD_DIM ** -0.5 * LOG2E),
        grid=(n // tn, t // tm),
        in_specs=[pl.BlockSpec((tm, d), lambda j, i: (i, 0)),
                  pl.BlockSpec((d, tn), lambda j, i: (0, j))],
        out_specs=pl.BlockSpec((tm, tn), lambda j, i: (i, j)),
        out_shape=jax.ShapeDtypeStruct((t, n), BF16),
        compiler_params=_params(("arbitrary", "arbitrary")),
        name="inproj",
    )(u, w)


def _dil_tables(tq, span):
    nside = span // tq
    o = lax.broadcasted_iota(I32, (2 * nside + 1, tq, tq), 0) - nside
    i = lax.broadcasted_iota(I32, (2 * nside + 1, tq, tq), 1)
    j = lax.broadcasted_iota(I32, (2 * nside + 1, tq, tq), 2)
    ad = jnp.abs(o * tq + j - i)
    mult = jnp.zeros_like(ad)
    for window, dil in DILATED_PATTERNS:
        mult += ((ad % dil == 0) & (ad // dil <= window // (2 * dil))).astype(I32)
    lm = jnp.where(mult > 0, jnp.log2(jnp.maximum(mult, 1).astype(F32)), NEG)
    return ad.astype(F32) * LOG2E, lm


def _dil_kernel(slopes_ref, q_ref, k_ref, v_ref, ad_ref, lm_ref, o_ref, bias_ref, vt_ref, st_ref,
                *, tq, nside, nq):
    h = pl.program_id(1)
    i = pl.program_id(2)

    @pl.when(i == 0)
    def _():
        bias_ref[:2 * nside + 1] = lm_ref[...] - slopes_ref[h] * ad_ref[...]
        bias_ref[2 * nside + 1] = jnp.full((tq, tq), NEG, F32)
        for kb in range(nq):
            vt_ref[kb] = v_ref[kb * tq:(kb + 1) * tq, :].astype(F32).T.astype(BF16)

    nwin = 2 * nside + 1
    b0 = jnp.clip(i - nside, 0, nq - nwin)
    q = q_ref[...]

    def score_block(j, m):
        o = b0 + j - i
        plane = jnp.where(jnp.abs(o) <= nside, nside - o, nwin)
        kb = k_ref[pl.ds(pl.multiple_of((b0 + j) * tq, tq), tq), :]
        s = _nt_dot(kb, q) + bias_ref[plane]
        st_ref[j] = s
        return jnp.maximum(m, jnp.max(s, axis=0, keepdims=True))

    m = lax.fori_loop(0, nwin, score_block, jnp.full((1, tq), NEG, F32), unroll=True)
    l = acc = None
    for j in range(nwin):
        p = jnp.exp2(st_ref[j] - m)
        cl = jnp.sum(p, axis=0, keepdims=True)
        ca = jnp.dot(vt_ref[b0 + j], p.astype(BF16), preferred_element_type=F32)
        l = cl if j == 0 else l + cl
        acc = ca if j == 0 else acc + ca
    o_ref[...] = (acc / l).T.astype(o_ref.dtype)


def _dil_attention(proj, slopes, batch, seq, q_col, k_col, v_col, tq=512):
    span = max(w // 2 for w, _ in DILATED_PATTERNS)
    nside = span // tq
    nq = seq // tq
    ad, lm = _dil_tables(tq, span)
    tab_spec = pl.BlockSpec((2 * nside + 1, tq, tq), lambda b, h, i, s: (0, 0, 0))
    kern = functools.partial(_dil_kernel, tq=tq, nside=nside, nq=nq)
    return pl.pallas_call(
        kern,
        grid_spec=pltpu.PrefetchScalarGridSpec(
            num_scalar_prefetch=1,
            grid=(batch, N_HEADS_A, nq),
            in_specs=[pl.BlockSpec((tq, HEAD_DIM), lambda b, h, i, s: (b * nq + i, q_col + h)),
                      pl.BlockSpec((seq, HEAD_DIM), lambda b, h, i, s: (b, k_col + h)),
                      pl.BlockSpec((seq, HEAD_DIM), lambda b, h, i, s: (b, v_col + h)),
                      tab_spec, tab_spec],
            out_specs=pl.BlockSpec((tq, HEAD_DIM), lambda b, h, i, s: (b * nq + i, h)),
            scratch_shapes=[pltpu.VMEM((2 * nside + 2, tq, tq), F32),
                            pltpu.VMEM((nq, HEAD_DIM, tq), BF16),
                            pltpu.VMEM((2 * nside + 1, tq, tq), F32)]),
        out_shape=jax.ShapeDtypeStruct((batch * seq, N_HEADS_A * HEAD_DIM), BF16),
        compiler_params=_params(("arbitrary", "arbitrary", "arbitrary")),
        name="dil_attn",
    )(slopes, proj, proj, proj, ad, lm)


def _diff_kernel(slopes_ref, lq1_ref, lk1_ref, lq2_ref, lk2_ref, g_ref, q_ref, k_ref, v_ref, o_ref,
                 vt_ref, bias_ref, st1_ref, st2_ref, *, tq, tk, nk, lambda_init):
    h = pl.program_id(1)
    i = pl.program_id(2)

    seq = k_ref.shape[0]
    nq = seq // tq

    @pl.when(i == 0)
    def _():
        for kb in range(nk):
            cols = slice(kb * tk, (kb + 1) * tk)
            vt_ref[:, cols] = v_ref[cols, :].astype(F32).T.astype(BF16)

        neg_c = -slopes_ref[h] * LOG2E
        base = (lax.broadcasted_iota(I32, (tq, tq), 0) - lax.broadcasted_iota(I32, (tq, tq), 1)
                - (seq - tq))

        def fill(r, carry):
            rows = pl.ds(pl.multiple_of(r * tq, tq), tq)
            bias_ref[rows, :] = jnp.abs(base + r * tq).astype(F32) * neg_c
            return carry

        lax.fori_loop(0, 2 * nq - 1, fill, 0)

    q = q_ref[...]
    qs = (q[:, :HEAD_DIM], q[:, HEAD_DIM:])
    st_refs = (st1_ref, st2_ref)
    w0 = (nq - 1 - i) * tq

    def score_chunk(c, mx):
        rows = pl.ds(pl.multiple_of(c * tk, tk), tk)
        kc = k_ref[rows, :]
        b = bias_ref[pl.ds(pl.multiple_of(w0 + c * tk, tq), tk), :]
        out = []
        for j in range(2):
            s = _nt_dot(kc[:, j * HEAD_DIM:(j + 1) * HEAD_DIM], qs[j]) + b
            st_refs[j][rows, :] = s
            out.append(jnp.maximum(mx[j], jnp.max(s, axis=0, keepdims=True)))
        return tuple(out)

    neg = jnp.full((1, tq), NEG, F32)
    mx = lax.fori_loop(0, nk, score_chunk, (neg, neg), unroll=4)
    ls = [None, None]
    accs = [None, None]
    for c in range(nk):
        rows = slice(c * tk, (c + 1) * tk)
        vt = vt_ref[:, rows]
        for j in range(2):
            p = jnp.exp2(st_refs[j][rows, :] - mx[j])
            cl = jnp.sum(p, axis=0, keepdims=True)
            ca = jnp.dot(vt, p.astype(BF16), preferred_element_type=F32)
            ls[j] = cl if c == 0 else ls[j] + cl
            accs[j] = ca if c == 0 else accs[j] + ca
    (a1, a2), (l1, l2) = accs, ls

    lam = (jnp.exp(jnp.sum(lq1_ref[...] * lk1_ref[...], axis=-1, keepdims=True))
           - jnp.exp(jnp.sum(lq2_ref[...] * lk2_ref[...], axis=-1, keepdims=True)) + lambda_init)
    ot = a1 / l1 - lam * (a2 / l2)
    yt = ot * lax.rsqrt(jnp.mean(ot * ot, axis=0, keepdims=True) + EPS)
    o_ref[...] = (yt.T * (g_ref[...] * (1.0 - lambda_init))).astype(o_ref.dtype)


def _diff_attention(proj, slopes, lam_vecs, subln_g, lambda_init, batch, seq, q_col, k_col, v_col,
                    tq=512, tk=512):
    nq = seq // tq
    w = 2 * HEAD_DIM
    vec_spec = pl.BlockSpec((1, HEAD_DIM), lambda b, h, i, s: (0, 0))
    kern = functools.partial(_diff_kernel, tq=tq, tk=tk, nk=seq // tk, lambda_init=lambda_init)
    return pl.pallas_call(
        kern,
        grid_spec=pltpu.PrefetchScalarGridSpec(
            num_scalar_prefetch=1,
            grid=(batch, N_HEADS_B, nq),
            in_specs=[vec_spec, vec_spec, vec_spec, vec_spec,
                      pl.BlockSpec((1, w), lambda b, h, i, s: (0, 0)),
                      pl.BlockSpec((tq, w), lambda b, h, i, s: (b * nq + i, q_col + h)),
                      pl.BlockSpec((seq, w), lambda b, h, i, s: (b, k_col + h)),
                      pl.BlockSpec((seq, w), lambda b, h, i, s: (b, v_col + h))],
            out_specs=pl.BlockSpec((tq, w), lambda b, h, i, s: (b * nq + i, h)),
            scratch_shapes=[pltpu.VMEM((w, seq), BF16),
                            pltpu.VMEM((2 * seq - tq, tq), F32),
                            pltpu.VMEM((seq, tq), F32),
                            pltpu.VMEM((seq, tq), F32)]),
        out_shape=jax.ShapeDtypeStruct((batch * seq, N_HEADS_B * w), BF16),
        compiler_params=_params(("arbitrary", "arbitrary", "arbitrary")),
        name="diff_attn",
    )(slopes, *lam_vecs, subln_g, proj, proj, proj)


def _merge_kernel(ya_ref, yb_ref, ga_ref, gb_ref, x_ref, g1_ref, sc2_ref, sh2_ref, n2g_ref,
                  woa_ref, wob_ref, wo_ref, wrh_ref, wrl_ref, br_ref,
                  h_ref, u2p_ref, lg_ref):
    a = jnp.dot(ya_ref[...], woa_ref[...], preferred_element_type=F32)
    b = jnp.dot(yb_ref[...], wob_ref[...], preferred_element_type=F32)
    merged = (jax.nn.sigmoid(ga_ref[...].astype(F32)) * a
              + jax.nn.sigmoid(gb_ref[...].astype(F32)) * b)
    h = x_ref[...] + g1_ref[...] * jnp.dot(merged.astype(BF16), wo_ref[...],
                                           preferred_element_type=F32)
    h_ref[...] = h
    y = h * lax.rsqrt(jnp.mean(h * h, axis=-1, keepdims=True) + EPS) * n2g_ref[...]
    u2 = y * (1.0 + sc2_ref[...]) + sh2_ref[...]
    hi, lo = _split_bf16(u2)
    lg = _nt_dot(wrh_ref[...], hi) + _nt_dot(wrl_ref[...], hi) + _nt_dot(wrh_ref[...], lo)
    lg_ref[...] = lg + br_ref[...]
    bits = pltpu.bitcast(hi.astype(F32), U32)
    half = bits.shape[1] // 2
    u2p_ref[...] = (bits[:, :half] >> 16) | (bits[:, half:] & jnp.uint32(0xFFFF0000))


def _merge(ya, yb, proj, x2, mod3, n2g, woa, wob, wo, wr_hi, wr_lo, br, seq, ga_col, gb_col, tm=512):
    t, d = x2.shape
    per_b = seq // tm
    wa = ya.shape[1]
    wb = yb.shape[1]
    ne = wr_hi.shape[0]

    def mod_spec(idx):
        return pl.BlockSpec((None, 1, d), lambda i: ((i // per_b) * N_MOD + idx, 0, 0))

    def const_spec(shape):
        return pl.BlockSpec(shape, lambda i: (0,) * len(shape), pipeline_mode=pl.Buffered(1))

    return pl.pallas_call(
        _merge_kernel,
        grid=(t // tm,),
        in_specs=[pl.BlockSpec((tm, wa), lambda i: (i, 0)),
                  pl.BlockSpec((tm, wb), lambda i: (i, 0)),
                  pl.BlockSpec((tm, d), lambda i: (i, ga_col)),
                  pl.BlockSpec((tm, d), lambda i: (i, gb_col)),
                  pl.BlockSpec((tm, d), lambda i: (i, 0)),
                  mod_spec(2), mod_spec(4), mod_spec(3),
                  const_spec((1, d)),
                  const_spec((wa, d)), const_spec((wb, d)), const_spec((d, d)),
                  const_spec((ne, d)), const_spec((ne, d)), const_spec((ne, 1))],
        out_specs=[pl.BlockSpec((tm, d), lambda i: (i, 0)),
                   pl.BlockSpec((tm, d // 2), lambda i: (i, 0)),
                   pl.BlockSpec((ne, tm), lambda i: (0, i))],
        out_shape=[jax.ShapeDtypeStruct((t, d), F32),
                   jax.ShapeDtypeStruct((t, d // 2), U32),
                   jax.ShapeDtypeStruct((ne, t), F32)],
        compiler_params=_params(("arbitrary",)),
        name="merge",
    )(ya, yb, proj, proj, x2, mod3, mod3, mod3, n2g, woa, wob, wo, wr_hi, wr_lo, br)


def _route_kernel(lg_ref, p_ref, dest_ref, cnt_ref, carry_ref, total_ref):
    sweep = pl.program_id(0)

    @pl.when(pl.program_id(1) == 0)
    def _():
        @pl.when(sweep == 0)
        def _():
            total_ref[...] = jnp.zeros_like(total_ref)

        @pl.when(sweep == 1)
        def _():
            total_ref[...] = carry_ref[...]

        carry_ref[...] = jnp.zeros_like(carry_ref)

    lg = lg_ref[...]
    ne, tr = lg.shape
    eio = lax.broadcasted_iota(I32, (ne, tr), 0)
    work = lg
    vals, hots = [], []
    for k in range(TOP_K):
        mx = jnp.max(work, axis=0, keepdims=True)
        am = jnp.min(jnp.where(work == mx, eio, ne), axis=0, keepdims=True)
        hot = eio == am
        vals.append(mx)
        hots.append(hot)
        work = jnp.where(hot, -jnp.inf, work)
    exps = [jnp.exp(v - vals[0]) for v in vals]
    denom = functools.reduce(jnp.add, exps)
    for k in range(TOP_K):
        p_ref[k:k + 1, :] = exps[k] / denom
    chosen = functools.reduce(jnp.logical_or, hots)
    sel = jnp.where(chosen, 1.0, 0.0)
    tri = (lax.broadcasted_iota(I32, (tr, tr), 0) < lax.broadcasted_iota(I32, (tr, tr), 1))
    before = jnp.dot(sel.astype(BF16), jnp.where(tri, 1.0, 0.0).astype(BF16),
                     preferred_element_type=F32)
    carry = carry_ref[...]
    nblk = jnp.floor((total_ref[...][:, 0:1] + (ROW_BLOCK - 1)) * (1.0 / ROW_BLOCK))
    e_row = lax.broadcasted_iota(I32, (ne, ne), 0)
    e_col = lax.broadcasted_iota(I32, (ne, ne), 1)
    nblk_lanes = jnp.sum(jnp.where(e_row == e_col, nblk, 0.0), axis=0, keepdims=True)
    first_row = jnp.sum(jnp.where(e_col < e_row, nblk_lanes, 0.0), axis=1,
                        keepdims=True) * ROW_BLOCK
    place = before + (carry[:, 0:1] + first_row)
    for k in range(TOP_K):
        dest_ref[k:k + 1, :] = jnp.sum(jnp.where(hots[k], place, 0.0), axis=0,
                                       keepdims=True).astype(I32)
    carry = carry + jnp.sum(sel, axis=1, keepdims=True)
    carry_ref[...] = carry
    cnt_ref[...] = carry.astype(I32)


def _route(logits_t, tr=1024):
    ne, t = logits_t.shape
    slot_spec = pl.BlockSpec((None, TOP_K, tr), lambda s, i: (s, 0, i))
    probs, dest, cnt = pl.pallas_call(
        _route_kernel,
        grid=(2, t // tr),
        in_specs=[pl.BlockSpec((ne, tr), lambda s, i: (0, i))],
        out_specs=[slot_spec, slot_spec,
                   pl.BlockSpec((None, ne, LANES), lambda s, i: (s, 0, 0))],
        out_shape=[jax.ShapeDtypeStruct((2, TOP_K, t), F32),
                   jax.ShapeDtypeStruct((2, TOP_K, t), I32),
                   jax.ShapeDtypeStruct((2, ne, LANES), I32)],
        scratch_shapes=[pltpu.VMEM((ne, LANES), F32), pltpu.VMEM((ne, LANES), F32)],
        compiler_params=_params(("arbitrary", "arbitrary")),
        name="route",
    )(logits_t)
    return probs[1], dest[1], cnt[1]


def _dispatch_kernel(dest_ref, fill_ref, nfill_ref, src_ref, dst_ref, sem, fill_sem, *, tm, t_total):
    i = pl.program_id(0)

    @pl.when(i == 0)
    def _():
        def fill(b):
            rows = pl.ds(pl.multiple_of(fill_ref[b] * ROW_BLOCK, ROW_BLOCK), ROW_BLOCK)
            return pltpu.make_async_copy(src_ref.at[pl.ds(0, ROW_BLOCK)], dst_ref.at[rows], fill_sem)

        def fill_start(b, c):
            fill(b).start()
            return c

        def fill_wait(b, c):
            fill(b).wait()
            return c

        lax.fori_loop(0, nfill_ref[0], fill_start, 0)
        lax.fori_loop(0, nfill_ref[0], fill_wait, 0)

    for k in range(TOP_K):
        def group(g, c):
            j0 = pl.multiple_of(g * SUBLANES, SUBLANES)
            tile = src_ref.at[pl.ds(j0, SUBLANES)]
            for u in range(SUBLANES):
                row = dest_ref[k * t_total + i * tm + j0 + u]
                pltpu.make_async_copy(tile.at[pl.ds(u, 1)], dst_ref.at[pl.ds(row, 1)], sem).start()
            return c
        lax.fori_loop(0, tm // SUBLANES, group, 0)
    for k in range(TOP_K):
        pltpu.make_async_copy(src_ref, dst_ref.at[pl.ds(0, tm)], sem).wait()


def _dispatch(dest_flat, fill_blocks, nfill, u2p, total_rows, tm=1024):
    t, w = u2p.shape
    kern = functools.partial(_dispatch_kernel, tm=tm, t_total=t)
    return pl.pallas_call(
        kern,
        grid_spec=pltpu.PrefetchScalarGridSpec(
            num_scalar_prefetch=3,
            grid=(t // tm,),
            in_specs=[pl.BlockSpec((tm, w), lambda i, d, fb, nf: (i, 0))],
            out_specs=pl.BlockSpec(memory_space=pl.ANY),
            scratch_shapes=[pltpu.SemaphoreType.DMA(()), pltpu.SemaphoreType.DMA(())]),
        out_shape=jax.ShapeDtypeStruct((total_rows, w), U32),
        compiler_params=_params(("arbitrary",)),
        name="dispatch",
    )(dest_flat, fill_blocks, nfill, u2p)


def _moe_kernel(che_ref, chblk_ref, chn_ref, nvalid_ref,
                xs_ref, wg_ref, wu_ref, wd_ref, bg_ref, bu_ref, bd_ref,
                ys_ref,
                xin_ref, xbf_ref, yacc_ref, act_ref, in_sem, out_sem,
                *, nff, nsplit, mm_rows):
    c = pl.program_id(0)
    f = pl.program_id(1)
    nvalid = nvalid_ref[0]
    half = xin_ref.shape[1]

    def in_copy(cc, j):
        row = pl.multiple_of((chblk_ref[cc] + j) * ROW_BLOCK, ROW_BLOCK)
        return pltpu.make_async_copy(xs_ref.at[pl.ds(row, ROW_BLOCK)],
                                     xin_ref.at[pl.ds(j * ROW_BLOCK, ROW_BLOCK)], in_sem)

    def out_copy(cc, j):
        row = pl.multiple_of((chblk_ref[cc] + j) * ROW_BLOCK, ROW_BLOCK)
        return pltpu.make_async_copy(yacc_ref.at[pl.ds(j * ROW_BLOCK, ROW_BLOCK)],
                                     ys_ref.at[pl.ds(row, ROW_BLOCK)], out_sem)

    def for_blocks(cc, fn):
        def one(j, carry):
            fn(cc, j)
            return carry
        lax.fori_loop(0, chn_ref[cc], one, 0)

    def chunk_step():
        @pl.when(f == 0)
        def _():
            @pl.when(c == 0)
            def _():
                xbf_ref[...] = jnp.zeros_like(xbf_ref)
                for_blocks(c, lambda cc, j: in_copy(cc, j).start())

            for_blocks(c, lambda cc, j: in_copy(cc, j).wait())

            def unpack(cc, j):
                rows = pl.ds(pl.multiple_of(j * ROW_BLOCK, ROW_BLOCK), ROW_BLOCK)
                w = xin_ref[rows, :]
                xbf_ref[rows, :half] = pltpu.bitcast(w << 16, F32).astype(BF16)
                xbf_ref[rows, half:] = pltpu.bitcast(w & jnp.uint32(0xFFFF0000), F32).astype(BF16)

            for_blocks(c, unpack)

            @pl.when(c + 1 < nvalid)
            def _():
                for_blocks(c + 1, lambda cc, j: in_copy(cc, j).start())

        mm_blocks = mm_rows // ROW_BLOCK
        rest = jnp.maximum(chn_ref[c] - mm_blocks, 0)
        n_full = rest // mm_blocks
        rem = rest - n_full * mm_blocks
        tail0 = pl.multiple_of((1 + n_full) * mm_rows, ROW_BLOCK)

        def for_tail(fn):
            for r in range(1, mm_blocks):
                pl.when(rem == r)(functools.partial(fn, pl.ds(tail0, r * ROW_BLOCK)))

        def gate_up(rows):
            x = xbf_ref[rows, :]
            g = jnp.minimum(jnp.dot(x, wg_ref[...].astype(BF16), preferred_element_type=F32)
                            + bg_ref[...], SWIGLU_LIMIT)
            u = jnp.clip(jnp.dot(x, wu_ref[...].astype(BF16), preferred_element_type=F32)
                         + bu_ref[...], -SWIGLU_LIMIT, SWIGLU_LIMIT)
            act_ref[rows, :] = ((u + 1.0) * (g * jax.nn.sigmoid(SWIGLU_ALPHA * g))).astype(BF16)

        def gate_up_body(rb, carry):
            gate_up(pl.ds(pl.multiple_of(rb * mm_rows, mm_rows), mm_rows))
            return carry

        lax.fori_loop(0, 1 + n_full, gate_up_body, 0)
        for_tail(gate_up)

        @pl.when((f == 0) & (c > 0))
        def _():
            for_blocks(c - 1, lambda cc, j: out_copy(cc, j).wait())

        first = f == 0
        ncol = yacc_ref.shape[1] // nsplit

        def down(rows):
            a = act_ref[rows, :]
            for s in range(nsplit):
                cols = slice(s * ncol, (s + 1) * ncol)
                part = jnp.dot(a, wd_ref[:, cols].astype(BF16), preferred_element_type=F32)
                base = jnp.where(first, jnp.broadcast_to(bd_ref[:, cols], part.shape),
                                 yacc_ref[rows, cols])
                yacc_ref[rows, cols] = base + part

        def down_body(rb, carry):
            down(pl.ds(pl.multiple_of(rb * mm_rows, mm_rows), mm_rows))
            return carry

        lax.fori_loop(0, 1 + n_full, down_body, 0)
        for_tail(down)

        @pl.when(f == nff - 1)
        def _():
            for_blocks(c, lambda cc, j: out_copy(cc, j).start())

            @pl.when(c == nvalid - 1)
            def _():
                for_blocks(c, lambda cc, j: out_copy(cc, j).wait())

                def fill(b):
                    rows = pl.ds(pl.multiple_of(b * ROW_BLOCK, ROW_BLOCK), ROW_BLOCK)
                    return pltpu.make_async_copy(yacc_ref.at[pl.ds(0, ROW_BLOCK)], ys_ref.at[rows],
                                                 out_sem)

                def fill_start(b, carry):
                    fill(b).start()
                    return carry

                def fill_wait(b, carry):
                    fill(b).wait()
                    return carry

                nused = chblk_ref[c] + chn_ref[c]
                ntotal = ys_ref.shape[0] // ROW_BLOCK
                lax.fori_loop(nused, ntotal, fill_start, 0)
                lax.fori_loop(nused, ntotal, fill_wait, 0)

    chunk_step()


def _moe(che, chblk, chn, nvalid, xs, wg, wu, wd, bg, bu, bd):
    ne, d, dff = wg.shape
    rows = CHUNK_BLOCKS * ROW_BLOCK
    nff = dff // FF_TILE

    kern = functools.partial(_moe_kernel, nff=nff, nsplit=4, mm_rows=MM_ROW_BLOCKS * ROW_BLOCK)
    return pl.pallas_call(
        kern,
        grid_spec=pltpu.PrefetchScalarGridSpec(
            num_scalar_prefetch=4,
            grid=(nvalid[0], nff),
            in_specs=[pl.BlockSpec(memory_space=pl.ANY),
                      pl.BlockSpec((None, d, FF_TILE), lambda c, f, e, b, n, nv: (e[c], 0, f)),
                      pl.BlockSpec((None, d, FF_TILE), lambda c, f, e, b, n, nv: (e[c], 0, f)),
                      pl.BlockSpec((None, FF_TILE, d), lambda c, f, e, b, n, nv: (e[c], f, 0)),
                      pl.BlockSpec((None, 1, FF_TILE), lambda c, f, e, b, n, nv: (e[c], 0, f)),
                      pl.BlockSpec((None, 1, FF_TILE), lambda c, f, e, b, n, nv: (e[c], 0, f)),
                      pl.BlockSpec((None, 1, d), lambda c, f, e, b, n, nv: (e[c], 0, 0))],
            out_specs=pl.BlockSpec(memory_space=pl.ANY),
            scratch_shapes=[pltpu.VMEM((rows, d // 2), U32),
                            pltpu.VMEM((rows, d), BF16),
                            pltpu.VMEM((rows, d), F32),
                            pltpu.VMEM((rows, FF_TILE), BF16),
                            pltpu.SemaphoreType.DMA(()),
                            pltpu.SemaphoreType.DMA(())]),
        out_shape=jax.ShapeDtypeStruct((xs.shape[0], d), F32),
        compiler_params=_params(("arbitrary", "arbitrary")),
        name="moe",
    )(che, chblk, chn, nvalid, xs, wg, wu, wd, bg, bu, bd)


def _combine_kernel(dest_ref, ys_ref, h_ref, p_ref, g2_ref, fg_ref, o_ref, buf_ref, sem,
                    *, tc, nsteps, final_norm):
    i = pl.program_id(0)
    t_total = nsteps * tc

    def issue(step, slot):
        for k in range(TOP_K):
            def group(g, c):
                j0 = pl.multiple_of(g * SUBLANES, SUBLANES)
                tile = buf_ref.at[slot, k, pl.ds(j0, SUBLANES)]
                for u in range(SUBLANES):
                    src = dest_ref[k * t_total + step * tc + j0 + u]
                    pltpu.make_async_copy(ys_ref.at[pl.ds(src, 1)], tile.at[pl.ds(u, 1)],
                                          sem.at[slot]).start()
                return c
            lax.fori_loop(0, tc // SUBLANES, group, 0)

    def drain(slot):
        for k in range(TOP_K):
            pltpu.make_async_copy(ys_ref.at[pl.ds(0, tc)], buf_ref.at[slot, k], sem.at[slot]).wait()

    slot = i % 2

    @pl.when(i == 0)
    def _():
        issue(0, 0)

    @pl.when(i + 1 < nsteps)
    def _():
        issue(i + 1, 1 - slot)

    drain(slot)
    p = p_ref[...]
    moe = p[:, 0:1] * buf_ref[slot, 0]
    for k in range(1, TOP_K):
        moe += p[:, k:k + 1] * buf_ref[slot, k]
    h = h_ref[...] + g2_ref[...] * moe
    if final_norm:
        h = h * lax.rsqrt(jnp.mean(h * h, axis=-1, keepdims=True) + EPS) * fg_ref[...]
    o_ref[...] = h


def _combine(dest_flat, ys, h, probs_t, mod3, final_g, seq, final_norm, tc=256):
    t, d = h.shape
    per_b = seq // tc
    nsteps = t // tc
    kern = functools.partial(_combine_kernel, tc=tc, nsteps=nsteps, final_norm=final_norm)
    return pl.pallas_call(
        kern,
        grid_spec=pltpu.PrefetchScalarGridSpec(
            num_scalar_prefetch=1,
            grid=(nsteps,),
            in_specs=[pl.BlockSpec(memory_space=pl.ANY),
                      pl.BlockSpec((tc, d), lambda i, s: (i, 0)),
                      pl.BlockSpec((tc, TOP_K), lambda i, s: (i, 0)),
                      pl.BlockSpec((None, 1, d), lambda i, s: ((i // per_b) * N_MOD + 5, 0, 0)),
                      pl.BlockSpec((1, d), lambda i, s: (0, 0))],
            out_specs=pl.BlockSpec((tc, d), lambda i, s: (i, 0)),
            scratch_shapes=[pltpu.VMEM((2, TOP_K, tc, d), F32),
                            pltpu.SemaphoreType.DMA((2,))]),
        out_shape=jax.ShapeDtypeStruct((t, d), F32),
        compiler_params=_params(("arbitrary",)),
        name="combine",
    )(dest_flat, ys, h, probs_t, mod3, final_g)


def _routing_plan(counts, t):
    nblk_total = -(-t * TOP_K // ROW_BLOCK) + N_EXPERTS
    nblk = (counts + ROW_BLOCK - 1) // ROW_BLOCK
    blk_end = jnp.cumsum(nblk)
    blk_start = blk_end - nblk
    ntail_max = nblk_total - t * TOP_K // ROW_BLOCK
    fill_blocks = jnp.concatenate([jnp.maximum(blk_end - 1, 0),
                                   jnp.minimum(blk_end[-1] + jnp.arange(ntail_max), nblk_total - 1)])
    nfill = N_EXPERTS + nblk_total - blk_end[-1]
    max_chunks = -(-nblk_total // CHUNK_BLOCKS) + N_EXPERTS
    nch = (nblk + CHUNK_BLOCKS - 1) // CHUNK_BLOCKS
    ch_end = jnp.cumsum(nch)
    ch_start = ch_end - nch
    nvalid = ch_end[-1]
    cid = jnp.minimum(jnp.arange(max_chunks, dtype=I32), nvalid - 1)
    che =jnp.clip(jnp.searchsorted(ch_end, cid, side='right'), 0, N_EXPERTS - 1).astype(I32)
    local = cid - ch_start[che]
    nch_e = jnp.maximum(nch[che], 1)
    n_mm = (nblk[che] + MM_ROW_BLOCKS - 1) // MM_ROW_BLOCKS
    size = n_mm // nch_e
    extra = n_mm % nch_e
    first_mm = local * size + jnp.minimum(local, extra)
    chblk = (blk_start[che] + first_mm * MM_ROW_BLOCKS).astype(I32)
    chn = jnp.minimum((size + (local < extra)) * MM_ROW_BLOCKS,
                      nblk[che] - first_mm * MM_ROW_BLOCKS).astype(I32)
    return (fill_blocks.astype(I32), nfill.astype(I32).reshape(1), che, chblk, chn,
            nvalid.astype(I32).reshape(1), nblk_total * ROW_BLOCK)


def kernel(x, c, w_ada, b_ada, norm1_g, w_in, lam_q1, lam_k1, lam_q2, lam_k2, subln_g, w_out_a, w_out_b, w_o, norm2_g, w_router, b_router, w_gate, b_gate, w_up, b_up, w_down, b_down, final_g):
    batch, seq, d = x.shape
    t = batch * seq
    depth = w_ada.shape[0]
    wa = N_HEADS_A * HEAD_DIM
    wb = N_HEADS_B * 2 * HEAD_DIM
    slopes = _alibi_slopes(N_HEADS_A + N_HEADS_B)
    slopes_a = jnp.asarray(slopes[:N_HEADS_A])
    slopes_b = jnp.asarray(slopes[N_HEADS_A:])
    c8 = jnp.pad(c, ((0, 8 - batch), (0, 0)))

    h = x.reshape(t, d)
    for l in range(depth):
        lambda_init = 0.8 - 0.6 * math.exp(-0.3 * l)
        mod = _ada(c8, w_ada[l], b_ada[l].reshape(1, -1))[:batch]
        mod3 = mod.reshape(batch * N_MOD, 1, d)

        u = _norm_mod(h, norm1_g[l].reshape(1, d), mod3, 1, 0, seq)
        proj = _inproj(u, w_in[l], q_tiles=(0, 3 * wa // 1024))
        y_a = _dil_attention(proj, slopes_a, batch, seq, 0, wa // HEAD_DIM, 2 * wa // HEAD_DIM)
        off_b = 3 * wa // (2 * HEAD_DIM)
        nb = wb // (2 * HEAD_DIM)
        y_b = _diff_attention(proj, slopes_b,
                              [v[l].reshape(1, HEAD_DIM) for v in (lam_q1, lam_k1, lam_q2, lam_k2)],
                              subln_g[l].reshape(1, -1), lambda_init, batch, seq,
                              off_b, off_b + nb, off_b + 2 * nb)
        wr_hi, wr_lo = _split_bf16(w_router[l].T)
        gate_col = (3 * wa + 3 * wb) // d
        h, u2p, logits_t = _merge(y_a, y_b, proj, h, mod3, norm2_g[l].reshape(1, d),
                                  w_out_a[l].astype(BF16), w_out_b[l].astype(BF16),
                                  w_o[l].astype(BF16), wr_hi, wr_lo, b_router[l].reshape(-1, 1),
                                  seq, gate_col, gate_col + 1)
        probs, dest, cnt = _route(logits_t)
        (fill_blocks, nfill, che, chblk, chn, nvalid, total_rows) = _routing_plan(cnt[:, 0], t)
        xs = _dispatch(dest.reshape(-1), fill_blocks, nfill, u2p, total_rows)
        ys = _moe(che, chblk, chn, nvalid, xs, w_gate[l], w_up[l], w_down[l],
                  b_gate[l].reshape(N_EXPERTS, 1, -1), b_up[l].reshape(N_EXPERTS, 1, -1),
                  b_down[l].reshape(N_EXPERTS, 1, -1))
        h = _combine(dest.reshape(-1), ys, h, probs.T, mod3, final_g.reshape(1, d), seq,
                     final_norm=(l == depth - 1))
    return h.reshape(batch, seq, d)
```

```python
import functools
import math

import numpy as np
import jax
import jax.numpy as jnp
from jax import lax
from jax.experimental import pallas as pl
from jax.experimental.pallas import tpu as pltpu

F32 = jnp.float32
BF16 = jnp.bfloat16
U32 = jnp.uint32
I32 = jnp.int32

HEAD_DIM = 128
N_HEADS_A = 8
N_HEADS_B = 4
DILATED_PATTERNS = ((128, 1), (512, 4), (2048, 16))
N_EXPERTS = 32
TOP_K = 4
SWIGLU_LIMIT = 7.0
SWIGLU_ALPHA = 1.702
N_MOD = 6
EPS = 1e-5
NEG = -1e30
LOG2E = math.log2(math.e)

LANES = 128
SUBLANES = 8
V7X_VMEM_LIMIT = 56 * 1024 * 1024

ROW_BLOCK = 128
CHUNK_BLOCKS = 12
MM_ROW_BLOCKS = 4
FF_TILE = 512


def _alibi_slopes(n):
    return np.array([2.0 ** (-8.0 * (i + 1) / n) for i in range(n)], dtype=np.float32)


def _nt_dot(a, b):
    return lax.dot_general(a, b, (((1,), (1,)), ((), ())), preferred_element_type=F32)


def _split_bf16(x):
    hi = x.astype(BF16)
    lo = (x - hi.astype(F32)).astype(BF16)
    return hi, lo


def _params(sem, vmem=V7X_VMEM_LIMIT):
    return pltpu.CompilerParams(dimension_semantics=sem, vmem_limit_bytes=vmem)


def _ada_kernel(c_ref, w_ref, b_ref, o_ref):
    c = c_ref[...]
    a = c * jax.nn.sigmoid(c)
    a_hi, a_lo = _split_bf16(a)
    w_hi, w_lo = _split_bf16(w_ref[...])
    m = a.shape[0]
    both = jnp.dot(jnp.concatenate([a_hi, a_lo], axis=0), w_hi, preferred_element_type=F32)
    acc = both[:m] + both[m:] + jnp.dot(a_hi, w_lo, preferred_element_type=F32)
    o_ref[...] = acc + b_ref[...]


def _ada(c8, w, b, tn=1024):
    m, d = c8.shape
    n = w.shape[1]
    return pl.pallas_call(
        _ada_kernel,
        grid=(n // tn,),
        in_specs=[pl.BlockSpec((m, d), lambda j: (0, 0)),
                  pl.BlockSpec((d, tn), lambda j: (0, j)),
                  pl.BlockSpec((1, tn), lambda j: (0, j))],
        out_specs=pl.BlockSpec((m, tn), lambda j: (0, j)),
        out_shape=jax.ShapeDtypeStruct((m, n), F32),
        compiler_params=_params(("arbitrary",)),
        name="ada",
    )(c8, w, b)


def _norm_mod_kernel(x_ref, g_ref, sc_ref, sh_ref, o_ref):
    x = x_ref[...]
    y = x * lax.rsqrt(jnp.mean(x * x, axis=-1, keepdims=True) + EPS) * g_ref[...]
    o_ref[...] = (y * (1.0 + sc_ref[...]) + sh_ref[...]).astype(o_ref.dtype)


def _norm_mod(x2, g, mod3, i_scale, i_shift, seq, tm=1024):
    t, d = x2.shape
    per_b = seq // tm
    return pl.pallas_call(
        _norm_mod_kernel,
        grid=(t // tm,),
        in_specs=[pl.BlockSpec((tm, d), lambda i: (i, 0)),
                  pl.BlockSpec((1, d), lambda i: (0, 0)),
                  pl.BlockSpec((None, 1, d), lambda i: ((i // per_b) * N_MOD + i_scale, 0, 0)),
                  pl.BlockSpec((None, 1, d), lambda i: ((i // per_b) * N_MOD + i_shift, 0, 0))],
        out_specs=pl.BlockSpec((tm, d), lambda i: (i, 0)),
        out_shape=jax.ShapeDtypeStruct((t, d), BF16),
        compiler_params=_params(("arbitrary",)),
        name="norm1",
    )(x2, g, mod3, mod3)


def _inproj_kernel(u_ref, w_ref, o_ref, *, q_tiles, scale):
    n = pl.program_id(0)
    acc = jnp.dot(u_ref[...], w_ref[...].astype(BF16), preferred_element_type=F32)
    is_q = functools.reduce(jnp.logical_or, [n == q for q in q_tiles])
    o_ref[...] = (acc * jnp.where(is_q, scale, 1.0)).astype(BF16)


def _inproj(u, w, q_tiles, tm=2048, tn=1024):
    t, d = u.shape
    n = w.shape[1]
    return pl.pallas_call(
        functools.partial(_inproj_kernel, q_tiles=q_tiles, scale=HEAD_DIM ** -0.5 * LOG2E),
        grid=(n // tn, t // tm),
        in_specs=[pl.BlockSpec((tm, d), lambda j, i: (i, 0)),
                  pl.BlockSpec((d, tn), lambda j, i: (0, j))],
        out_specs=pl.BlockSpec((tm, tn), lambda j, i: (i, j)),
        out_shape=jax.ShapeDtypeStruct((t, n), BF16),
        compiler_params=_params(("arbitrary", "arbitrary")),
        name="inproj",
    )(u, w)


def _dil_tables(tq, span):
    nside = span // tq
    o = lax.broadcasted_iota(I32, (2 * nside + 1, tq, tq), 0) - nside
    i = lax.broadcasted_iota(I32, (2 * nside + 1, tq, tq), 1)
    j = lax.broadcasted_iota(I32, (2 * nside + 1, tq, tq), 2)
    ad = jnp.abs(o * tq + j - i)
    mult = jnp.zeros_like(ad)
    for window, dil in DILATED_PATTERNS:
        mult += ((ad % dil == 0) & (ad // dil <= window // (2 * dil))).astype(I32)
    lm = jnp.where(mult > 0, jnp.log2(jnp.maximum(mult, 1).astype(F32)), NEG)
    return ad.astype(F32) * LOG2E, lm


def _dil_kernel(slopes_ref, q_ref, k_ref, v_ref, ad_ref, lm_ref, o_ref, bias_ref, vt_ref, st_ref,
                *, tq, nside, nq):
    h = pl.program_id(1)
    i = pl.program_id(2)

    @pl.when(i == 0)
    def _():
        bias_ref[:2 * nside + 1] = lm_ref[...] - slopes_ref[h] * ad_ref[...]
        bias_ref[2 * nside + 1] = jnp.full((tq, tq), NEG, F32)
        for kb in range(nq):
            vt_ref[kb] = v_ref[kb * tq:(kb + 1) * tq, :].astype(F32).T.astype(BF16)

    nwin = 2 * nside + 1
    b0 = jnp.clip(i - nside, 0, nq - nwin)
    q = q_ref[...]

    def score_block(j, m):
        o = b0 + j - i
        plane = jnp.where(jnp.abs(o) <= nside, nside - o, nwin)
        kb = k_ref[pl.ds(pl.multiple_of((b0 + j) * tq, tq), tq), :]
        s = _nt_dot(kb, q) + bias_ref[plane]
        st_ref[j] = s
        return jnp.maximum(m, jnp.max(s, axis=0, keepdims=True))

    m = lax.fori_loop(0, nwin, score_block, jnp.full((1, tq), NEG, F32), unroll=True)
    l = acc = None
    for j in range(nwin):
        p = jnp.exp2(st_ref[j] - m)
        cl = jnp.sum(p, axis=0, keepdims=True)
        ca = jnp.dot(vt_ref[b0 + j], p.astype(BF16), preferred_element_type=F32)
        l = cl if j == 0 else l + cl
        acc = ca if j == 0 else acc + ca
    o_ref[...] = (acc / l).T.astype(o_ref.dtype)


def _dil_attention(proj, slopes, batch, seq, q_col, k_col, v_col, tq=512):
    span = max(w // 2 for w, _ in DILATED_PATTERNS)
    nside = span // tq
    nq = seq // tq
    ad, lm = _dil_tables(tq, span)
    tab_spec = pl.BlockSpec((2 * nside + 1, tq, tq), lambda b, h, i, s: (0, 0, 0))
    kern = functools.partial(_dil_kernel, tq=tq, nside=nside, nq=nq)
    return pl.pallas_call(
        kern,
        grid_spec=pltpu.PrefetchScalarGridSpec(
            num_scalar_prefetch=1,
            grid=(batch, N_HEADS_A, nq),
            in_specs=[pl.BlockSpec((tq, HEAD_DIM), lambda b, h, i, s: (b * nq + i, q_col + h)),
                      pl.BlockSpec((seq, HEAD_DIM), lambda b, h, i, s: (b, k_col + h)),
                      pl.BlockSpec((seq, HEAD_DIM), lambda b, h, i, s: (b, v_col + h)),
                      tab_spec, tab_spec],
            out_specs=pl.BlockSpec((tq, HEAD_DIM), lambda b, h, i, s: (b * nq + i, h)),
            scratch_shapes=[pltpu.VMEM((2 * nside + 2, tq, tq), F32),
                            pltpu.VMEM((nq, HEAD_DIM, tq), BF16),
                            pltpu.VMEM((2 * nside + 1, tq, tq), F32)]),
        out_shape=jax.ShapeDtypeStruct((batch * seq, N_HEADS_A * HEAD_DIM), BF16),
        compiler_params=_params(("arbitrary", "arbitrary", "arbitrary")),
        name="dil_attn",
    )(slopes, proj, proj, proj, ad, lm)


def _diff_kernel(slopes_ref, lq1_ref, lk1_ref, lq2_ref, lk2_ref, g_ref, q_ref, k_ref, v_ref, o_ref,
                 vt_ref, bias_ref, st1_ref, st2_ref, *, tq, tk, nk, lambda_init):
    h = pl.program_id(1)
    i = pl.program_id(2)

    seq = k_ref.shape[0]
    nq = seq // tq

    @pl.when(i == 0)
    def _():
        for kb in range(nk):
            cols = slice(kb * tk, (kb + 1) * tk)
            vt_ref[:, cols] = v_ref[cols, :].astype(F32).T.astype(BF16)

        neg_c = -slopes_ref[h] * LOG2E
        base = (lax.broadcasted_iota(I32, (tq, tq), 0) - lax.broadcasted_iota(I32, (tq, tq), 1)
                - (seq - tq))

        def fill(r, carry):
            rows = pl.ds(pl.multiple_of(r * tq, tq), tq)
            bias_ref[rows, :] = jnp.abs(base + r * tq).astype(F32) * neg_c
            return carry

        lax.fori_loop(0, 2 * nq - 1, fill, 0)

    q = q_ref[...]
    qs = (q[:, :HEAD_DIM], q[:, HEAD_DIM:])
    st_refs = (st1_ref, st2_ref)
    w0 = (nq - 1 - i) * tq

    def score_chunk(c, mx):
        rows = pl.ds(pl.multiple_of(c * tk, tk), tk)
        kc = k_ref[rows, :]
        b = bias_ref[pl.ds(pl.multiple_of(w0 + c * tk, tq), tk), :]
        out = []
        for j in range(2):
            s = _nt_dot(kc[:, j * HEAD_DIM:(j + 1) * HEAD_DIM], qs[j]) + b
            st_refs[j][rows, :] = s
            out.append(jnp.maximum(mx[j], jnp.max(s, axis=0, keepdims=True)))
        return tuple(out)

    neg = jnp.full((1, tq), NEG, F32)
    mx = lax.fori_loop(0, nk, score_chunk, (neg, neg), unroll=4)
    ls = [None, None]
    accs = [None, None]
    for c in range(nk):
        rows = slice(c * tk, (c + 1) * tk)
        vt = vt_ref[:, rows]
        for j in range(2):
            p = jnp.exp2(st_refs[j][rows, :] - mx[j])
            cl = jnp.sum(p, axis=0, keepdims=True)
            ca = jnp.dot(vt, p.astype(BF16), preferred_element_type=F32)
            ls[j] = cl if c == 0 else ls[j] + cl
            accs[j] = ca if c == 0 else accs[j] + ca
    (a1, a2), (l1, l2) = accs, ls

    lam = (jnp.exp(jnp.sum(lq1_ref[...] * lk1_ref[...], axis=-1, keepdims=True))
           - jnp.exp(jnp.sum(lq2_ref[...] * lk2_ref[...], axis=-1, keepdims=True)) + lambda_init)
    ot = a1 / l1 - lam * (a2 / l2)
    yt = ot * lax.rsqrt(jnp.mean(ot * ot, axis=0, keepdims=True) + EPS)
    o_ref[...] = (yt.T * (g_ref[...] * (1.0 - lambda_init))).astype(o_ref.dtype)


def _diff_attention(proj, slopes, lam_vecs, subln_g, lambda_init, batch, seq, q_col, k_col, v_col,
                    tq=512, tk=512):
    nq = seq // tq
    w = 2 * HEAD_DIM
    vec_spec = pl.BlockSpec((1, HEAD_DIM), lambda b, h, i, s: (0, 0))
    kern = functools.partial(_diff_kernel, tq=tq, tk=tk, nk=seq // tk, lambda_init=lambda_init)
    return pl.pallas_call(
        kern,
        grid_spec=pltpu.PrefetchScalarGridSpec(
            num_scalar_prefetch=1,
            grid=(batch, N_HEADS_B, nq),
            in_specs=[vec_spec, vec_spec, vec_spec, vec_spec,
                      pl.BlockSpec((1, w), lambda b, h, i, s: (0, 0)),
                      pl.BlockSpec((tq, w), lambda b, h, i, s: (b * nq + i, q_col + h)),
                      pl.BlockSpec((seq, w), lambda b, h, i, s: (b, k_col + h)),
                      pl.BlockSpec((seq, w), lambda b, h, i, s: (b, v_col + h))],
            out_specs=pl.BlockSpec((tq, w), lambda b, h, i, s: (b * nq + i, h)),
            scratch_shapes=[pltpu.VMEM((w, seq), BF16),
                            pltpu.VMEM((2 * seq - tq, tq), F32),
                            pltpu.VMEM((seq, tq), F32),
                            pltpu.VMEM((seq, tq), F32)]),
        out_shape=jax.ShapeDtypeStruct((batch * seq, N_HEADS_B * w), BF16),
        compiler_params=_params(("arbitrary", "arbitrary", "arbitrary")),
        name="diff_attn",
    )(slopes, *lam_vecs, subln_g, proj, proj, proj)


def _merge_kernel(ya_ref, yb_ref, ga_ref, gb_ref, x_ref, g1_ref, sc2_ref, sh2_ref, n2g_ref,
                  woa_ref, wob_ref, wo_ref, wrh_ref, wrl_ref, br_ref,
                  h_ref, u2p_ref, lg_ref):
    a = jnp.dot(ya_ref[...], woa_ref[...], preferred_element_type=F32)
    b = jnp.dot(yb_ref[...], wob_ref[...], preferred_element_type=F32)
    merged = (jax.nn.sigmoid(ga_ref[...].astype(F32)) * a
              + jax.nn.sigmoid(gb_ref[...].astype(F32)) * b)
    h = x_ref[...] + g1_ref[...] * jnp.dot(merged.astype(BF16), wo_ref[...],
                                           preferred_element_type=F32)
    h_ref[...] = h
    y = h * lax.rsqrt(jnp.mean(h * h, axis=-1, keepdims=True) + EPS) * n2g_ref[...]
    u2 = y * (1.0 + sc2_ref[...]) + sh2_ref[...]
    hi, lo = _split_bf16(u2)
    lg = _nt_dot(wrh_ref[...], hi) + _nt_dot(wrl_ref[...], hi) + _nt_dot(wrh_ref[...], lo)
    lg_ref[...] = lg + br_ref[...]
    bits = pltpu.bitcast(hi.astype(F32), U32)
    half = bits.shape[1] // 2
    word = (bits[:, :half] >> 16) | (bits[:, half:] & jnp.uint32(0xFFFF0000))
    for j in range(half // LANES):
        u2p_ref[:, j, :] = word[:, j * LANES:(j + 1) * LANES]


def _merge(ya, yb, proj, x2, mod3, n2g, woa, wob, wo, wr_hi, wr_lo, br, seq, ga_col, gb_col, tm=512):
    t, d = x2.shape
    per_b = seq // tm
    wa = ya.shape[1]
    wb = yb.shape[1]
    ne = wr_hi.shape[0]

    def mod_spec(idx):
        return pl.BlockSpec((None, 1, d), lambda i: ((i // per_b) * N_MOD + idx, 0, 0))

    def const_spec(shape):
        return pl.BlockSpec(shape, lambda i: (0,) * len(shape), pipeline_mode=pl.Buffered(1))

    return pl.pallas_call(
        _merge_kernel,
        grid=(t // tm,),
        in_specs=[pl.BlockSpec((tm, wa), lambda i: (i, 0)),
                  pl.BlockSpec((tm, wb), lambda i: (i, 0)),
                  pl.BlockSpec((tm, d), lambda i: (i, ga_col)),
                  pl.BlockSpec((tm, d), lambda i: (i, gb_col)),
                  pl.BlockSpec((tm, d), lambda i: (i, 0)),
                  mod_spec(2), mod_spec(4), mod_spec(3),
                  const_spec((1, d)),
                  const_spec((wa, d)), const_spec((wb, d)), const_spec((d, d)),
                  const_spec((ne, d)), const_spec((ne, d)), const_spec((ne, 1))],
        out_specs=[pl.BlockSpec((tm, d), lambda i: (i, 0)),
                   pl.BlockSpec((tm, d // 2 // LANES, LANES), lambda i: (i, 0, 0)),
                   pl.BlockSpec((ne, tm), lambda i: (0, i))],
        out_shape=[jax.ShapeDtypeStruct((t, d), F32),
                   jax.ShapeDtypeStruct((t, d // 2 // LANES, LANES), U32),
                   jax.ShapeDtypeStruct((ne, t), F32)],
        compiler_params=_params(("arbitrary",)),
        name="merge",
    )(ya, yb, proj, proj, x2, mod3, mod3, mod3, n2g, woa, wob, wo, wr_hi, wr_lo, br)


def _route_kernel(lg_ref, p_ref, dest_ref, cnt_ref, carry_ref, total_ref):
    sweep = pl.program_id(0)

    @pl.when(pl.program_id(1) == 0)
    def _():
        @pl.when(sweep == 0)
        def _():
            total_ref[...] = jnp.zeros_like(total_ref)

        @pl.when(sweep == 1)
        def _():
            total_ref[...] = carry_ref[...]

        carry_ref[...] = jnp.zeros_like(carry_ref)

    lg = lg_ref[...]
    ne, tr = lg.shape
    eio = lax.broadcasted_iota(I32, (ne, tr), 0)
    work = lg
    vals, hots = [], []
    for k in range(TOP_K):
        mx = jnp.max(work, axis=0, keepdims=True)
        am = jnp.min(jnp.where(work == mx, eio, ne), axis=0, keepdims=True)
        hot = eio == am
        vals.append(mx)
        hots.append(hot)
        work = jnp.where(hot, -jnp.inf, work)
    exps = [jnp.exp(v - vals[0]) for v in vals]
    denom = functools.reduce(jnp.add, exps)
    for k in range(TOP_K):
        p_ref[k:k + 1, :] = exps[k] / denom
    chosen = functools.reduce(jnp.logical_or, hots)
    sel = jnp.where(chosen, 1.0, 0.0)
    tri = (lax.broadcasted_iota(I32, (tr, tr), 0) < lax.broadcasted_iota(I32, (tr, tr), 1))
    before = jnp.dot(sel.astype(BF16), jnp.where(tri, 1.0, 0.0).astype(BF16),
                     preferred_element_type=F32)
    carry = carry_ref[...]
    nblk = jnp.floor((total_ref[...][:, 0:1] + (ROW_BLOCK - 1)) * (1.0 / ROW_BLOCK))
    e_row = lax.broadcasted_iota(I32, (ne, ne), 0)
    e_col = lax.broadcasted_iota(I32, (ne, ne), 1)
    nblk_lanes = jnp.sum(jnp.where(e_row == e_col, nblk, 0.0), axis=0, keepdims=True)
    first_row = jnp.sum(jnp.where(e_col < e_row, nblk_lanes, 0.0), axis=1,
                        keepdims=True) * ROW_BLOCK
    place = before + (carry[:, 0:1] + first_row)
    for k in range(TOP_K):
        dest_ref[k:k + 1, :] = jnp.sum(jnp.where(hots[k], place, 0.0), axis=0,
                                       keepdims=True).astype(I32)
    carry = carry + jnp.sum(sel, axis=1, keepdims=True)
    carry_ref[...] = carry
    cnt_ref[...] = carry.astype(I32)


def _route(logits_t, tr=1024):
    ne, t = logits_t.shape
    slot_spec = pl.BlockSpec((None, TOP_K, tr), lambda s, i: (s, 0, i))
    probs, dest, cnt = pl.pallas_call(
        _route_kernel,
        grid=(2, t // tr),
        in_specs=[pl.BlockSpec((ne, tr), lambda s, i: (0, i))],
        out_specs=[slot_spec, slot_spec,
                   pl.BlockSpec((None, ne, LANES), lambda s, i: (s, 0, 0))],
        out_shape=[jax.ShapeDtypeStruct((2, TOP_K, t), F32),
                   jax.ShapeDtypeStruct((2, TOP_K, t), I32),
                   jax.ShapeDtypeStruct((2, ne, LANES), I32)],
        scratch_shapes=[pltpu.VMEM((ne, LANES), F32), pltpu.VMEM((ne, LANES), F32)],
        compiler_params=_params(("arbitrary", "arbitrary")),
        name="route",
    )(logits_t)
    return probs[1], dest[1], cnt[1]


def _dispatch_kernel(dest_ref, fill_ref, nfill_ref, src_ref, dst_ref, sem, fill_sem, *, tm, t_total):
    i = pl.program_id(0)

    @pl.when(i == 0)
    def _():
        def fill(b):
            rows = pl.ds(pl.multiple_of(fill_ref[b] * ROW_BLOCK, ROW_BLOCK), ROW_BLOCK)
            return pltpu.make_async_copy(src_ref.at[pl.ds(0, ROW_BLOCK)], dst_ref.at[rows], fill_sem)

        def fill_start(b, c):
            fill(b).start()
            return c

        def fill_wait(b, c):
            fill(b).wait()
            return c

        lax.fori_loop(0, nfill_ref[0], fill_start, 0)
        lax.fori_loop(0, nfill_ref[0], fill_wait, 0)

    for k in range(TOP_K):
        def group(g, c):
            j0 = pl.multiple_of(g * SUBLANES, SUBLANES)
            tile = src_ref.at[pl.ds(j0, SUBLANES)]
            for u in range(SUBLANES):
                row = dest_ref[k * t_total + i * tm + j0 + u]
                pltpu.make_async_copy(tile.at[pl.ds(u, 1)], dst_ref.at[pl.ds(row, 1)],
                                      sem).start(priority=u % 2)
            return c
        lax.fori_loop(0, tm // SUBLANES, group, 0)
    for k in range(TOP_K):
        pltpu.make_async_copy(src_ref, dst_ref.at[pl.ds(0, tm)], sem).wait()


def _dispatch(dest_flat, fill_blocks, nfill, u2p, total_rows, tm=1024):
    t, ws, wl = u2p.shape
    kern = functools.partial(_dispatch_kernel, tm=tm, t_total=t)
    return pl.pallas_call(
        kern,
        grid_spec=pltpu.PrefetchScalarGridSpec(
            num_scalar_prefetch=3,
            grid=(t // tm,),
            in_specs=[pl.BlockSpec((tm, ws, wl), lambda i, d, fb, nf: (i, 0, 0))],
            out_specs=pl.BlockSpec(memory_space=pl.ANY),
            scratch_shapes=[pltpu.SemaphoreType.DMA(()), pltpu.SemaphoreType.DMA(())]),
        out_shape=jax.ShapeDtypeStruct((total_rows, ws, wl), U32),
        compiler_params=_params(("arbitrary",)),
        name="dispatch",
    )(dest_flat, fill_blocks, nfill, u2p)


def _moe_kernel(che_ref, chblk_ref, chn_ref, nvalid_ref,
                xs_ref, wg_ref, wu_ref, wd_ref, bg_ref, bu_ref, bd_ref,
                ys_ref,
                xin_ref, xbf_ref, yacc_ref, act_ref, in_sem, out_sem,
                *, nff, nsplit, mm_rows):
    c = pl.program_id(0)
    f = pl.program_id(1)
    nvalid = nvalid_ref[0]
    half = xin_ref.shape[1]

    class _InCopies:
        def __init__(self, cc, j):
            row = pl.multiple_of((chblk_ref[cc] + j) * ROW_BLOCK, ROW_BLOCK)
            dst_rows = pl.ds(pl.multiple_of(j * ROW_BLOCK, ROW_BLOCK), ROW_BLOCK)
            self.copies = [
                pltpu.make_async_copy(xs_ref.at[pl.ds(row, ROW_BLOCK), s],
                                      xin_ref.at[dst_rows, pl.ds(s * LANES, LANES)], in_sem)
                for s in range(xs_ref.shape[1])]

        def start(self):
            for cp in self.copies:
                cp.start()

        def wait(self):
            for cp in self.copies:
                cp.wait()

    in_copy = _InCopies

    def out_copy(cc, j):
        row = pl.multiple_of((chblk_ref[cc] + j) * ROW_BLOCK, ROW_BLOCK)
        return pltpu.make_async_copy(yacc_ref.at[pl.ds(j * ROW_BLOCK, ROW_BLOCK)],
                                     ys_ref.at[pl.ds(row, ROW_BLOCK)], out_sem)

    def for_blocks(cc, fn):
        def one(j, carry):
            fn(cc, j)
            return carry
        lax.fori_loop(0, chn_ref[cc], one, 0)

    def chunk_step():
        @pl.when(f == 0)
        def _():
            @pl.when(c == 0)
            def _():
                xbf_ref[...] = jnp.zeros_like(xbf_ref)
                for_blocks(c, lambda cc, j: in_copy(cc, j).start())

            for_blocks(c, lambda cc, j: in_copy(cc, j).wait())

            def unpack(cc, j):
                rows = pl.ds(pl.multiple_of(j * ROW_BLOCK, ROW_BLOCK), ROW_BLOCK)
                w = xin_ref[rows, :]
                xbf_ref[rows, :half] = pltpu.bitcast(w << 16, F32).astype(BF16)
                xbf_ref[rows, half:] = pltpu.bitcast(w & jnp.uint32(0xFFFF0000), F32).astype(BF16)

            for_blocks(c, unpack)

            @pl.when(c + 1 < nvalid)
            def _():
                for_blocks(c + 1, lambda cc, j: in_copy(cc, j).start())

        mm_blocks = mm_rows // ROW_BLOCK
        rest = jnp.maximum(chn_ref[c] - mm_blocks, 0)
        n_full = rest // mm_blocks
        rem = rest - n_full * mm_blocks
        tail0 = pl.multiple_of((1 + n_full) * mm_rows, ROW_BLOCK)

        def for_tail(fn):
            for r in range(1, mm_blocks):
                pl.when(rem == r)(functools.partial(fn, pl.ds(tail0, r * ROW_BLOCK)))

        def gate_up(rows):
            x = xbf_ref[rows, :]
            g = jnp.minimum(jnp.dot(x, wg_ref[...].astype(BF16), preferred_element_type=F32)
                            + bg_ref[...], SWIGLU_LIMIT)
            u = jnp.clip(jnp.dot(x, wu_ref[...].astype(BF16), preferred_element_type=F32)
                         + bu_ref[...], -SWIGLU_LIMIT, SWIGLU_LIMIT)
            act_ref[rows, :] = ((u + 1.0) * (g * jax.nn.sigmoid(SWIGLU_ALPHA * g))).astype(BF16)

        def gate_up_body(rb, carry):
            gate_up(pl.ds(pl.multiple_of(rb * mm_rows, mm_rows), mm_rows))
            return carry

        lax.fori_loop(0, 1 + n_full, gate_up_body, 0)
        for_tail(gate_up)

        @pl.when((f == 0) & (c > 0))
        def _():
            for_blocks(c - 1, lambda cc, j: out_copy(cc, j).wait())

        first = f == 0
        ncol = yacc_ref.shape[1] // nsplit

        def down(rows):
            a = act_ref[rows, :]
            for s in range(nsplit):
                cols = slice(s * ncol, (s + 1) * ncol)
                part = jnp.dot(a, wd_ref[:, cols].astype(BF16), preferred_element_type=F32)
                base = jnp.where(first, jnp.broadcast_to(bd_ref[:, cols], part.shape),
                                 yacc_ref[rows, cols])
                yacc_ref[rows, cols] = base + part

        def down_body(rb, carry):
            down(pl.ds(pl.multiple_of(rb * mm_rows, mm_rows), mm_rows))
            return carry

        lax.fori_loop(0, 1 + n_full, down_body, 0)
        for_tail(down)

        @pl.when(f == nff - 1)
        def _():
            for_blocks(c, lambda cc, j: out_copy(cc, j).start())

            @pl.when(c == nvalid - 1)
            def _():
                for_blocks(c, lambda cc, j: out_copy(cc, j).wait())

                def fill(b):
                    rows = pl.ds(pl.multiple_of(b * ROW_BLOCK, ROW_BLOCK), ROW_BLOCK)
                    return pltpu.make_async_copy(yacc_ref.at[pl.ds(0, ROW_BLOCK)], ys_ref.at[rows],
                                                 out_sem)

                def fill_start(b, carry):
                    fill(b).start()
                    return carry

                def fill_wait(b, carry):
                    fill(b).wait()
                    return carry

                nused = chblk_ref[c] + chn_ref[c]
                ntotal = ys_ref.shape[0] // ROW_BLOCK
                lax.fori_loop(nused, ntotal, fill_start, 0)
                lax.fori_loop(nused, ntotal, fill_wait, 0)

    chunk_step()


def _moe(che, chblk, chn, nvalid, xs, wg, wu, wd, bg, bu, bd):
    ne, d, dff = wg.shape
    rows = CHUNK_BLOCKS * ROW_BLOCK
    nff = dff // FF_TILE

    kern = functools.partial(_moe_kernel, nff=nff, nsplit=4, mm_rows=MM_ROW_BLOCKS * ROW_BLOCK)
    return pl.pallas_call(
        kern,
        grid_spec=pltpu.PrefetchScalarGridSpec(
            num_scalar_prefetch=4,
            grid=(nvalid[0], nff),
            in_specs=[pl.BlockSpec(memory_space=pl.ANY),
                      pl.BlockSpec((None, d, FF_TILE), lambda c, f, e, b, n, nv: (e[c], 0, f)),
                      pl.BlockSpec((None, d, FF_TILE), lambda c, f, e, b, n, nv: (e[c], 0, f)),
                      pl.BlockSpec((None, FF_TILE, d), lambda c, f, e, b, n, nv: (e[c], f, 0)),
                      pl.BlockSpec((None, 1, FF_TILE), lambda c, f, e, b, n, nv: (e[c], 0, f)),
                      pl.BlockSpec((None, 1, FF_TILE), lambda c, f, e, b, n, nv: (e[c], 0, f)),
                      pl.BlockSpec((None, 1, d), lambda c, f, e, b, n, nv: (e[c], 0, 0))],
            out_specs=pl.BlockSpec(memory_space=pl.ANY),
            scratch_shapes=[pltpu.VMEM((rows, d // 2), U32),
                            pltpu.VMEM((rows, d), BF16),
                            pltpu.VMEM((rows, d), F32),
                            pltpu.VMEM((rows, FF_TILE), BF16),
                            pltpu.SemaphoreType.DMA(()),
                            pltpu.SemaphoreType.DMA(())]),
        out_shape=jax.ShapeDtypeStruct((xs.shape[0], d), F32),
        compiler_params=_params(("arbitrary", "arbitrary")),
        name="moe",
    )(che, chblk, chn, nvalid, xs, wg, wu, wd, bg, bu, bd)


def _combine_kernel(dest_ref, ys_ref, h_ref, p_ref, g2_ref, fg_ref, o_ref, buf_ref, sem,
                    *, tc, nsteps, final_norm):
    i = pl.program_id(0)
    t_total = nsteps * tc

    def issue(step, slot):
        for k in range(TOP_K):
            def group(g, c):
                j0 = pl.multiple_of(g * SUBLANES, SUBLANES)
                tile = buf_ref.at[slot, k, pl.ds(j0, SUBLANES)]
                for u in range(SUBLANES):
                    src = dest_ref[k * t_total + step * tc + j0 + u]
                    pltpu.make_async_copy(ys_ref.at[pl.ds(src, 1)], tile.at[pl.ds(u, 1)],
                                          sem.at[slot]).start(priority=u % 2)
                return c
            lax.fori_loop(0, tc // SUBLANES, group, 0)

    def drain(slot):
        for k in range(TOP_K):
            pltpu.make_async_copy(ys_ref.at[pl.ds(0, tc)], buf_ref.at[slot, k], sem.at[slot]).wait()

    slot = i % 2

    @pl.when(i == 0)
    def _():
        issue(0, 0)

    @pl.when(i + 1 < nsteps)
    def _():
        issue(i + 1, 1 - slot)

    drain(slot)
    p = p_ref[...]
    moe = p[:, 0:1] * buf_ref[slot, 0]
    for k in range(1, TOP_K):
        moe += p[:, k:k + 1] * buf_ref[slot, k]
    h = h_ref[...] + g2_ref[...] * moe
    if final_norm:
        h = h * lax.rsqrt(jnp.mean(h * h, axis=-1, keepdims=True) + EPS) * fg_ref[...]
    o_ref[...] = h


def _combine(dest_flat, ys, h, probs_t, mod3, final_g, seq, final_norm, tc=256):
    t, d = h.shape
    per_b = seq // tc
    nsteps = t // tc
    kern = functools.partial(_combine_kernel, tc=tc, nsteps=nsteps, final_norm=final_norm)
    return pl.pallas_call(
        kern,
        grid_spec=pltpu.PrefetchScalarGridSpec(
            num_scalar_prefetch=1,
            grid=(nsteps,),
            in_specs=[pl.BlockSpec(memory_space=pl.ANY),
                      pl.BlockSpec((tc, d), lambda i, s: (i, 0)),
                      pl.BlockSpec((tc, TOP_K), lambda i, s: (i, 0)),
                      pl.BlockSpec((None, 1, d), lambda i, s: ((i // per_b) * N_MOD + 5, 0, 0)),
                      pl.BlockSpec((1, d), lambda i, s: (0, 0))],
            out_specs=pl.BlockSpec((tc, d), lambda i, s: (i, 0)),
            scratch_shapes=[pltpu.VMEM((2, TOP_K, tc, d), F32),
                            pltpu.SemaphoreType.DMA((2,))]),
        out_shape=jax.ShapeDtypeStruct((t, d), F32),
        compiler_params=_params(("arbitrary",)),
        name="combine",
    )(dest_flat, ys, h, probs_t, mod3, final_g)


def _routing_plan(counts, t):
    nblk_total = -(-t * TOP_K // ROW_BLOCK) + N_EXPERTS
    nblk = (counts + ROW_BLOCK - 1) // ROW_BLOCK
    blk_end = jnp.cumsum(nblk)
    blk_start = blk_end - nblk
    ntail_max = nblk_total - t * TOP_K // ROW_BLOCK
    fill_blocks = jnp.concatenate([jnp.maximum(blk_end - 1, 0),
                                   jnp.minimum(blk_end[-1] + jnp.arange(ntail_max), nblk_total - 1)])
    nfill = N_EXPERTS + nblk_total - blk_end[-1]
    max_chunks = -(-nblk_total // CHUNK_BLOCKS) + N_EXPERTS
    nch = (nblk + CHUNK_BLOCKS - 1) // CHUNK_BLOCKS
    ch_end = jnp.cumsum(nch)
    ch_start = ch_end - nch
    nvalid = ch_end[-1]
    cid = jnp.minimum(jnp.arange(max_chunks, dtype=I32), nvalid - 1)
    che =jnp.clip(jnp.searchsorted(ch_end, cid, side='right'), 0, N_EXPERTS - 1).astype(I32)
    local = cid - ch_start[che]
    nch_e = jnp.maximum(nch[che], 1)
    n_mm = (nblk[che] + MM_ROW_BLOCKS - 1) // MM_ROW_BLOCKS
    size = n_mm // nch_e
    extra = n_mm % nch_e
    first_mm = local * size + jnp.minimum(local, extra)
    chblk = (blk_start[che] + first_mm * MM_ROW_BLOCKS).astype(I32)
    chn = jnp.minimum((size + (local < extra)) * MM_ROW_BLOCKS,
                      nblk[che] - first_mm * MM_ROW_BLOCKS).astype(I32)
    return (fill_blocks.astype(I32), nfill.astype(I32).reshape(1), che, chblk, chn,
            nvalid.astype(I32).reshape(1), nblk_total * ROW_BLOCK)


def kernel(x, c, w_ada, b_ada, norm1_g, w_in, lam_q1, lam_k1, lam_q2, lam_k2, subln_g, w_out_a, w_out_b, w_o, norm2_g, w_router, b_router, w_gate, b_gate, w_up, b_up, w_down, b_down, final_g):
    batch, seq, d = x.shape
    t = batch * seq
    depth = w_ada.shape[0]
    wa = N_HEADS_A * HEAD_DIM
    wb = N_HEADS_B * 2 * HEAD_DIM
    slopes = _alibi_slopes(N_HEADS_A + N_HEADS_B)
    slopes_a = jnp.asarray(slopes[:N_HEADS_A])
    slopes_b = jnp.asarray(slopes[N_HEADS_A:])
    c8 = jnp.pad(c, ((0, 8 - batch), (0, 0)))

    h = x.reshape(t, d)
    for l in range(depth):
        lambda_init = 0.8 - 0.6 * math.exp(-0.3 * l)
        mod = _ada(c8, w_ada[l], b_ada[l].reshape(1, -1))[:batch]
        mod3 = mod.reshape(batch * N_MOD, 1, d)

        u = _norm_mod(h, norm1_g[l].reshape(1, d), mod3, 1, 0, seq)
        proj = _inproj(u, w_in[l], q_tiles=(0, 3 * wa // 1024))
        y_a = _dil_attention(proj, slopes_a, batch, seq, 0, wa // HEAD_DIM, 2 * wa // HEAD_DIM)
        off_b = 3 * wa // (2 * HEAD_DIM)
        nb = wb // (2 * HEAD_DIM)
        y_b = _diff_attention(proj, slopes_b,
                              [v[l].reshape(1, HEAD_DIM) for v in (lam_q1, lam_k1, lam_q2, lam_k2)],
                              subln_g[l].reshape(1, -1), lambda_init, batch, seq,
                              off_b, off_b + nb, off_b + 2 * nb)
        wr_hi, wr_lo = _split_bf16(w_router[l].T)
        gate_col = (3 * wa + 3 * wb) // d
        h, u2p, logits_t = _merge(y_a, y_b, proj, h, mod3, norm2_g[l].reshape(1, d),
                                  w_out_a[l].astype(BF16), w_out_b[l].astype(BF16),
                                  w_o[l].astype(BF16), wr_hi, wr_lo, b_router[l].reshape(-1, 1),
                                  seq, gate_col, gate_col + 1)
        probs, dest, cnt = _route(logits_t)
        (fill_blocks, nfill, che, chblk, chn, nvalid, total_rows) = _routing_plan(cnt[:, 0], t)
        xs = _dispatch(dest.reshape(-1), fill_blocks, nfill, u2p, total_rows)
        ys = _moe(che, chblk, chn, nvalid, xs, w_gate[l], w_up[l], w_down[l],
                  b_gate[l].reshape(N_EXPERTS, 1, -1), b_up[l].reshape(N_EXPERTS, 1, -1),
                  b_down[l].reshape(N_EXPERTS, 1, -1))
        h = _combine(dest.reshape(-1), ys, h, probs.T, mod3, final_g.reshape(1, d), seq,
                     final_norm=(l == depth - 1))
    return h.reshape(batch, seq, d)
```

```python
import functools
import math

import numpy as np
import jax
import jax.numpy as jnp
from jax import lax
from jax.experimental import pallas as pl
from jax.experimental.pallas import tpu as pltpu

F32 = jnp.float32
BF16 = jnp.bfloat16
U32 = jnp.uint32
I32 = jnp.int32

HEAD_DIM = 128
N_HEADS_A = 8
N_HEADS_B = 4
DILATED_PATTERNS = ((128, 1), (512, 4), (2048, 16))
N_EXPERTS = 32
TOP_K = 4
SWIGLU_LIMIT = 7.0
SWIGLU_ALPHA = 1.702
N_MOD = 6
EPS = 1e-5
NEG = -1e30
LOG2E = math.log2(math.e)

LANES = 128
SUBLANES = 8
V7X_VMEM_LIMIT = 56 * 1024 * 1024

ROW_BLOCK = 128
CHUNK_BLOCKS = 12
MM_ROW_BLOCKS = 4
FF_TILE = 512


def _alibi_slopes(n):
    return np.array([2.0 ** (-8.0 * (i + 1) / n) for i in range(n)], dtype=np.float32)


def _nt_dot(a, b):
    return lax.dot_general(a, b, (((1,), (1,)), ((), ())), preferred_element_type=F32)


def _split_bf16(x):
    hi = x.astype(BF16)
    lo = (x - hi.astype(F32)).astype(BF16)
    return hi, lo


def _params(sem, vmem=V7X_VMEM_LIMIT):
    return pltpu.CompilerParams(dimension_semantics=sem, vmem_limit_bytes=vmem)


def _ada_kernel(c_ref, w_ref, b_ref, o_ref):
    c = c_ref[...]
    a = c * jax.nn.sigmoid(c)
    a_hi, a_lo = _split_bf16(a)
    w_hi, w_lo = _split_bf16(w_ref[...])
    m = a.shape[0]
    both = jnp.dot(jnp.concatenate([a_hi, a_lo], axis=0), w_hi, preferred_element_type=F32)
    acc = both[:m] + both[m:] + jnp.dot(a_hi, w_lo, preferred_element_type=F32)
    o_ref[...] = acc + b_ref[...]


def _ada(c8, w, b, tn=1024):
    m, d = c8.shape
    n = w.shape[1]
    return pl.pallas_call(
        _ada_kernel,
        grid=(n // tn,),
        in_specs=[pl.BlockSpec((m, d), lambda j: (0, 0)),
                  pl.BlockSpec((d, tn), lambda j: (0, j)),
                  pl.BlockSpec((1, tn), lambda j: (0, j))],
        out_specs=pl.BlockSpec((m, tn), lambda j: (0, j)),
        out_shape=jax.ShapeDtypeStruct((m, n), F32),
        compiler_params=_params(("arbitrary",)),
        name="ada",
    )(c8, w, b)


def _norm_mod_kernel(x_ref, g_ref, sc_ref, sh_ref, o_ref):
    x = x_ref[...]
    y = x * lax.rsqrt(jnp.mean(x * x, axis=-1, keepdims=True) + EPS) * g_ref[...]
    o_ref[...] = (y * (1.0 + sc_ref[...]) + sh_ref[...]).astype(o_ref.dtype)


def _norm_mod(x2, g, mod3, i_scale, i_shift, seq, tm=1024):
    t, d = x2.shape
    per_b = seq // tm
    return pl.pallas_call(
        _norm_mod_kernel,
        grid=(t // tm,),
        in_specs=[pl.BlockSpec((tm, d), lambda i: (i, 0)),
                  pl.BlockSpec((1, d), lambda i: (0, 0)),
                  pl.BlockSpec((None, 1, d), lambda i: ((i // per_b) * N_MOD + i_scale, 0, 0)),
                  pl.BlockSpec((None, 1, d), lambda i: ((i // per_b) * N_MOD + i_shift, 0, 0))],
        out_specs=pl.BlockSpec((tm, d), lambda i: (i, 0)),
        out_shape=jax.ShapeDtypeStruct((t, d), BF16),
        compiler_params=_params(("arbitrary",)),
        name="norm1",
    )(x2, g, mod3, mod3)


def _inproj_kernel(u_ref, w_ref, o_ref, *, q_tiles, scale):
    n = pl.program_id(0)
    acc = jnp.dot(u_ref[...], w_ref[...].astype(BF16), preferred_element_type=F32)
    is_q = functools.reduce(jnp.logical_or, [n == q for q in q_tiles])
    o_ref[...] = (acc * jnp.where(is_q, scale, 1.0)).astype(BF16)


def _inproj(u, w, q_tiles, tm=2048, tn=1024):
    t, d = u.shape
    n = w.shape[1]
    return pl.pallas_call(
        functools.partial(_inproj_kernel, q_tiles=q_tiles, scale=HEAD_DIM ** -0.5 * LOG2E),
        grid=(n // tn, t // tm),
        in_specs=[pl.BlockSpec((tm, d), lambda j, i: (i, 0)),
                  pl.BlockSpec((d, tn), lambda j, i: (0, j))],
        out_specs=pl.BlockSpec((tm, tn), lambda j, i: (i, j)),
        out_shape=jax.ShapeDtypeStruct((t, n), BF16),
        compiler_params=_params(("arbitrary", "arbitrary")),
        name="inproj",
    )(u, w)


def _dil_tables(tq, span):
    nside = span // tq
    o = lax.broadcasted_iota(I32, (2 * nside + 1, tq, tq), 0) - nside
    i = lax.broadcasted_iota(I32, (2 * nside + 1, tq, tq), 1)
    j = lax.broadcasted_iota(I32, (2 * nside + 1, tq, tq), 2)
    ad = jnp.abs(o * tq + j - i)
    mult = jnp.zeros_like(ad)
    for window, dil in DILATED_PATTERNS:
        mult += ((ad % dil == 0) & (ad // dil <= window // (2 * dil))).astype(I32)
    lm = jnp.where(mult > 0, jnp.log2(jnp.maximum(mult, 1).astype(F32)), NEG)
    return ad.astype(F32) * LOG2E, lm


def _dil_kernel(slopes_ref, q_ref, k_ref, v_ref, ad_ref, lm_ref, o_ref, bias_ref, vt_ref, st_ref,
                *, tq, nside, nq, heads_per_step):
    hp = pl.program_id(1)
    i = pl.program_id(2)
    heads = [slice(a * HEAD_DIM, (a + 1) * HEAD_DIM) for a in range(heads_per_step)]

    @pl.when(i == 0)
    def _():
        for a, cols in enumerate(heads):
            slope = slopes_ref[hp * heads_per_step + a]
            bias_ref[a, :2 * nside + 1] = lm_ref[...] - slope * ad_ref[...]
            bias_ref[a, 2 * nside + 1] = jnp.full((tq, tq), NEG, F32)
            for kb in range(nq):
                vt_ref[a, kb] = v_ref[kb * tq:(kb + 1) * tq, cols].astype(F32).T.astype(BF16)

    nwin = 2 * nside + 1
    b0 = jnp.clip(i - nside, 0, nq - nwin)
    q = q_ref[...]
    ms = []
    for a, cols in enumerate(heads):
        m = None
        for j in range(nwin):
            o = b0 + j - i
            plane = jnp.where(jnp.abs(o) <= nside, nside - o, nwin)
            kb = k_ref[pl.ds(pl.multiple_of((b0 + j) * tq, tq), tq), cols]
            s = _nt_dot(kb, q[:, cols]) + bias_ref[a, plane]
            st_ref[a, j] = s
            cm = jnp.max(s, axis=0, keepdims=True)
            m = cm if j == 0 else jnp.maximum(m, cm)
        ms.append(m)
    for a, cols in enumerate(heads):
        l = acc = None
        for j in range(nwin):
            p = jnp.exp2(st_ref[a, j] - ms[a])
            cl = jnp.sum(p, axis=0, keepdims=True)
            ca = jnp.dot(vt_ref[a, b0 + j], p.astype(BF16), preferred_element_type=F32)
            l = cl if j == 0 else l + cl
            acc = ca if j == 0 else acc + ca
        o_ref[:, cols] = (acc / l).T.astype(o_ref.dtype)


def _dil_attention(proj, slopes, batch, seq, q_col, k_col, v_col, tq=512):
    span = max(w // 2 for w, _ in DILATED_PATTERNS)
    nside = span // tq
    nq = seq // tq
    ad, lm = _dil_tables(tq, span)
    hps = 2
    w = hps * HEAD_DIM
    qc, kc, vc = q_col // hps, k_col // hps, v_col // hps
    tab_spec = pl.BlockSpec((2 * nside + 1, tq, tq), lambda b, h, i, s: (0, 0, 0),
                            pipeline_mode=pl.Buffered(1))
    kern = functools.partial(_dil_kernel, tq=tq, nside=nside, nq=nq, heads_per_step=hps)
    return pl.pallas_call(
        kern,
        grid_spec=pltpu.PrefetchScalarGridSpec(
            num_scalar_prefetch=1,
            grid=(batch, N_HEADS_A // hps, nq),
            in_specs=[pl.BlockSpec((tq, w), lambda b, h, i, s: (b * nq + i, qc + h)),
                      pl.BlockSpec((seq, w), lambda b, h, i, s: (b, kc + h)),
                      pl.BlockSpec((seq, w), lambda b, h, i, s: (b, vc + h)),
                      tab_spec, tab_spec],
            out_specs=pl.BlockSpec((tq, w), lambda b, h, i, s: (b * nq + i, h)),
            scratch_shapes=[pltpu.VMEM((hps, 2 * nside + 2, tq, tq), F32),
                            pltpu.VMEM((hps, nq, HEAD_DIM, tq), BF16),
                            pltpu.VMEM((hps, 2 * nside + 1, tq, tq), F32)]),
        out_shape=jax.ShapeDtypeStruct((batch * seq, N_HEADS_A * HEAD_DIM), BF16),
        compiler_params=_params(("arbitrary", "arbitrary", "arbitrary")),
        name="dil_attn",
    )(slopes, proj, proj, proj, ad, lm)


def _diff_kernel(slopes_ref, lq1_ref, lk1_ref, lq2_ref, lk2_ref, g_ref, q_ref, k_ref, v_ref, o_ref,
                 vt_ref, bias_ref, st1_ref, st2_ref, *, tq, tk, nk, lambda_init):
    h = pl.program_id(1)
    i = pl.program_id(2)

    seq = k_ref.shape[0]
    nq = seq // tq

    @pl.when(i == 0)
    def _():
        for kb in range(nk):
            cols = slice(kb * tk, (kb + 1) * tk)
            vt_ref[:, cols] = v_ref[cols, :].astype(F32).T.astype(BF16)

        neg_c = -slopes_ref[h] * LOG2E
        base = (lax.broadcasted_iota(I32, (tq, tq), 0) - lax.broadcasted_iota(I32, (tq, tq), 1)
                - (seq - tq))

        def fill(r, carry):
            rows = pl.ds(pl.multiple_of(r * tq, tq), tq)
            bias_ref[rows, :] = jnp.abs(base + r * tq).astype(F32) * neg_c
            return carry

        lax.fori_loop(0, 2 * nq - 1, fill, 0)

    q = q_ref[...]
    qs = (q[:, :HEAD_DIM], q[:, HEAD_DIM:])
    st_refs = (st1_ref, st2_ref)
    w0 = (nq - 1 - i) * tq

    def score_chunk(c, mx):
        rows = pl.ds(pl.multiple_of(c * tk, tk), tk)
        kc = k_ref[rows, :]
        b = bias_ref[pl.ds(pl.multiple_of(w0 + c * tk, tq), tk), :]
        out = []
        for j in range(2):
            s = _nt_dot(kc[:, j * HEAD_DIM:(j + 1) * HEAD_DIM], qs[j]) + b
            st_refs[j][rows, :] = s
            out.append(jnp.maximum(mx[j], jnp.max(s, axis=0, keepdims=True)))
        return tuple(out)

    neg = jnp.full((1, tq), NEG, F32)
    mx = lax.fori_loop(0, nk, score_chunk, (neg, neg), unroll=4)
    ls = [None, None]
    accs = [None, None]
    for c in range(nk):
        rows = slice(c * tk, (c + 1) * tk)
        vt = vt_ref[:, rows]
        for j in range(2):
            p = jnp.exp2(st_refs[j][rows, :] - mx[j])
            cl = jnp.sum(p, axis=0, keepdims=True)
            ca = jnp.dot(vt, p.astype(BF16), preferred_element_type=F32)
            ls[j] = cl if c == 0 else ls[j] + cl
            accs[j] = ca if c == 0 else accs[j] + ca
    (a1, a2), (l1, l2) = accs, ls

    lam = (jnp.exp(jnp.sum(lq1_ref[...] * lk1_ref[...], axis=-1, keepdims=True))
           - jnp.exp(jnp.sum(lq2_ref[...] * lk2_ref[...], axis=-1, keepdims=True)) + lambda_init)
    ot = a1 / l1 - lam * (a2 / l2)
    yt = ot * lax.rsqrt(jnp.mean(ot * ot, axis=0, keepdims=True) + EPS)
    o_ref[...] = (yt.T * (g_ref[...] * (1.0 - lambda_init))).astype(o_ref.dtype)


def _diff_attention(proj, slopes, lam_vecs, subln_g, lambda_init, batch, seq, q_col, k_col, v_col,
                    tq=512, tk=512):
    nq = seq // tq
    w = 2 * HEAD_DIM
    vec_spec = pl.BlockSpec((1, HEAD_DIM), lambda b, h, i, s: (0, 0))
    kern = functools.partial(_diff_kernel, tq=tq, tk=tk, nk=seq // tk, lambda_init=lambda_init)
    return pl.pallas_call(
        kern,
        grid_spec=pltpu.PrefetchScalarGridSpec(
            num_scalar_prefetch=1,
            grid=(batch, N_HEADS_B, nq),
            in_specs=[vec_spec, vec_spec, vec_spec, vec_spec,
                      pl.BlockSpec((1, w), lambda b, h, i, s: (0, 0)),
                      pl.BlockSpec((tq, w), lambda b, h, i, s: (b * nq + i, q_col + h)),
                      pl.BlockSpec((seq, w), lambda b, h, i, s: (b, k_col + h)),
                      pl.BlockSpec((seq, w), lambda b, h, i, s: (b, v_col + h))],
            out_specs=pl.BlockSpec((tq, w), lambda b, h, i, s: (b * nq + i, h)),
            scratch_shapes=[pltpu.VMEM((w, seq), BF16),
                            pltpu.VMEM((2 * seq - tq, tq), F32),
                            pltpu.VMEM((seq, tq), F32),
                            pltpu.VMEM((seq, tq), F32)]),
        out_shape=jax.ShapeDtypeStruct((batch * seq, N_HEADS_B * w), BF16),
        compiler_params=_params(("arbitrary", "arbitrary", "arbitrary")),
        name="diff_attn",
    )(slopes, *lam_vecs, subln_g, proj, proj, proj)


def _merge_kernel(ya_ref, yb_ref, ga_ref, gb_ref, x_ref, g1_ref, sc2_ref, sh2_ref, n2g_ref,
                  woa_ref, wob_ref, wo_ref, wrh_ref, wrl_ref, br_ref,
                  h_ref, u2p_ref, lg_ref):
    a = jnp.dot(ya_ref[...], woa_ref[...], preferred_element_type=F32)
    b = jnp.dot(yb_ref[...], wob_ref[...], preferred_element_type=F32)
    merged = (jax.nn.sigmoid(ga_ref[...].astype(F32)) * a
              + jax.nn.sigmoid(gb_ref[...].astype(F32)) * b)
    h = x_ref[...] + g1_ref[...] * jnp.dot(merged.astype(BF16), wo_ref[...],
                                           preferred_element_type=F32)
    h_ref[...] = h
    y = h * lax.rsqrt(jnp.mean(h * h, axis=-1, keepdims=True) + EPS) * n2g_ref[...]
    u2 = y * (1.0 + sc2_ref[...]) + sh2_ref[...]
    hi, lo = _split_bf16(u2)
    lg = _nt_dot(wrh_ref[...], hi) + _nt_dot(wrl_ref[...], hi) + _nt_dot(wrh_ref[...], lo)
    lg_ref[...] = lg + br_ref[...]
    bits = pltpu.bitcast(hi.astype(F32), U32)
    half = bits.shape[1] // 2
    word = (bits[:, :half] >> 16) | (bits[:, half:] & jnp.uint32(0xFFFF0000))
    for j in range(half // LANES):
        u2p_ref[:, j, :] = word[:, j * LANES:(j + 1) * LANES]


def _merge(ya, yb, proj, x2, mod3, n2g, woa, wob, wo, wr_hi, wr_lo, br, seq, ga_col, gb_col, tm=512):
    t, d = x2.shape
    per_b = seq // tm
    wa = ya.shape[1]
    wb = yb.shape[1]
    ne = wr_hi.shape[0]

    def mod_spec(idx):
        return pl.BlockSpec((None, 1, d), lambda i: ((i // per_b) * N_MOD + idx, 0, 0))

    def const_spec(shape):
        return pl.BlockSpec(shape, lambda i: (0,) * len(shape), pipeline_mode=pl.Buffered(1))

    return pl.pallas_call(
        _merge_kernel,
        grid=(t // tm,),
        in_specs=[pl.BlockSpec((tm, wa), lambda i: (i, 0)),
                  pl.BlockSpec((tm, wb), lambda i: (i, 0)),
                  pl.BlockSpec((tm, d), lambda i: (i, ga_col)),
                  pl.BlockSpec((tm, d), lambda i: (i, gb_col)),
                  pl.BlockSpec((tm, d), lambda i: (i, 0)),
                  mod_spec(2), mod_spec(4), mod_spec(3),
                  const_spec((1, d)),
                  const_spec((wa, d)), const_spec((wb, d)), const_spec((d, d)),
                  const_spec((ne, d)), const_spec((ne, d)), const_spec((ne, 1))],
        out_specs=[pl.BlockSpec((tm, d), lambda i: (i, 0)),
                   pl.BlockSpec((tm, d // 2 // LANES, LANES), lambda i: (i, 0, 0)),
                   pl.BlockSpec((ne, tm), lambda i: (0, i))],
        out_shape=[jax.ShapeDtypeStruct((t, d), F32),
                   jax.ShapeDtypeStruct((t, d // 2 // LANES, LANES), U32),
                   jax.ShapeDtypeStruct((ne, t), F32)],
        compiler_params=_params(("arbitrary",)),
        name="merge",
    )(ya, yb, proj, proj, x2, mod3, mod3, mod3, n2g, woa, wob, wo, wr_hi, wr_lo, br)


def _route_kernel(lg_ref, p_ref, dest_ref, cnt_ref, carry_ref, total_ref):
    sweep = pl.program_id(0)

    @pl.when(pl.program_id(1) == 0)
    def _():
        @pl.when(sweep == 0)
        def _():
            total_ref[...] = jnp.zeros_like(total_ref)

        @pl.when(sweep == 1)
        def _():
            total_ref[...] = carry_ref[...]

        carry_ref[...] = jnp.zeros_like(carry_ref)

    lg = lg_ref[...]
    ne, tr = lg.shape
    eio = lax.broadcasted_iota(I32, (ne, tr), 0)
    work = lg
    vals, hots = [], []
    for k in range(TOP_K):
        mx = jnp.max(work, axis=0, keepdims=True)
        am = jnp.min(jnp.where(work == mx, eio, ne), axis=0, keepdims=True)
        hot = eio == am
        vals.append(mx)
        hots.append(hot)
        work = jnp.where(hot, -jnp.inf, work)
    exps = [jnp.exp(v - vals[0]) for v in vals]
    denom = functools.reduce(jnp.add, exps)
    for k in range(TOP_K):
        p_ref[k:k + 1, :] = exps[k] / denom
    chosen = functools.reduce(jnp.logical_or, hots)
    sel = jnp.where(chosen, 1.0, 0.0)
    tri = (lax.broadcasted_iota(I32, (tr, tr), 0) < lax.broadcasted_iota(I32, (tr, tr), 1))
    before = jnp.dot(sel.astype(BF16), jnp.where(tri, 1.0, 0.0).astype(BF16),
                     preferred_element_type=F32)
    carry = carry_ref[...]
    nblk = jnp.floor((total_ref[...][:, 0:1] + (ROW_BLOCK - 1)) * (1.0 / ROW_BLOCK))
    e_row = lax.broadcasted_iota(I32, (ne, ne), 0)
    e_col = lax.broadcasted_iota(I32, (ne, ne), 1)
    nblk_lanes = jnp.sum(jnp.where(e_row == e_col, nblk, 0.0), axis=0, keepdims=True)
    first_row = jnp.sum(jnp.where(e_col < e_row, nblk_lanes, 0.0), axis=1,
                        keepdims=True) * ROW_BLOCK
    place = before + (carry[:, 0:1] + first_row)
    for k in range(TOP_K):
        dest_ref[k:k + 1, :] = jnp.sum(jnp.where(hots[k], place, 0.0), axis=0,
                                       keepdims=True).astype(I32)
    carry = carry + jnp.sum(sel, axis=1, keepdims=True)
    carry_ref[...] = carry
    cnt_ref[...] = carry.astype(I32)


def _route(logits_t, tr=1024):
    ne, t = logits_t.shape
    slot_spec = pl.BlockSpec((None, TOP_K, tr), lambda s, i: (s, 0, i))
    probs, dest, cnt = pl.pallas_call(
        _route_kernel,
        grid=(2, t // tr),
        in_specs=[pl.BlockSpec((ne, tr), lambda s, i: (0, i))],
        out_specs=[slot_spec, slot_spec,
                   pl.BlockSpec((None, ne, LANES), lambda s, i: (s, 0, 0))],
        out_shape=[jax.ShapeDtypeStruct((2, TOP_K, t), F32),
                   jax.ShapeDtypeStruct((2, TOP_K, t), I32),
                   jax.ShapeDtypeStruct((2, ne, LANES), I32)],
        scratch_shapes=[pltpu.VMEM((ne, LANES), F32), pltpu.VMEM((ne, LANES), F32)],
        compiler_params=_params(("arbitrary", "arbitrary")),
        name="route",
    )(logits_t)
    return probs[1], dest[1], cnt[1]


def _dispatch_kernel(dest_ref, fill_ref, nfill_ref, src_ref, dst_ref, sem, fill_sem, *, tm, t_total):
    i = pl.program_id(0)

    @pl.when(i == 0)
    def _():
        def fill(b):
            rows = pl.ds(pl.multiple_of(fill_ref[b] * ROW_BLOCK, ROW_BLOCK), ROW_BLOCK)
            return pltpu.make_async_copy(src_ref.at[pl.ds(0, ROW_BLOCK)], dst_ref.at[rows], fill_sem)

        def fill_start(b, c):
            fill(b).start()
            return c

        def fill_wait(b, c):
            fill(b).wait()
            return c

        lax.fori_loop(0, nfill_ref[0], fill_start, 0)
        lax.fori_loop(0, nfill_ref[0], fill_wait, 0)

    for k in range(TOP_K):
        def group(g, c):
            j0 = pl.multiple_of(g * SUBLANES, SUBLANES)
            tile = src_ref.at[pl.ds(j0, SUBLANES)]
            for u in range(SUBLANES):
                row = dest_ref[k * t_total + i * tm + j0 + u]
                pltpu.make_async_copy(tile.at[pl.ds(u, 1)], dst_ref.at[pl.ds(row, 1)],
                                      sem).start(priority=u % 2)
            return c
        lax.fori_loop(0, tm // SUBLANES, group, 0)
    for k in range(TOP_K):
        pltpu.make_async_copy(src_ref, dst_ref.at[pl.ds(0, tm)], sem).wait()


def _dispatch(dest_flat, fill_blocks, nfill, u2p, total_rows, tm=1024):
    t, ws, wl = u2p.shape
    kern = functools.partial(_dispatch_kernel, tm=tm, t_total=t)
    return pl.pallas_call(
        kern,
        grid_spec=pltpu.PrefetchScalarGridSpec(
            num_scalar_prefetch=3,
            grid=(t // tm,),
            in_specs=[pl.BlockSpec((tm, ws, wl), lambda i, d, fb, nf: (i, 0, 0))],
            out_specs=pl.BlockSpec(memory_space=pl.ANY),
            scratch_shapes=[pltpu.SemaphoreType.DMA(()), pltpu.SemaphoreType.DMA(())]),
        out_shape=jax.ShapeDtypeStruct((total_rows, ws, wl), U32),
        compiler_params=_params(("arbitrary",)),
        name="dispatch",
    )(dest_flat, fill_blocks, nfill, u2p)


def _moe_kernel(che_ref, chblk_ref, chn_ref, nvalid_ref,
                xs_ref, wg_ref, wu_ref, wd_ref, bg_ref, bu_ref, bd_ref,
                ys_ref,
                xin_ref, xbf_ref, yacc_ref, act_ref, in_sem, out_sem,
                *, nff, nsplit, mm_rows):
    c = pl.program_id(0)
    f = pl.program_id(1)
    nvalid = nvalid_ref[0]
    half = xin_ref.shape[1]

    class _InCopies:
        def __init__(self, cc, j):
            row = pl.multiple_of((chblk_ref[cc] + j) * ROW_BLOCK, ROW_BLOCK)
            dst_rows = pl.ds(pl.multiple_of(j * ROW_BLOCK, ROW_BLOCK), ROW_BLOCK)
            self.copies = [
                pltpu.make_async_copy(xs_ref.at[pl.ds(row, ROW_BLOCK), s],
                                      xin_ref.at[dst_rows, pl.ds(s * LANES, LANES)], in_sem)
                for s in range(xs_ref.shape[1])]

        def start(self):
            for cp in self.copies:
                cp.start()

        def wait(self):
            for cp in self.copies:
                cp.wait()

    in_copy = _InCopies

    def out_copy(cc, j):
        row = pl.multiple_of((chblk_ref[cc] + j) * ROW_BLOCK, ROW_BLOCK)
        return pltpu.make_async_copy(yacc_ref.at[pl.ds(j * ROW_BLOCK, ROW_BLOCK)],
                                     ys_ref.at[pl.ds(row, ROW_BLOCK)], out_sem)

    def for_blocks(cc, fn):
        def one(j, carry):
            fn(cc, j)
            return carry
        lax.fori_loop(0, chn_ref[cc], one, 0)

    def chunk_step():
        @pl.when(f == 0)
        def _():
            @pl.when(c == 0)
            def _():
                xbf_ref[...] = jnp.zeros_like(xbf_ref)
                for_blocks(c, lambda cc, j: in_copy(cc, j).start())

            for_blocks(c, lambda cc, j: in_copy(cc, j).wait())

            def unpack(cc, j):
                rows = pl.ds(pl.multiple_of(j * ROW_BLOCK, ROW_BLOCK), ROW_BLOCK)
                w = xin_ref[rows, :]
                xbf_ref[rows, :half] = pltpu.bitcast(w << 16, F32).astype(BF16)
                xbf_ref[rows, half:] = pltpu.bitcast(w & jnp.uint32(0xFFFF0000), F32).astype(BF16)

            for_blocks(c, unpack)

            @pl.when(c + 1 < nvalid)
            def _():
                for_blocks(c + 1, lambda cc, j: in_copy(cc, j).start())

        mm_blocks = mm_rows // ROW_BLOCK
        rest = jnp.maximum(chn_ref[c] - mm_blocks, 0)
        n_full = rest // mm_blocks
        rem = rest - n_full * mm_blocks
        tail0 = pl.multiple_of((1 + n_full) * mm_rows, ROW_BLOCK)

        def for_tail(fn):
            for r in range(1, mm_blocks):
                pl.when(rem == r)(functools.partial(fn, pl.ds(tail0, r * ROW_BLOCK)))

        def gate_up(rows):
            x = xbf_ref[rows, :]
            g = jnp.minimum(jnp.dot(x, wg_ref[...].astype(BF16), preferred_element_type=F32)
                            + bg_ref[...], SWIGLU_LIMIT)
            u = jnp.clip(jnp.dot(x, wu_ref[...].astype(BF16), preferred_element_type=F32)
                         + bu_ref[...], -SWIGLU_LIMIT, SWIGLU_LIMIT)
            act_ref[rows, :] = ((u + 1.0) * (g * jax.nn.sigmoid(SWIGLU_ALPHA * g))).astype(BF16)

        def gate_up_body(rb, carry):
            gate_up(pl.ds(pl.multiple_of(rb * mm_rows, mm_rows), mm_rows))
            return carry

        lax.fori_loop(0, 1 + n_full, gate_up_body, 0)
        for_tail(gate_up)

        @pl.when((f == 0) & (c > 0))
        def _():
            for_blocks(c - 1, lambda cc, j: out_copy(cc, j).wait())

        first = f == 0
        ncol = yacc_ref.shape[1] // nsplit

        def down(rows):
            a = act_ref[rows, :]
            for s in range(nsplit):
                cols = slice(s * ncol, (s + 1) * ncol)
                part = jnp.dot(a, wd_ref[:, cols].astype(BF16), preferred_element_type=F32)
                base = jnp.where(first, jnp.broadcast_to(bd_ref[:, cols], part.shape),
                                 yacc_ref[rows, cols])
                yacc_ref[rows, cols] = base + part

        def down_body(rb, carry):
            down(pl.ds(pl.multiple_of(rb * mm_rows, mm_rows), mm_rows))
            return carry

        lax.fori_loop(0, 1 + n_full, down_body, 0)
        for_tail(down)

        @pl.when(f == nff - 1)
        def _():
            for_blocks(c, lambda cc, j: out_copy(cc, j).start())

            @pl.when(c == nvalid - 1)
            def _():
                for_blocks(c, lambda cc, j: out_copy(cc, j).wait())

                def fill(b):
                    rows = pl.ds(pl.multiple_of(b * ROW_BLOCK, ROW_BLOCK), ROW_BLOCK)
                    return pltpu.make_async_copy(yacc_ref.at[pl.ds(0, ROW_BLOCK)], ys_ref.at[rows],
                                                 out_sem)

                def fill_start(b, carry):
                    fill(b).start()
                    return carry

                def fill_wait(b, carry):
                    fill(b).wait()
                    return carry

                nused = chblk_ref[c] + chn_ref[c]
                ntotal = ys_ref.shape[0] // ROW_BLOCK
                lax.fori_loop(nused, ntotal, fill_start, 0)
                lax.fori_loop(nused, ntotal, fill_wait, 0)

    chunk_step()


def _moe(che, chblk, chn, nvalid, xs, wg, wu, wd, bg, bu, bd):
    ne, d, dff = wg.shape
    rows = CHUNK_BLOCKS * ROW_BLOCK
    nff = dff // FF_TILE

    kern = functools.partial(_moe_kernel, nff=nff, nsplit=4, mm_rows=MM_ROW_BLOCKS * ROW_BLOCK)
    return pl.pallas_call(
        kern,
        grid_spec=pltpu.PrefetchScalarGridSpec(
            num_scalar_prefetch=4,
            grid=(nvalid[0], nff),
            in_specs=[pl.BlockSpec(memory_space=pl.ANY),
                      pl.BlockSpec((None, d, FF_TILE), lambda c, f, e, b, n, nv: (e[c], 0, f)),
                      pl.BlockSpec((None, d, FF_TILE), lambda c, f, e, b, n, nv: (e[c], 0, f)),
                      pl.BlockSpec((None, FF_TILE, d), lambda c, f, e, b, n, nv: (e[c], f, 0)),
                      pl.BlockSpec((None, 1, FF_TILE), lambda c, f, e, b, n, nv: (e[c], 0, f)),
                      pl.BlockSpec((None, 1, FF_TILE), lambda c, f, e, b, n, nv: (e[c], 0, f)),
                      pl.BlockSpec((None, 1, d), lambda c, f, e, b, n, nv: (e[c], 0, 0))],
            out_specs=pl.BlockSpec(memory_space=pl.ANY),
            scratch_shapes=[pltpu.VMEM((rows, d // 2), U32),
                            pltpu.VMEM((rows, d), BF16),
                            pltpu.VMEM((rows, d), F32),
                            pltpu.VMEM((rows, FF_TILE), BF16),
                            pltpu.SemaphoreType.DMA(()),
                            pltpu.SemaphoreType.DMA(())]),
        out_shape=jax.ShapeDtypeStruct((xs.shape[0], d), F32),
        compiler_params=_params(("arbitrary", "arbitrary")),
        name="moe",
    )(che, chblk, chn, nvalid, xs, wg, wu, wd, bg, bu, bd)


def _combine_kernel(dest_ref, ys_ref, h_ref, p_ref, g2_ref, fg_ref, o_ref, buf_ref, sem,
                    *, tc, nsteps, final_norm):
    i = pl.program_id(0)
    t_total = nsteps * tc

    def issue(step, slot):
        for k in range(TOP_K):
            def group(g, c):
                j0 = pl.multiple_of(g * SUBLANES, SUBLANES)
                tile = buf_ref.at[slot, k, pl.ds(j0, SUBLANES)]
                for u in range(SUBLANES):
                    src = dest_ref[k * t_total + step * tc + j0 + u]
                    pltpu.make_async_copy(ys_ref.at[pl.ds(src, 1)], tile.at[pl.ds(u, 1)],
                                          sem.at[slot]).start(priority=u % 2)
                return c
            lax.fori_loop(0, tc // SUBLANES, group, 0)

    def drain(slot):
        for k in range(TOP_K):
            pltpu.make_async_copy(ys_ref.at[pl.ds(0, tc)], buf_ref.at[slot, k], sem.at[slot]).wait()

    slot = i % 2

    @pl.when(i == 0)
    def _():
        issue(0, 0)

    @pl.when(i + 1 < nsteps)
    def _():
        issue(i + 1, 1 - slot)

    drain(slot)
    p = p_ref[...]
    moe = p[:, 0:1] * buf_ref[slot, 0]
    for k in range(1, TOP_K):
        moe += p[:, k:k + 1] * buf_ref[slot, k]
    h = h_ref[...] + g2_ref[...] * moe
    if final_norm:
        h = h * lax.rsqrt(jnp.mean(h * h, axis=-1, keepdims=True) + EPS) * fg_ref[...]
    o_ref[...] = h


def _combine(dest_flat, ys, h, probs_t, mod3, final_g, seq, final_norm, tc=256):
    t, d = h.shape
    per_b = seq // tc
    nsteps = t // tc
    kern = functools.partial(_combine_kernel, tc=tc, nsteps=nsteps, final_norm=final_norm)
    return pl.pallas_call(
        kern,
        grid_spec=pltpu.PrefetchScalarGridSpec(
            num_scalar_prefetch=1,
            grid=(nsteps,),
            in_specs=[pl.BlockSpec(memory_space=pl.ANY),
                      pl.BlockSpec((tc, d), lambda i, s: (i, 0)),
                      pl.BlockSpec((tc, TOP_K), lambda i, s: (i, 0)),
                      pl.BlockSpec((None, 1, d), lambda i, s: ((i // per_b) * N_MOD + 5, 0, 0)),
                      pl.BlockSpec((1, d), lambda i, s: (0, 0))],
            out_specs=pl.BlockSpec((tc, d), lambda i, s: (i, 0)),
            scratch_shapes=[pltpu.VMEM((2, TOP_K, tc, d), F32),
                            pltpu.SemaphoreType.DMA((2,))]),
        out_shape=jax.ShapeDtypeStruct((t, d), F32),
        compiler_params=_params(("arbitrary",)),
        name="combine",
    )(dest_flat, ys, h, probs_t, mod3, final_g)


def _routing_plan(counts, t):
    nblk_total = -(-t * TOP_K // ROW_BLOCK) + N_EXPERTS
    nblk = (counts + ROW_BLOCK - 1) // ROW_BLOCK
    blk_end = jnp.cumsum(nblk)
    blk_start = blk_end - nblk
    ntail_max = nblk_total - t * TOP_K // ROW_BLOCK
    fill_blocks = jnp.concatenate([jnp.maximum(blk_end - 1, 0),
                                   jnp.minimum(blk_end[-1] + jnp.arange(ntail_max), nblk_total - 1)])
    nfill = N_EXPERTS + nblk_total - blk_end[-1]
    max_chunks = -(-nblk_total // CHUNK_BLOCKS) + N_EXPERTS
    nch = (nblk + CHUNK_BLOCKS - 1) // CHUNK_BLOCKS
    ch_end = jnp.cumsum(nch)
    ch_start = ch_end - nch
    nvalid = ch_end[-1]
    cid = jnp.minimum(jnp.arange(max_chunks, dtype=I32), nvalid - 1)
    che =jnp.clip(jnp.searchsorted(ch_end, cid, side='right'), 0, N_EXPERTS - 1).astype(I32)
    local = cid - ch_start[che]
    nch_e = jnp.maximum(nch[che], 1)
    n_mm = (nblk[che] + MM_ROW_BLOCKS - 1) // MM_ROW_BLOCKS
    size = n_mm // nch_e
    extra = n_mm % nch_e
    first_mm = local * size + jnp.minimum(local, extra)
    chblk = (blk_start[che] + first_mm * MM_ROW_BLOCKS).astype(I32)
    chn = jnp.minimum((size + (local < extra)) * MM_ROW_BLOCKS,
                      nblk[che] - first_mm * MM_ROW_BLOCKS).astype(I32)
    return (fill_blocks.astype(I32), nfill.astype(I32).reshape(1), che, chblk, chn,
            nvalid.astype(I32).reshape(1), nblk_total * ROW_BLOCK)


def kernel(x, c, w_ada, b_ada, norm1_g, w_in, lam_q1, lam_k1, lam_q2, lam_k2, subln_g, w_out_a, w_out_b, w_o, norm2_g, w_router, b_router, w_gate, b_gate, w_up, b_up, w_down, b_down, final_g):
    batch, seq, d = x.shape
    t = batch * seq
    depth = w_ada.shape[0]
    wa = N_HEADS_A * HEAD_DIM
    wb = N_HEADS_B * 2 * HEAD_DIM
    slopes = _alibi_slopes(N_HEADS_A + N_HEADS_B)
    slopes_a = jnp.asarray(slopes[:N_HEADS_A])
    slopes_b = jnp.asarray(slopes[N_HEADS_A:])
    c8 = jnp.pad(c, ((0, 8 - batch), (0, 0)))

    h = x.reshape(t, d)
    for l in range(depth):
        lambda_init = 0.8 - 0.6 * math.exp(-0.3 * l)
        mod = _ada(c8, w_ada[l], b_ada[l].reshape(1, -1))[:batch]
        mod3 = mod.reshape(batch * N_MOD, 1, d)

        u = _norm_mod(h, norm1_g[l].reshape(1, d), mod3, 1, 0, seq)
        proj = _inproj(u, w_in[l], q_tiles=(0, 3 * wa // 1024))
        y_a = _dil_attention(proj, slopes_a, batch, seq, 0, wa // HEAD_DIM, 2 * wa // HEAD_DIM)
        off_b = 3 * wa // (2 * HEAD_DIM)
        nb = wb // (2 * HEAD_DIM)
        y_b = _diff_attention(proj, slopes_b,
                              [v[l].reshape(1, HEAD_DIM) for v in (lam_q1, lam_k1, lam_q2, lam_k2)],
                              subln_g[l].reshape(1, -1), lambda_init, batch, seq,
                              off_b, off_b + nb, off_b + 2 * nb)
        wr_hi, wr_lo = _split_bf16(w_router[l].T)
        gate_col = (3 * wa + 3 * wb) // d
        h, u2p, logits_t = _merge(y_a, y_b, proj, h, mod3, norm2_g[l].reshape(1, d),
                                  w_out_a[l].astype(BF16), w_out_b[l].astype(BF16),
                                  w_o[l].astype(BF16), wr_hi, wr_lo, b_router[l].reshape(-1, 1),
                                  seq, gate_col, gate_col + 1)
        probs, dest, cnt = _route(logits_t)
        (fill_blocks, nfill, che, chblk, chn, nvalid, total_rows) = _routing_plan(cnt[:, 0], t)
        xs = _dispatch(dest.reshape(-1), fill_blocks, nfill, u2p, total_rows)
        ys = _moe(che, chblk, chn, nvalid, xs, w_gate[l], w_up[l], w_down[l],
                  b_gate[l].reshape(N_EXPERTS, 1, -1), b_up[l].reshape(N_EXPERTS, 1, -1),
                  b_down[l].reshape(N_EXPERTS, 1, -1))
        h = _combine(dest.reshape(-1), ys, h, probs.T, mod3, final_g.reshape(1, d), seq,
                     final_norm=(l == depth - 1))
    return h.reshape(batch, seq, d)
```
